```python
import jax, jax.numpy as jnp
from jax import lax
import numpy as np

D_MODEL = 1024
BATCH = 8
SEQ = 4096
DEPTH = 1

D_MIX = D_MODEL
D_MLSTM = D_MIX // 2
D_POOL = D_MIX - D_MLSTM
N_MLSTM_HEADS = 4
HEAD_DIM = D_MLSTM // N_MLSTM_HEADS
POOL_WINDOWS = (2, 4, 8, 16)
N_POOL_GROUPS = len(POOL_WINDOWS)
POOL_GROUP_DIM = D_POOL // N_POOL_GROUPS
N_DIRS = 2
N_GATE_COLS = N_DIRS * 2 * N_MLSTM_HEADS
D_IN_PROJ = 4 * D_MLSTM + N_GATE_COLS + D_POOL
D_FF = 4 * D_MODEL
CHUNK = 128
EPS = 1e-6

kernel_name = "hybrid_mlstm_pool_adaln_encoder"


def _rmsnorm(x, g):
    xf = x.astype(jnp.float32)
    y = xf * lax.rsqrt(jnp.mean(xf * xf, axis=-1, keepdims=True) + EPS)
    return (y * g.astype(jnp.float32)).astype(x.dtype)


def _modulate(h, shift, scale):
    return h * (1 + scale[:, None, :]) + shift[:, None, :]


def _mlstm_scan(q, k, v, log_i, log_f):
    B, H, S, Dh = q.shape
    nc = S // CHUNK

    def to_chunks(a):
        a = a.reshape((B, H, nc, CHUNK) + a.shape[3:])
        return jnp.moveaxis(a, 2, 0)

    xs = (to_chunks(q), to_chunks(k), to_chunks(v), to_chunks(log_f), to_chunks(log_i))
    tril = jnp.tril(jnp.ones((CHUNK, CHUNK), dtype=bool))

    def step(carry, inp):
        C, n, m = carry
        q_c, k_c, v_c, lf, li = inp
        b = jnp.cumsum(lf, axis=-1)
        d = b[..., :, None] - b[..., None, :] + li[..., None, :]
        d = jnp.where(tril, d, -jnp.inf)
        inter = b + m[..., None]
        m_t = jnp.maximum(inter, jnp.max(d, axis=-1))
        w = jnp.exp(d - m_t[..., None])
        a = jnp.exp(inter - m_t)
        s = jnp.einsum('bhtd,bhsd->bhts', q_c, k_c) * w
        num = (jnp.einsum('bhts,bhse->bhte', s, v_c)
               + a[..., None] * jnp.einsum('bhtd,bhde->bhte', q_c, C))
        den = jnp.sum(s, axis=-1) + a * jnp.einsum('bhtd,bhd->bht', q_c, n)
        h = num / jnp.maximum(jnp.abs(den), jnp.exp(-m_t))[..., None]
        b_last = b[..., -1]
        g = b_last[..., None] - b + li
        m_new = jnp.maximum(b_last + m, jnp.max(g, axis=-1))
        decay = jnp.exp(b_last + m - m_new)
        wk = jnp.exp(g - m_new[..., None])
        C_new = decay[..., None, None] * C + jnp.einsum('bhs,bhsd,bhse->bhde', wk, k_c, v_c)
        n_new = decay[..., None] * n + jnp.einsum('bhs,bhsd->bhd', wk, k_c)
        return (C_new, n_new, m_new), h

    init = (jnp.zeros((B, H, Dh, Dh), jnp.float32),
            jnp.zeros((B, H, Dh), jnp.float32),
            jnp.zeros((B, H), jnp.float32))
    _, hs = lax.scan(step, init, xs)
    return jnp.moveaxis(hs, 0, 2).reshape(B, H, S, Dh)


def _multiscale_pool(u, w_pool, pool_scale):
    B, S, _ = u.shape
    uf = u.astype(jnp.float32).reshape(B, S, N_POOL_GROUPS, POOL_GROUP_DIM)
    prefix = jnp.concatenate([jnp.zeros_like(uf[:, :1]), jnp.cumsum(uf, axis=1)], axis=1)
    t = jnp.arange(S)
    means = []
    for gi, win in enumerate(POOL_WINDOWS):
        lo = jnp.clip(t - win // 2, 0, S)
        hi = jnp.clip(t + win // 2, 0, S)
        p = prefix[:, :, gi]
        cnt = (hi - lo).astype(jnp.float32)
        means.append((p[:, hi] - p[:, lo]) / cnt[None, :, None])
    pooled = jnp.stack(means, axis=2)
    mixed = (pooled - uf).astype(u.dtype)
    y = jnp.einsum('bsgc,gce->bsge', mixed, w_pool).reshape(B, S, D_POOL)
    return y * pool_scale


def _mixer(h, w_in, b_igate, b_fgate, g_head, w_pool, pool_scale, w_out):
    B, S, _ = h.shape
    H = N_MLSTM_HEADS
    proj = jnp.einsum('bsd,dp->bsp', h, w_in)
    cuts = [D_MLSTM, 2 * D_MLSTM, 3 * D_MLSTM, 4 * D_MLSTM, 4 * D_MLSTM + N_GATE_COLS]
    q, k, v, o, gates, u = jnp.split(proj, cuts, axis=-1)

    def heads(a):
        return a.reshape(B, S, H, HEAD_DIM).transpose(0, 2, 1, 3).astype(jnp.float32)

    qh = heads(q) * (HEAD_DIM ** -0.5)
    kh = heads(k)
    vh = heads(v)
    gates = gates.astype(jnp.float32).reshape(B, S, N_DIRS, 2, H).transpose(2, 3, 0, 4, 1)
    log_i = gates[:, 0] + b_igate.astype(jnp.float32)[:, None, :, None]
    log_f = jax.nn.log_sigmoid(gates[:, 1] + b_fgate.astype(jnp.float32)[:, None, :, None])

    def flip(a):
        return jnp.flip(a, axis=2)

    h_fwd = _mlstm_scan(qh, kh, vh, log_i[0], log_f[0])
    h_bwd = flip(_mlstm_scan(flip(qh), flip(kh), flip(vh), flip(log_i[1]), flip(log_f[1])))
    hm = h_fwd + h_bwd
    hm = hm * lax.rsqrt(jnp.mean(hm * hm, axis=-1, keepdims=True) + EPS)
    hm = hm * g_head.astype(jnp.float32).reshape(H, 1, HEAD_DIM)
    hm = hm.transpose(0, 2, 1, 3).reshape(B, S, D_MLSTM)
    y_a = (jax.nn.sigmoid(o.astype(jnp.float32)) * hm).astype(h.dtype)

    y_b = _multiscale_pool(u, w_pool, pool_scale)
    y = jnp.concatenate([y_a, y_b], axis=-1)
    return jnp.einsum('bsm,md->bsd', y, w_out)


def setup_inputs(seed: int = 0) -> dict:
    key = jax.random.key(seed)
    ks = jax.random.split(key, 17)
    f32 = jnp.float32
    H = N_MLSTM_HEADS

    def nrm(k, shape, scale):
        return jax.random.normal(k, shape, f32) * scale

    x = nrm(ks[0], (BATCH, SEQ, D_MODEL), 1.0)
    c = nrm(ks[1], (BATCH, D_MODEL), 1.0)
    w_ada = nrm(ks[2], (DEPTH, D_MODEL, 6 * D_MODEL), D_MODEL ** -0.5)
    b_ada = nrm(ks[3], (DEPTH, 6 * D_MODEL), 0.02)
    g_mix = 1.0 + nrm(ks[4], (DEPTH, D_MODEL), 0.05)
    w_in = nrm(ks[5], (DEPTH, D_MODEL, D_IN_PROJ), D_MODEL ** -0.5)
    b_igate = nrm(ks[6], (DEPTH, N_DIRS, H), 0.1)
    b_fgate = (jnp.broadcast_to(jnp.linspace(3.0, 6.0, H, dtype=f32), (DEPTH, N_DIRS, H))
               + nrm(ks[7], (DEPTH, N_DIRS, H), 0.1))
    g_head = 1.0 + nrm(ks[8], (DEPTH, D_MLSTM), 0.05)
    w_pool = nrm(ks[9], (DEPTH, N_POOL_GROUPS, POOL_GROUP_DIM, POOL_GROUP_DIM), POOL_GROUP_DIM ** -0.5)
    pool_scale = 1.0 + nrm(ks[10], (DEPTH, D_POOL), 0.1)
    w_out = nrm(ks[11], (DEPTH, D_MIX, D_MODEL), D_MIX ** -0.5)
    g_ffn = 1.0 + nrm(ks[12], (DEPTH, D_MODEL), 0.05)
    w_ff1 = nrm(ks[13], (DEPTH, D_MODEL, D_FF), D_MODEL ** -0.5)
    w_ff2 = nrm(ks[14], (DEPTH, D_FF, D_MODEL), D_FF ** -0.5)
    g_final = 1.0 + nrm(ks[15], (D_MODEL,), 0.05)
    return {"x": x, "c": c, "w_ada": w_ada, "b_ada": b_ada, "g_mix": g_mix, "w_in": w_in,
            "b_igate": b_igate, "b_fgate": b_fgate, "g_head": g_head, "w_pool": w_pool,
            "pool_scale": pool_scale, "w_out": w_out, "g_ffn": g_ffn, "w_ff1": w_ff1,
            "w_ff2": w_ff2, "g_final": g_final}


def reference(x, c, w_ada, b_ada, g_mix, w_in, b_igate, b_fgate, g_head, w_pool,
              pool_scale, w_out, g_ffn, w_ff1, w_ff2, g_final):
    c_act = jax.nn.silu(c)
    for l in range(DEPTH):
        mod = c_act @ w_ada[l] + b_ada[l]
        sh1, sc1, gt1, sh2, sc2, gt2 = jnp.split(mod, 6, axis=-1)
        h = _modulate(_rmsnorm(x, g_mix[l]), sh1, sc1)
        x = x + gt1[:, None, :] * _mixer(h, w_in[l], b_igate[l], b_fgate[l], g_head[l],
                                         w_pool[l], pool_scale[l], w_out[l])
        h = _modulate(_rmsnorm(x, g_ffn[l]), sh2, sc2)
        a = jnp.square(jax.nn.relu(jnp.einsum('bsd,df->bsf', h, w_ff1[l])))
        x = x + gt2[:, None, :] * jnp.einsum('bsf,fd->bsd', a, w_ff2[l])
    return _rmsnorm(x, g_final)
```

```python
import functools

import jax
import jax.numpy as jnp
from jax import lax
from jax.experimental import pallas as pl
from jax.experimental.pallas import tpu as pltpu

F32 = jnp.float32
BF16 = jnp.bfloat16

LANES = 128
BF16_SUBLANES = 16
VMEM_LIMIT_BYTES = 56 * 1024 * 1024

N_HEADS = 4
HEAD_DIM = 128
CHUNK = 128
POOL_WINDOWS = (2, 4, 8, 16)
POOL_HALO = 16
N_DIRS = 2
N_GATE_ROWS = 2 * N_DIRS
EPS = 1e-6

INPROJ_ROWS = 1024
OUTFFN_ROWS = 512
FF_COLS = 1024
AUG_ROWS = HEAD_DIM + BF16_SUBLANES

ROW_A, ROW_M, ROW_AINTER, ROW_EXPNEG, ROW_WK, ROW_DECAY = range(6)
COEF_ROWS = 8


def _const_spec(shape):
    zeros = (0,) * len(shape)
    return pl.BlockSpec(shape, lambda *_: zeros, pipeline_mode=pl.Buffered(1))


def _rms_scale(x):
    return x * lax.rsqrt(jnp.mean(x * x, axis=-1, keepdims=True) + EPS)


def _adaln_kernel(c_ref, w_ref, b_ref, o_ref):
    c = c_ref[...]
    act = c * jax.nn.sigmoid(c)
    o_ref[...] = jnp.dot(act.astype(BF16), w_ref[...].astype(BF16),
                         preferred_element_type=F32) + b_ref[...]


def _adaln(c, w_ada, b_ada):
    batch, d = c.shape
    n = w_ada.shape[1]
    return pl.pallas_call(
        _adaln_kernel,
        grid=(n // d,),
        in_specs=[pl.BlockSpec((batch, d), lambda j: (0, 0)),
                  pl.BlockSpec((d, d), lambda j: (0, j)),
                  pl.BlockSpec((1, d), lambda j: (0, j))],
        out_specs=pl.BlockSpec((batch, d), lambda j: (0, j)),
        out_shape=jax.ShapeDtypeStruct((batch, n), F32),
        compiler_params=pltpu.CompilerParams(dimension_semantics=("arbitrary",),
                                             vmem_limit_bytes=VMEM_LIMIT_BYTES),
    )(c, w_ada, b_ada.reshape(1, n))


def _inproj_kernel(x_ref, mod_ref, g_ref, w_ref, qt_ref, k_ref, vt_ref, o_ref, u_ref, gates_ref):
    dm = N_HEADS * HEAD_DIM
    x = x_ref[0]
    shift = mod_ref[0, 0:1, :]
    scale = mod_ref[0, 1:2, :]
    h = (_rms_scale(x) * g_ref[...]) * (1.0 + scale) + shift
    hb = h.astype(BF16)

    def proj(lo, width):
        return jnp.dot(hb, w_ref[:, lo:lo + width], preferred_element_type=F32)

    q = proj(0, dm) * (HEAD_DIM ** -0.5)
    for hd in range(N_HEADS):
        qt_ref[0, hd] = q[:, hd * HEAD_DIM:(hd + 1) * HEAD_DIM].T.astype(BF16)
    k = proj(dm, dm)
    for hd in range(N_HEADS):
        k_ref[0, hd] = k[:, hd * HEAD_DIM:(hd + 1) * HEAD_DIM].astype(BF16)
    v = proj(2 * dm, dm)
    for hd in range(N_HEADS):
        vt_ref[0, hd] = v[:, hd * HEAD_DIM:(hd + 1) * HEAD_DIM].T.astype(BF16)
    o_ref[0] = proj(3 * dm, dm).astype(BF16)
    u_ref[0] = proj(4 * dm, u_ref.shape[2]).astype(BF16)
    gt = proj(4 * dm + u_ref.shape[2], LANES).T
    n_chunks = x.shape[0] // CHUNK
    for hd in range(N_HEADS):
        for r in range(N_GATE_ROWS):
            row = hd * N_GATE_ROWS + r
            for j in range(n_chunks):
                gates_ref[0, hd, r, j:j + 1, :] = gt[row:row + 1, j * CHUNK:(j + 1) * CHUNK]


def _inproj(x, mod, g_mix, w_cat, d_pool):
    batch, seq, d = x.shape
    tm = INPROJ_ROWS
    dm = N_HEADS * HEAD_DIM
    grid = (batch, seq // tm)
    head_t = jax.ShapeDtypeStruct((batch, N_HEADS, HEAD_DIM, seq), BF16)
    head_n = jax.ShapeDtypeStruct((batch, N_HEADS, seq, HEAD_DIM), BF16)
    out_shape = (head_t, head_n, head_t,
                 jax.ShapeDtypeStruct((batch, seq, dm), BF16),
                 jax.ShapeDtypeStruct((batch, seq, d_pool), BF16),
                 jax.ShapeDtypeStruct((batch, N_HEADS, N_GATE_ROWS, seq // CHUNK, CHUNK), F32))
    spec_t = pl.BlockSpec((1, N_HEADS, HEAD_DIM, tm), lambda b, i: (b, 0, 0, i))
    spec_n = pl.BlockSpec((1, N_HEADS, tm, HEAD_DIM), lambda b, i: (b, 0, i, 0))
    out_specs = (spec_t, spec_n, spec_t,
                 pl.BlockSpec((1, tm, dm), lambda b, i: (b, i, 0)),
                 pl.BlockSpec((1, tm, d_pool), lambda b, i: (b, i, 0)),
                 pl.BlockSpec((1, N_HEADS, N_GATE_ROWS, tm // CHUNK, CHUNK), lambda b, i: (b, 0, 0, i, 0)))
    return pl.pallas_call(
        _inproj_kernel,
        grid=grid,
        in_specs=[pl.BlockSpec((1, tm, d), lambda b, i: (b, i, 0)),
                  pl.BlockSpec((1,) + mod.shape[1:], lambda b, i: (b, 0, 0)),
                  _const_spec((1, d)),
                  _const_spec(w_cat.shape)],
        out_specs=out_specs,
        out_shape=out_shape,
        compiler_params=pltpu.CompilerParams(dimension_semantics=("arbitrary", "arbitrary"),
                                             vmem_limit_bytes=VMEM_LIMIT_BYTES),
    )(x, mod, g_mix.reshape(1, d), w_cat)


def _lane_scan(x, op, fill, reverse):
    lane = lax.broadcasted_iota(jnp.int32, x.shape, 1)
    step = 1
    while step < CHUNK:
        if reverse:
            shifted = jnp.where(lane < CHUNK - step, pltpu.roll(x, CHUNK - step, axis=1), fill)
        else:
            shifted = jnp.where(lane >= step, pltpu.roll(x, step, axis=1), fill)
        x = op(x, shifted)
        step *= 2
    return x


def _mlstm_kernel(qt_ref, k_ref, vt_ref, gates_ref, gbias_ref, out_ref,
                  vaug_ref, coef_ref, mprev_ref, state_ref, ht_ref):
    n_chunks = gates_ref.shape[3]
    seq = n_chunks * CHUNK

    vaug_ref[0:HEAD_DIM, :] = vt_ref[0, 0]
    vaug_ref[HEAD_DIM:AUG_ROWS, :] = jnp.ones((AUG_ROWS - HEAD_DIM, seq), BF16)

    for d in range(N_DIRS):
        reverse = d == 1
        last = 0 if reverse else CHUNK - 1
        log_i = gates_ref[0, 0, 2 * d] + gbias_ref[0, 2 * d:2 * d + 1, :]
        z = gates_ref[0, 0, 2 * d + 1] + gbias_ref[0, 2 * d + 1:2 * d + 2, :]
        log_f = -(jnp.maximum(-z, 0.0) + jnp.log1p(jnp.exp(-jnp.abs(z))))
        b = _lane_scan(log_f, jnp.add, 0.0, reverse)
        a = log_i - b
        cm = _lane_scan(a, jnp.maximum, -jnp.inf, reverse)
        b_last = jnp.broadcast_to(b[:, last:last + 1], b.shape)
        cm_last = jnp.broadcast_to(cm[:, last:last + 1], b.shape)
        m = jnp.zeros((1, CHUNK), F32)
        order = range(n_chunks - 1, -1, -1) if reverse else range(n_chunks)
        for c in order:
            mprev_ref[c:c + 1, :] = m
            m = b_last[c:c + 1, :] + jnp.maximum(m, cm_last[c:c + 1, :])
        m_prev = mprev_ref[...]
        m_row = jnp.maximum(m_prev, cm)
        a_inter = jnp.exp(m_prev - m_row)
        exp_neg = jnp.exp(-(b + m_row))
        m_end = jnp.broadcast_to(m_row[:, last:last + 1], b.shape)
        wk = jnp.exp(a - m_end)
        decay = jnp.broadcast_to(a_inter[:, last:last + 1], b.shape)
        rows = {ROW_A: a, ROW_M: m_row, ROW_AINTER: a_inter, ROW_EXPNEG: exp_neg,
                ROW_WK: wk, ROW_DECAY: decay}
        for c in range(n_chunks):
            for r, val in rows.items():
                coef_ref[d, c, r:r + 1, :] = val[c:c + 1, :]

    state_ref[...] = jnp.zeros(state_ref.shape, F32)
    key_pos = lax.broadcasted_iota(jnp.int32, (CHUNK, CHUNK), 0)
    qry_pos = lax.broadcasted_iota(jnp.int32, (CHUNK, CHUNK), 1)
    visible = (key_pos <= qry_pos, key_pos >= qry_pos)

    def one_chunk(d, c):
        off = pl.multiple_of(c * CHUNK, CHUNK)
        coef = coef_ref[d, c]
        a_row = coef[ROW_A:ROW_A + 1, :]
        m_row = coef[ROW_M:ROW_M + 1, :]
        a_inter = coef[ROW_AINTER:ROW_AINTER + 1, :]
        exp_neg = coef[ROW_EXPNEG:ROW_EXPNEG + 1, :]
        wk = coef[ROW_WK:ROW_WK + 1, :]
        decay = coef[ROW_DECAY:ROW_DECAY + 1, :]
        qt_c = qt_ref[0, 0, :, pl.ds(off, CHUNK)]
        k_c = k_ref[0, 0, pl.ds(off, CHUNK), :]
        vt_c = vaug_ref[:, pl.ds(off, CHUNK)]
        state = state_ref[d]

        scores = jnp.dot(k_c, qt_c, preferred_element_type=F32)
        a_col = jnp.broadcast_to(a_row, (CHUNK, CHUNK)).T
        log_w = jnp.where(visible[d], a_col - m_row, -jnp.inf)
        st = (scores * jnp.exp(log_w)).astype(BF16)
        intra = jnp.dot(vt_c, st, preferred_element_type=F32)
        inter = jnp.dot(state.astype(BF16), qt_c, preferred_element_type=F32)
        numden = intra + a_inter * inter
        den = numden[HEAD_DIM:HEAD_DIM + 1, :]
        ht = numden[0:HEAD_DIM, :] * (1.0 / jnp.maximum(jnp.abs(den), exp_neg))
        ht_ref[d, :, pl.ds(off, CHUNK)] = ht

        vw = (vt_c.astype(F32) * wk).astype(BF16)
        update = jnp.dot(vw, k_c, preferred_element_type=F32)
        state_ref[d] = decay[:, 0:1] * state + update

    def body(c, carry):
        one_chunk(0, c)
        one_chunk(1, n_chunks - 1 - c)
        return carry

    lax.fori_loop(0, n_chunks, body, 0)

    def finish(c, carry):
        off = pl.multiple_of(c * CHUNK, CHUNK)
        hsum = ht_ref[0, :, pl.ds(off, CHUNK)] + ht_ref[1, :, pl.ds(off, CHUNK)]
        out_ref[0, 0, pl.ds(off, CHUNK), :] = hsum.T.astype(BF16)
        return carry

    lax.fori_loop(0, n_chunks, finish, 0)


def _mlstm(qt, k, vt, gates, gbias):
    batch, heads, hd, seq = qt.shape
    n_chunks = seq // CHUNK
    spec_t = pl.BlockSpec((1, 1, hd, seq), lambda b, h: (b, h, 0, 0))
    spec_n = pl.BlockSpec((1, 1, seq, hd), lambda b, h: (b, h, 0, 0))
    return pl.pallas_call(
        _mlstm_kernel,
        grid=(batch, heads),
        in_specs=[spec_t, spec_n, spec_t,
                  pl.BlockSpec((1, 1, N_GATE_ROWS, n_chunks, CHUNK), lambda b, h: (b, h, 0, 0, 0)),
                  pl.BlockSpec((1, N_GATE_ROWS, CHUNK), lambda b, h: (h, 0, 0))],
        out_specs=spec_n,
        out_shape=jax.ShapeDtypeStruct((batch, heads, seq, hd), BF16),
        scratch_shapes=[pltpu.VMEM((AUG_ROWS, seq), BF16),
                        pltpu.VMEM((N_DIRS, n_chunks, COEF_ROWS, CHUNK), F32),
                        pltpu.VMEM((n_chunks, CHUNK), F32),
                        pltpu.VMEM((N_DIRS, AUG_ROWS, hd), F32),
                        pltpu.VMEM((N_DIRS, hd, seq), F32)],
        compiler_params=pltpu.CompilerParams(dimension_semantics=("arbitrary", "arbitrary"),
                                             vmem_limit_bytes=VMEM_LIMIT_BYTES),
    )(qt, k, vt, gates, gbias)


def _outffn_kernel(x_ref, hm_ref, o_ref, u_ref, uprev_ref, unext_ref, mod_ref,
                   ghead_ref, pscale_ref, gffn_ref, gfinal_ref,
                   wpool_ref, wout_ref, wff1_ref, wff2_ref, out_ref,
                   ubuf_ref, ycat_ref, *, seq, final_norm):
    tm = x_ref.shape[1]
    dm = N_HEADS * HEAD_DIM
    tile = pl.program_id(1)
    n_tiles = pl.num_programs(1)

    for hd in range(N_HEADS):
        cols = slice(hd * HEAD_DIM, (hd + 1) * HEAD_DIM)
        hn = _rms_scale(hm_ref[0, hd].astype(F32)) * ghead_ref[:, cols]
        gate = jax.nn.sigmoid(o_ref[0, :, cols].astype(F32))
        ycat_ref[:, cols] = (gate * hn).astype(BF16)

    ubuf_ref[0:POOL_HALO, :] = jnp.where(tile > 0, uprev_ref[0].astype(F32), 0.0)
    ubuf_ref[POOL_HALO:POOL_HALO + tm, :] = u_ref[0].astype(F32)
    ubuf_ref[POOL_HALO + tm:, :] = jnp.where(tile < n_tiles - 1, unext_ref[0].astype(F32), 0.0)
    pos = tile * tm + lax.broadcasted_iota(jnp.int32, (tm, HEAD_DIM), 0)
    for gi, win in enumerate(POOL_WINDOWS):
        half = win // 2
        cols = slice(gi * HEAD_DIM, (gi + 1) * HEAD_DIM)
        total = ubuf_ref[POOL_HALO - half:POOL_HALO - half + tm, cols]
        for j in range(1 - half, half):
            total = total + ubuf_ref[POOL_HALO + j:POOL_HALO + j + tm, cols]
        cnt = jnp.minimum(pos + half, seq) - jnp.maximum(pos - half, 0)
        mixed = total / cnt.astype(F32) - ubuf_ref[POOL_HALO:POOL_HALO + tm, cols]
        yb = jnp.dot(mixed.astype(BF16), wpool_ref[gi], preferred_element_type=F32)
        ycat_ref[:, dm + gi * HEAD_DIM:dm + (gi + 1) * HEAD_DIM] = (yb * pscale_ref[:, cols]).astype(BF16)

    gate1 = mod_ref[0, 2:3, :]
    x1 = x_ref[0] + gate1 * jnp.dot(ycat_ref[...], wout_ref[...], preferred_element_type=F32)

    shift2 = mod_ref[0, 3:4, :]
    scale2 = mod_ref[0, 4:5, :]
    gate2 = mod_ref[0, 5:6, :]
    h2 = ((_rms_scale(x1) * gffn_ref[...]) * (1.0 + scale2) + shift2).astype(BF16)
    acc = jnp.zeros(x1.shape, F32)
    for j in range(wff1_ref.shape[1] // FF_COLS):
        hid = jnp.dot(h2, wff1_ref[:, j * FF_COLS:(j + 1) * FF_COLS], preferred_element_type=F32)
        hid = jnp.square(jnp.maximum(hid, 0.0)).astype(BF16)
        acc = acc + jnp.dot(hid, wff2_ref[j * FF_COLS:(j + 1) * FF_COLS, :], preferred_element_type=F32)
    x2 = x1 + gate2 * acc
    if final_norm:
        x2 = _rms_scale(x2) * gfinal_ref[...]
    out_ref[0] = x2


def _outffn(x, hm, o, u, mod, g_head, pool_scale, g_ffn, g_final, w_pool, w_out, w_ff1, w_ff2,
            final_norm):
    batch, seq, d = x.shape
    tm = OUTFFN_ROWS
    dm = N_HEADS * HEAD_DIM
    d_pool = u.shape[2]
    halo_per_tile = tm // POOL_HALO
    n_halo_blocks = seq // POOL_HALO
    kernel = functools.partial(_outffn_kernel, seq=seq, final_norm=final_norm)
    return pl.pallas_call(
        kernel,
        grid=(batch, seq // tm),
        in_specs=[pl.BlockSpec((1, tm, d), lambda b, i: (b, i, 0)),
                  pl.BlockSpec((1, N_HEADS, tm, HEAD_DIM), lambda b, i: (b, 0, i, 0)),
                  pl.BlockSpec((1, tm, dm), lambda b, i: (b, i, 0)),
                  pl.BlockSpec((1, tm, d_pool), lambda b, i: (b, i, 0)),
                  pl.BlockSpec((1, POOL_HALO, d_pool),
                               lambda b, i: (b, jnp.maximum(i * halo_per_tile - 1, 0), 0)),
                  pl.BlockSpec((1, POOL_HALO, d_pool),
                               lambda b, i: (b, jnp.minimum((i + 1) * halo_per_tile, n_halo_blocks - 1), 0)),
                  pl.BlockSpec((1,) + mod.shape[1:], lambda b, i: (b, 0, 0)),
                  _const_spec((1, dm)),
                  _const_spec((1, d_pool)),
                  _const_spec((1, d)),
                  _const_spec((1, d)),
                  _const_spec(w_pool.shape),
                  _const_spec(w_out.shape),
                  _const_spec(w_ff1.shape),
                  _const_spec(w_ff2.shape)],
        out_specs=pl.BlockSpec((1, tm, d), lambda b, i: (b, i, 0)),
        out_shape=jax.ShapeDtypeStruct((batch, seq, d), F32),
        scratch_shapes=[pltpu.VMEM((tm + 2 * POOL_HALO, d_pool), F32),
                        pltpu.VMEM((tm, dm + d_pool), BF16)],
        compiler_params=pltpu.CompilerParams(dimension_semantics=("arbitrary", "arbitrary"),
                                             vmem_limit_bytes=VMEM_LIMIT_BYTES),
    )(x, hm, o, u, u, u, mod, g_head.reshape(1, dm), pool_scale.reshape(1, d_pool),
      g_ffn.reshape(1, d), g_final.reshape(1, d), w_pool, w_out, w_ff1, w_ff2)


def _pack_w_in(w_in):
    dm = N_HEADS * HEAD_DIM
    n_gates = N_DIRS * 2 * N_HEADS
    main = w_in[:, :4 * dm]
    gates = w_in[:, 4 * dm:4 * dm + n_gates]
    pool = w_in[:, 4 * dm + n_gates:]
    gates = gates.reshape(-1, N_DIRS, 2, N_HEADS).transpose(0, 3, 1, 2).reshape(-1, n_gates)
    gates = jnp.pad(gates, ((0, 0), (0, LANES - n_gates)))
    return jnp.concatenate([main, pool, gates], axis=1).astype(BF16)


def _pack_gate_bias(b_igate, b_fgate):
    rows = jnp.stack([b_igate, b_fgate], axis=1)
    rows = rows.transpose(2, 0, 1).reshape(N_HEADS, N_GATE_ROWS)
    return jnp.broadcast_to(rows[:, :, None], (N_HEADS, N_GATE_ROWS, CHUNK)).astype(F32)


def kernel(x, c, w_ada, b_ada, g_mix, w_in, b_igate, b_fgate, g_head, w_pool, pool_scale, w_out,
           g_ffn, w_ff1, w_ff2, g_final):
    depth = w_ada.shape[0]
    batch, seq, d = x.shape
    d_pool = pool_scale.shape[1]
    assert g_head.shape[1] == N_HEADS * HEAD_DIM and w_pool.shape[1] == len(POOL_WINDOWS)
    assert seq % INPROJ_ROWS == 0 and seq % OUTFFN_ROWS == 0 and w_ff1.shape[2] % FF_COLS == 0
    for l in range(depth):
        mod = _adaln(c, w_ada[l], b_ada[l]).reshape(batch, 6, d)
        qt, k, vt, o, u, gates = _inproj(x, mod, g_mix[l], _pack_w_in(w_in[l]), d_pool)
        hm = _mlstm(qt, k, vt, gates, _pack_gate_bias(b_igate[l], b_fgate[l]))
        x = _outffn(x, hm, o, u, mod, g_head[l], pool_scale[l], g_ffn[l], g_final,
                    w_pool[l].astype(BF16), w_out[l].astype(BF16), w_ff1[l].astype(BF16),
                    w_ff2[l].astype(BF16), final_norm=(l == depth - 1))
    return x
```

```python
import functools

import jax
import jax.numpy as jnp
from jax import lax
from jax.experimental import pallas as pl
from jax.experimental.pallas import tpu as pltpu

F32 = jnp.float32
BF16 = jnp.bfloat16

LANES = 128
BF16_SUBLANES = 16
VMEM_LIMIT_BYTES = 56 * 1024 * 1024

N_HEADS = 4
HEAD_DIM = 128
CHUNK = 128
POOL_WINDOWS = (2, 4, 8, 16)
POOL_HALO = 16
N_DIRS = 2
N_GATE_ROWS = 2 * N_DIRS
EPS = 1e-6

INPROJ_ROWS = 1024
OUTFFN_ROWS = 512
FF_COLS = 1024
AUG_ROWS = HEAD_DIM + BF16_SUBLANES

ROW_A, ROW_M, ROW_AINTER, ROW_EXPNEG, ROW_WK, ROW_DECAY = range(6)
N_COEF = 6
CHUNKS_PER_STEP = 4


def _const_spec(shape):
    zeros = (0,) * len(shape)
    return pl.BlockSpec(shape, lambda *_: zeros, pipeline_mode=pl.Buffered(1))


def _rms_scale(x):
    return x * lax.rsqrt(jnp.mean(x * x, axis=-1, keepdims=True) + EPS)


def _adaln_kernel(c_ref, w_ref, b_ref, o_ref):
    c = c_ref[...]
    act = c * jax.nn.sigmoid(c)
    o_ref[...] = jnp.dot(act.astype(BF16), w_ref[...].astype(BF16),
                         preferred_element_type=F32) + b_ref[...]


def _adaln(c, w_ada, b_ada):
    batch, d = c.shape
    n = w_ada.shape[1]
    return pl.pallas_call(
        _adaln_kernel,
        grid=(n // d,),
        in_specs=[pl.BlockSpec((batch, d), lambda j: (0, 0)),
                  pl.BlockSpec((d, d), lambda j: (0, j)),
                  pl.BlockSpec((1, d), lambda j: (0, j))],
        out_specs=pl.BlockSpec((batch, d), lambda j: (0, j)),
        out_shape=jax.ShapeDtypeStruct((batch, n), F32),
        compiler_params=pltpu.CompilerParams(dimension_semantics=("arbitrary",),
                                             vmem_limit_bytes=VMEM_LIMIT_BYTES),
    )(c, w_ada, b_ada.reshape(1, n))


def _inproj_kernel(x_ref, mod_ref, g_ref, w_ref, qt_ref, k_ref, vt_ref, o_ref, u_ref, gates_ref):
    dm = N_HEADS * HEAD_DIM
    x = x_ref[0]
    shift = mod_ref[0, 0:1, :]
    scale = mod_ref[0, 1:2, :]
    h = (_rms_scale(x) * g_ref[...]) * (1.0 + scale) + shift
    hb = h.astype(BF16)

    def proj(lo, width):
        return jnp.dot(hb, w_ref[:, lo:lo + width], preferred_element_type=F32)

    q = proj(0, dm) * (HEAD_DIM ** -0.5)
    for hd in range(N_HEADS):
        qt_ref[0, hd] = q[:, hd * HEAD_DIM:(hd + 1) * HEAD_DIM].T.astype(BF16)
    k = proj(dm, dm)
    for hd in range(N_HEADS):
        k_ref[0, hd] = k[:, hd * HEAD_DIM:(hd + 1) * HEAD_DIM].astype(BF16)
    v = proj(2 * dm, dm)
    for hd in range(N_HEADS):
        vt_ref[0, hd, 0:HEAD_DIM, :] = v[:, hd * HEAD_DIM:(hd + 1) * HEAD_DIM].T.astype(BF16)
        vt_ref[0, hd, HEAD_DIM:AUG_ROWS, :] = jnp.ones((AUG_ROWS - HEAD_DIM, x.shape[0]), BF16)
    o_ref[0] = proj(3 * dm, dm).astype(BF16)
    u_ref[0] = proj(4 * dm, u_ref.shape[2]).astype(BF16)
    gt = proj(4 * dm + u_ref.shape[2], LANES).T
    n_chunks = x.shape[0] // CHUNK
    for hd in range(N_HEADS):
        for r in range(N_GATE_ROWS):
            row = hd * N_GATE_ROWS + r
            for j in range(n_chunks):
                gates_ref[0, hd, r, j:j + 1, :] = gt[row:row + 1, j * CHUNK:(j + 1) * CHUNK]


def _inproj(x, mod, g_mix, w_cat, d_pool):
    batch, seq, d = x.shape
    tm = INPROJ_ROWS
    dm = N_HEADS * HEAD_DIM
    grid = (batch, seq // tm)
    head_t = jax.ShapeDtypeStruct((batch, N_HEADS, HEAD_DIM, seq), BF16)
    head_n = jax.ShapeDtypeStruct((batch, N_HEADS, seq, HEAD_DIM), BF16)
    head_aug = jax.ShapeDtypeStruct((batch, N_HEADS, AUG_ROWS, seq), BF16)
    out_shape = (head_t, head_n, head_aug,
                 jax.ShapeDtypeStruct((batch, seq, dm), BF16),
                 jax.ShapeDtypeStruct((batch, seq, d_pool), BF16),
                 jax.ShapeDtypeStruct((batch, N_HEADS, N_GATE_ROWS, seq // CHUNK, CHUNK), F32))
    spec_t = pl.BlockSpec((1, N_HEADS, HEAD_DIM, tm), lambda b, i: (b, 0, 0, i))
    spec_n = pl.BlockSpec((1, N_HEADS, tm, HEAD_DIM), lambda b, i: (b, 0, i, 0))
    spec_aug = pl.BlockSpec((1, N_HEADS, AUG_ROWS, tm), lambda b, i: (b, 0, 0, i))
    out_specs = (spec_t, spec_n, spec_aug,
                 pl.BlockSpec((1, tm, dm), lambda b, i: (b, i, 0)),
                 pl.BlockSpec((1, tm, d_pool), lambda b, i: (b, i, 0)),
                 pl.BlockSpec((1, N_HEADS, N_GATE_ROWS, tm // CHUNK, CHUNK), lambda b, i: (b, 0, 0, i, 0)))
    return pl.pallas_call(
        _inproj_kernel,
        grid=grid,
        in_specs=[pl.BlockSpec((1, tm, d), lambda b, i: (b, i, 0)),
                  pl.BlockSpec((1,) + mod.shape[1:], lambda b, i: (b, 0, 0)),
                  _const_spec((1, d)),
                  _const_spec(w_cat.shape)],
        out_specs=out_specs,
        out_shape=out_shape,
        compiler_params=pltpu.CompilerParams(dimension_semantics=("arbitrary", "arbitrary"),
                                             vmem_limit_bytes=VMEM_LIMIT_BYTES),
    )(x, mod, g_mix.reshape(1, d), w_cat)


def _lane_scan(x, op, fill, reverse):
    lane = lax.broadcasted_iota(jnp.int32, x.shape, 1)
    step = 1
    while step < CHUNK:
        if reverse:
            shifted = jnp.where(lane < CHUNK - step, pltpu.roll(x, CHUNK - step, axis=1), fill)
        else:
            shifted = jnp.where(lane >= step, pltpu.roll(x, step, axis=1), fill)
        x = op(x, shifted)
        step *= 2
    return x


def _mlstm_kernel(qt_ref, k_ref, vt_ref, gates_ref, gbias_ref, out_ref,
                  coef_ref, mprev_ref, sprev_ref, st_ref):
    n_chunks = gates_ref.shape[3]

    for d in range(N_DIRS):
        reverse = d == 1
        last = 0 if reverse else CHUNK - 1
        log_i = gates_ref[0, 0, 2 * d] + gbias_ref[0, 2 * d:2 * d + 1, :]
        z = gates_ref[0, 0, 2 * d + 1] + gbias_ref[0, 2 * d + 1:2 * d + 2, :]
        log_f = -(jnp.maximum(-z, 0.0) + jnp.log1p(jnp.exp(-jnp.abs(z))))
        b = _lane_scan(log_f, jnp.add, 0.0, reverse)
        a = log_i - b
        cm = _lane_scan(a, jnp.maximum, -jnp.inf, reverse)
        b_last = jnp.broadcast_to(b[:, last:last + 1], b.shape)
        cm_last = jnp.broadcast_to(cm[:, last:last + 1], b.shape)
        m = jnp.zeros((1, CHUNK), F32)
        order = range(n_chunks - 1, -1, -1) if reverse else range(n_chunks)
        for c in order:
            mprev_ref[d, c:c + 1, :] = m
            m = b_last[c:c + 1, :] + jnp.maximum(m, cm_last[c:c + 1, :])
        m_prev = mprev_ref[d]
        m_row = jnp.maximum(m_prev, cm)
        a_inter = jnp.exp(m_prev - m_row)
        exp_neg = jnp.exp(-(b + m_row))
        m_end = jnp.broadcast_to(m_row[:, last:last + 1], b.shape)
        wk = jnp.exp(a - m_end)
        decay = jnp.broadcast_to(a_inter[:, last:last + 1], b.shape)
        for r, val in ((ROW_A, a), (ROW_M, m_row), (ROW_AINTER, a_inter), (ROW_EXPNEG, exp_neg),
                       (ROW_WK, wk), (ROW_DECAY, decay)):
            coef_ref[d, r] = val

    def coef_row(d, r, c):
        return coef_ref[d, r, pl.ds(c, 1), :]

    def chunk_operands(c):
        off = pl.multiple_of(c * CHUNK, CHUNK)
        return off, qt_ref[0, 0, :, pl.ds(off, CHUNK)], k_ref[0, 0, pl.ds(off, CHUNK), :], \
            vt_ref[0, 0, :, pl.ds(off, CHUNK)]

    key_pos = lax.broadcasted_iota(jnp.int32, (CHUNK, CHUNK), 0)
    qry_pos = lax.broadcasted_iota(jnp.int32, (CHUNK, CHUNK), 1)
    visible = (key_pos <= qry_pos, key_pos >= qry_pos)

    def weighted_scores(c):
        _, qt_c, k_c, _ = chunk_operands(c)
        scores = jnp.dot(k_c, qt_c, preferred_element_type=F32)
        for d in range(N_DIRS):
            a_col = jnp.broadcast_to(coef_row(d, ROW_A, c), (CHUNK, CHUNK)).T
            weight = jnp.exp(jnp.where(visible[d], a_col - coef_row(d, ROW_M, c), -jnp.inf))
            st_ref[d, c] = (scores * weight).astype(BF16)

    def state_step(d, c, state):
        _, _, k_c, vt_c = chunk_operands(c)
        sprev_ref[d, c] = state.astype(BF16)
        vw = (vt_c.astype(F32) * coef_row(d, ROW_WK, c)).astype(BF16)
        return coef_row(d, ROW_DECAY, c) * state + jnp.dot(vw, k_c, preferred_element_type=F32)

    def state_body(i, states):
        s_fwd, s_bwd = states
        for j in range(CHUNKS_PER_STEP):
            c = i * CHUNKS_PER_STEP + j
            weighted_scores(c)
            s_fwd = state_step(0, c, s_fwd)
            s_bwd = state_step(1, n_chunks - 1 - c, s_bwd)
        return s_fwd, s_bwd

    zero_state = jnp.zeros((AUG_ROWS, HEAD_DIM), F32)
    lax.fori_loop(0, n_chunks // CHUNKS_PER_STEP, state_body, (zero_state, zero_state))

    def output_chunk(c):
        off, qt_c, _, vt_c = chunk_operands(c)
        hsum = None
        for d in range(N_DIRS):
            qa = (qt_c.astype(F32) * coef_row(d, ROW_AINTER, c)).astype(BF16)
            lhs = jnp.concatenate([vt_c, sprev_ref[d, c]], axis=1)
            rhs = jnp.concatenate([st_ref[d, c], qa], axis=0)
            numden = jnp.dot(lhs, rhs, preferred_element_type=F32)
            den = numden[HEAD_DIM:HEAD_DIM + 1, :]
            ht = numden[0:HEAD_DIM, :] * (1.0 / jnp.maximum(jnp.abs(den), coef_row(d, ROW_EXPNEG, c)))
            hsum = ht if hsum is None else hsum + ht
        out_ref[0, 0, pl.ds(off, CHUNK), :] = hsum.T.astype(BF16)

    def output_body(i, carry):
        for j in range(CHUNKS_PER_STEP):
            output_chunk(i * CHUNKS_PER_STEP + j)
        return carry

    lax.fori_loop(0, n_chunks // CHUNKS_PER_STEP, output_body, 0)


def _mlstm(qt, k, vt, gates, gbias):
    batch, heads, hd, seq = qt.shape
    n_chunks = seq // CHUNK
    assert n_chunks % CHUNKS_PER_STEP == 0
    spec_t = pl.BlockSpec((1, 1, hd, seq), lambda b, h: (b, h, 0, 0))
    spec_n = pl.BlockSpec((1, 1, seq, hd), lambda b, h: (b, h, 0, 0))
    spec_aug = pl.BlockSpec((1, 1, AUG_ROWS, seq), lambda b, h: (b, h, 0, 0))
    return pl.pallas_call(
        _mlstm_kernel,
        grid=(batch, heads),
        in_specs=[spec_t, spec_n, spec_aug,
                  pl.BlockSpec((1, 1, N_GATE_ROWS, n_chunks, CHUNK), lambda b, h: (b, h, 0, 0, 0)),
                  pl.BlockSpec((1, N_GATE_ROWS, CHUNK), lambda b, h: (h, 0, 0))],
        out_specs=spec_n,
        out_shape=jax.ShapeDtypeStruct((batch, heads, seq, hd), BF16),
        scratch_shapes=[pltpu.VMEM((N_DIRS, N_COEF, n_chunks, CHUNK), F32),
                        pltpu.VMEM((N_DIRS, n_chunks, CHUNK), F32),
                        pltpu.VMEM((N_DIRS, n_chunks, AUG_ROWS, hd), BF16),
                        pltpu.VMEM((N_DIRS, n_chunks, CHUNK, CHUNK), BF16)],
        compiler_params=pltpu.CompilerParams(dimension_semantics=("arbitrary", "arbitrary"),
                                             vmem_limit_bytes=VMEM_LIMIT_BYTES),
    )(qt, k, vt, gates, gbias)


def _outffn_kernel(x_ref, hm_ref, o_ref, u_ref, uprev_ref, unext_ref, mod_ref,
                   ghead_ref, pscale_ref, gffn_ref, gfinal_ref,
                   wpool_ref, wout_ref, wff1_ref, wff2_ref, out_ref,
                   ubuf_ref, ycat_ref, *, seq, final_norm):
    tm = x_ref.shape[1]
    dm = N_HEADS * HEAD_DIM
    tile = pl.program_id(1)
    n_tiles = pl.num_programs(1)

    for hd in range(N_HEADS):
        cols = slice(hd * HEAD_DIM, (hd + 1) * HEAD_DIM)
        hn = _rms_scale(hm_ref[0, hd].astype(F32)) * ghead_ref[:, cols]
        gate = jax.nn.sigmoid(o_ref[0, :, cols].astype(F32))
        ycat_ref[:, cols] = (gate * hn).astype(BF16)

    ubuf_ref[0:POOL_HALO, :] = jnp.where(tile > 0, uprev_ref[0].astype(F32), 0.0)
    ubuf_ref[POOL_HALO:POOL_HALO + tm, :] = u_ref[0].astype(F32)
    ubuf_ref[POOL_HALO + tm:, :] = jnp.where(tile < n_tiles - 1, unext_ref[0].astype(F32), 0.0)
    pos = tile * tm + lax.broadcasted_iota(jnp.int32, (tm, HEAD_DIM), 0)
    for gi, win in enumerate(POOL_WINDOWS):
        half = win // 2
        cols = slice(gi * HEAD_DIM, (gi + 1) * HEAD_DIM)
        total = ubuf_ref[POOL_HALO - half:POOL_HALO - half + tm, cols]
        for j in range(1 - half, half):
            total = total + ubuf_ref[POOL_HALO + j:POOL_HALO + j + tm, cols]
        cnt = jnp.minimum(pos + half, seq) - jnp.maximum(pos - half, 0)
        mixed = total / cnt.astype(F32) - ubuf_ref[POOL_HALO:POOL_HALO + tm, cols]
        yb = jnp.dot(mixed.astype(BF16), wpool_ref[gi], preferred_element_type=F32)
        ycat_ref[:, dm + gi * HEAD_DIM:dm + (gi + 1) * HEAD_DIM] = (yb * pscale_ref[:, cols]).astype(BF16)

    gate1 = mod_ref[0, 2:3, :]
    x1 = x_ref[0] + gate1 * jnp.dot(ycat_ref[...], wout_ref[...], preferred_element_type=F32)

    shift2 = mod_ref[0, 3:4, :]
    scale2 = mod_ref[0, 4:5, :]
    gate2 = mod_ref[0, 5:6, :]
    h2 = ((_rms_scale(x1) * gffn_ref[...]) * (1.0 + scale2) + shift2).astype(BF16)
    acc = jnp.zeros(x1.shape, F32)
    for j in range(wff1_ref.shape[1] // FF_COLS):
        hid = jnp.dot(h2, wff1_ref[:, j * FF_COLS:(j + 1) * FF_COLS], preferred_element_type=F32)
        hid = jnp.square(jnp.maximum(hid, 0.0)).astype(BF16)
        acc = acc + jnp.dot(hid, wff2_ref[j * FF_COLS:(j + 1) * FF_COLS, :], preferred_element_type=F32)
    x2 = x1 + gate2 * acc
    if final_norm:
        x2 = _rms_scale(x2) * gfinal_ref[...]
    out_ref[0] = x2


def _outffn(x, hm, o, u, mod, g_head, pool_scale, g_ffn, g_final, w_pool, w_out, w_ff1, w_ff2,
            final_norm):
    batch, seq, d = x.shape
    tm = OUTFFN_ROWS
    dm = N_HEADS * HEAD_DIM
    d_pool = u.shape[2]
    halo_per_tile = tm // POOL_HALO
    n_halo_blocks = seq // POOL_HALO
    kernel = functools.partial(_outffn_kernel, seq=seq, final_norm=final_norm)
    return pl.pallas_call(
        kernel,
        grid=(batch, seq // tm),
        in_specs=[pl.BlockSpec((1, tm, d), lambda b, i: (b, i, 0)),
                  pl.BlockSpec((1, N_HEADS, tm, HEAD_DIM), lambda b, i: (b, 0, i, 0)),
                  pl.BlockSpec((1, tm, dm), lambda b, i: (b, i, 0)),
                  pl.BlockSpec((1, tm, d_pool), lambda b, i: (b, i, 0)),
                  pl.BlockSpec((1, POOL_HALO, d_pool),
                               lambda b, i: (b, jnp.maximum(i * halo_per_tile - 1, 0), 0)),
                  pl.BlockSpec((1, POOL_HALO, d_pool),
                               lambda b, i: (b, jnp.minimum((i + 1) * halo_per_tile, n_halo_blocks - 1), 0)),
                  pl.BlockSpec((1,) + mod.shape[1:], lambda b, i: (b, 0, 0)),
                  _const_spec((1, dm)),
                  _const_spec((1, d_pool)),
                  _const_spec((1, d)),
                  _const_spec((1, d)),
                  _const_spec(w_pool.shape),
                  _const_spec(w_out.shape),
                  _const_spec(w_ff1.shape),
                  _const_spec(w_ff2.shape)],
        out_specs=pl.BlockSpec((1, tm, d), lambda b, i: (b, i, 0)),
        out_shape=jax.ShapeDtypeStruct((batch, seq, d), F32),
        scratch_shapes=[pltpu.VMEM((tm + 2 * POOL_HALO, d_pool), F32),
                        pltpu.VMEM((tm, dm + d_pool), BF16)],
        compiler_params=pltpu.CompilerParams(dimension_semantics=("arbitrary", "arbitrary"),
                                             vmem_limit_bytes=VMEM_LIMIT_BYTES),
    )(x, hm, o, u, u, u, mod, g_head.reshape(1, dm), pool_scale.reshape(1, d_pool),
      g_ffn.reshape(1, d), g_final.reshape(1, d), w_pool, w_out, w_ff1, w_ff2)


def _pack_w_in(w_in):
    dm = N_HEADS * HEAD_DIM
    n_gates = N_DIRS * 2 * N_HEADS
    main = w_in[:, :4 * dm]
    gates = w_in[:, 4 * dm:4 * dm + n_gates]
    pool = w_in[:, 4 * dm + n_gates:]
    gates = gates.reshape(-1, N_DIRS, 2, N_HEADS).transpose(0, 3, 1, 2).reshape(-1, n_gates)
    gates = jnp.pad(gates, ((0, 0), (0, LANES - n_gates)))
    return jnp.concatenate([main, pool, gates], axis=1).astype(BF16)


def _pack_gate_bias(b_igate, b_fgate):
    rows = jnp.stack([b_igate, b_fgate], axis=1)
    rows = rows.transpose(2, 0, 1).reshape(N_HEADS, N_GATE_ROWS)
    return jnp.broadcast_to(rows[:, :, None], (N_HEADS, N_GATE_ROWS, CHUNK)).astype(F32)


def kernel(x, c, w_ada, b_ada, g_mix, w_in, b_igate, b_fgate, g_head, w_pool, pool_scale, w_out,
           g_ffn, w_ff1, w_ff2, g_final):
    depth = w_ada.shape[0]
    batch, seq, d = x.shape
    d_pool = pool_scale.shape[1]
    assert g_head.shape[1] == N_HEADS * HEAD_DIM and w_pool.shape[1] == len(POOL_WINDOWS)
    assert seq % INPROJ_ROWS == 0 and seq % OUTFFN_ROWS == 0 and w_ff1.shape[2] % FF_COLS == 0
    for l in range(depth):
        mod = _adaln(c, w_ada[l], b_ada[l]).reshape(batch, 6, d)
        qt, k, vt, o, u, gates = _inproj(x, mod, g_mix[l], _pack_w_in(w_in[l]), d_pool)
        hm = _mlstm(qt, k, vt, gates, _pack_gate_bias(b_igate[l], b_fgate[l]))
        x = _outffn(x, hm, o, u, mod, g_head[l], pool_scale[l], g_ffn[l], g_final,
                    w_pool[l].astype(BF16), w_out[l].astype(BF16), w_ff1[l].astype(BF16),
                    w_ff2[l].astype(BF16), final_norm=(l == depth - 1))
    return x
```

```python
import functools

import jax
import jax.numpy as jnp
import numpy as np
from jax import lax
from jax.experimental import pallas as pl
from jax.experimental.pallas import tpu as pltpu

F32 = jnp.float32
BF16 = jnp.bfloat16

LANES = 128
BF16_SUBLANES = 16
VMEM_LIMIT_BYTES = 56 * 1024 * 1024

N_HEADS = 4
HEAD_DIM = 128
CHUNK = 128
POOL_WINDOWS = (2, 4, 8, 16)
POOL_BLOCK = 128
POOL_HALO = 16
N_DIRS = 2
N_GATE_ROWS = 2 * N_DIRS
EPS = 1e-6

INPROJ_ROWS = 1024
OUTFFN_ROWS = 512
FF_COLS = 1024
AUG_ROWS = HEAD_DIM + BF16_SUBLANES

ROW_A, ROW_M, ROW_AINTER, ROW_EXPNEG, ROW_WK, ROW_DECAY = range(6)
N_COEF = 6
CHUNKS_PER_STEP = 4


def _const_spec(shape):
    zeros = (0,) * len(shape)
    return pl.BlockSpec(shape, lambda *_: zeros, pipeline_mode=pl.Buffered(1))


def _rms_scale(x):
    return x * lax.rsqrt(jnp.mean(x * x, axis=-1, keepdims=True) + EPS)


def _adaln_kernel(c_ref, w_ref, b_ref, o_ref):
    c = c_ref[...]
    act = c * jax.nn.sigmoid(c)
    o_ref[...] = jnp.dot(act.astype(BF16), w_ref[...].astype(BF16),
                         preferred_element_type=F32) + b_ref[...]


def _adaln(c, w_ada, b_ada):
    batch, d = c.shape
    n = w_ada.shape[1]
    return pl.pallas_call(
        _adaln_kernel,
        grid=(n // d,),
        in_specs=[pl.BlockSpec((batch, d), lambda j: (0, 0)),
                  pl.BlockSpec((d, d), lambda j: (0, j)),
                  pl.BlockSpec((1, d), lambda j: (0, j))],
        out_specs=pl.BlockSpec((batch, d), lambda j: (0, j)),
        out_shape=jax.ShapeDtypeStruct((batch, n), F32),
        compiler_params=pltpu.CompilerParams(dimension_semantics=("arbitrary",),
                                             vmem_limit_bytes=VMEM_LIMIT_BYTES),
    )(c, w_ada, b_ada.reshape(1, n))


def _inproj_kernel(x_ref, mod_ref, g_ref, w_ref, qt_ref, k_ref, vt_ref, o_ref, u_ref, gates_ref):
    dm = N_HEADS * HEAD_DIM
    x = x_ref[0]
    shift = mod_ref[0, 0:1, :]
    scale = mod_ref[0, 1:2, :]
    h = (_rms_scale(x) * g_ref[...]) * (1.0 + scale) + shift
    hb = h.astype(BF16)

    def proj(lo, width):
        return jnp.dot(hb, w_ref[:, lo:lo + width], preferred_element_type=F32)

    q = proj(0, dm) * (HEAD_DIM ** -0.5)
    for hd in range(N_HEADS):
        qt_ref[0, hd] = q[:, hd * HEAD_DIM:(hd + 1) * HEAD_DIM].T.astype(BF16)
    k = proj(dm, dm)
    for hd in range(N_HEADS):
        k_ref[0, hd] = k[:, hd * HEAD_DIM:(hd + 1) * HEAD_DIM].astype(BF16)
    v = proj(2 * dm, dm)
    for hd in range(N_HEADS):
        vt_ref[0, hd, 0:HEAD_DIM, :] = v[:, hd * HEAD_DIM:(hd + 1) * HEAD_DIM].T.astype(BF16)
        vt_ref[0, hd, HEAD_DIM:AUG_ROWS, :] = jnp.ones((AUG_ROWS - HEAD_DIM, x.shape[0]), BF16)
    o_ref[0] = proj(3 * dm, dm).astype(BF16)
    u_ref[0] = proj(4 * dm, u_ref.shape[2]).astype(BF16)
    gt = proj(4 * dm + u_ref.shape[2], LANES).T
    n_chunks = x.shape[0] // CHUNK
    for hd in range(N_HEADS):
        for r in range(N_GATE_ROWS):
            row = hd * N_GATE_ROWS + r
            for j in range(n_chunks):
                gates_ref[0, hd, r, j:j + 1, :] = gt[row:row + 1, j * CHUNK:(j + 1) * CHUNK]


def _inproj(x, mod, g_mix, w_cat, d_pool):
    batch, seq, d = x.shape
    tm = INPROJ_ROWS
    dm = N_HEADS * HEAD_DIM
    grid = (batch, seq // tm)
    head_t = jax.ShapeDtypeStruct((batch, N_HEADS, HEAD_DIM, seq), BF16)
    head_n = jax.ShapeDtypeStruct((batch, N_HEADS, seq, HEAD_DIM), BF16)
    head_aug = jax.ShapeDtypeStruct((batch, N_HEADS, AUG_ROWS, seq), BF16)
    out_shape = (head_t, head_n, head_aug,
                 jax.ShapeDtypeStruct((batch, seq, dm), BF16),
                 jax.ShapeDtypeStruct((batch, seq, d_pool), BF16),
                 jax.ShapeDtypeStruct((batch, N_HEADS, N_GATE_ROWS, seq // CHUNK, CHUNK), F32))
    spec_t = pl.BlockSpec((1, N_HEADS, HEAD_DIM, tm), lambda b, i: (b, 0, 0, i))
    spec_n = pl.BlockSpec((1, N_HEADS, tm, HEAD_DIM), lambda b, i: (b, 0, i, 0))
    spec_aug = pl.BlockSpec((1, N_HEADS, AUG_ROWS, tm), lambda b, i: (b, 0, 0, i))
    out_specs = (spec_t, spec_n, spec_aug,
                 pl.BlockSpec((1, tm, dm), lambda b, i: (b, i, 0)),
                 pl.BlockSpec((1, tm, d_pool), lambda b, i: (b, i, 0)),
                 pl.BlockSpec((1, N_HEADS, N_GATE_ROWS, tm // CHUNK, CHUNK), lambda b, i: (b, 0, 0, i, 0)))
    return pl.pallas_call(
        _inproj_kernel,
        grid=grid,
        in_specs=[pl.BlockSpec((1, tm, d), lambda b, i: (b, i, 0)),
                  pl.BlockSpec((1,) + mod.shape[1:], lambda b, i: (b, 0, 0)),
                  _const_spec((1, d)),
                  _const_spec(w_cat.shape)],
        out_specs=out_specs,
        out_shape=out_shape,
        compiler_params=pltpu.CompilerParams(dimension_semantics=("arbitrary", "arbitrary"),
                                             vmem_limit_bytes=VMEM_LIMIT_BYTES),
    )(x, mod, g_mix.reshape(1, d), w_cat)


def _lane_scan(x, op, fill, reverse):
    lane = lax.broadcasted_iota(jnp.int32, x.shape, 1)
    step = 1
    while step < CHUNK:
        if reverse:
            shifted = jnp.where(lane < CHUNK - step, pltpu.roll(x, CHUNK - step, axis=1), fill)
        else:
            shifted = jnp.where(lane >= step, pltpu.roll(x, step, axis=1), fill)
        x = op(x, shifted)
        step *= 2
    return x


def _mlstm_kernel(qt_ref, k_ref, vt_ref, gates_ref, gbias_ref, out_ref,
                  coef_ref, mprev_ref, sprev_ref, st_ref):
    n_chunks = gates_ref.shape[3]

    for d in range(N_DIRS):
        reverse = d == 1
        last = 0 if reverse else CHUNK - 1
        log_i = gates_ref[0, 0, 2 * d] + gbias_ref[0, 2 * d:2 * d + 1, :]
        z = gates_ref[0, 0, 2 * d + 1] + gbias_ref[0, 2 * d + 1:2 * d + 2, :]
        log_f = -(jnp.maximum(-z, 0.0) + jnp.log1p(jnp.exp(-jnp.abs(z))))
        b = _lane_scan(log_f, jnp.add, 0.0, reverse)
        a = log_i - b
        cm = _lane_scan(a, jnp.maximum, -jnp.inf, reverse)
        b_last = jnp.broadcast_to(b[:, last:last + 1], b.shape)
        cm_last = jnp.broadcast_to(cm[:, last:last + 1], b.shape)
        m = jnp.zeros((1, CHUNK), F32)
        order = range(n_chunks - 1, -1, -1) if reverse else range(n_chunks)
        for c in order:
            mprev_ref[d, c:c + 1, :] = m
            m = b_last[c:c + 1, :] + jnp.maximum(m, cm_last[c:c + 1, :])
        m_prev = mprev_ref[d]
        m_row = jnp.maximum(m_prev, cm)
        a_inter = jnp.exp(m_prev - m_row)
        exp_neg = jnp.exp(-(b + m_row))
        m_end = jnp.broadcast_to(m_row[:, last:last + 1], b.shape)
        wk = jnp.exp(a - m_end)
        decay = jnp.broadcast_to(a_inter[:, last:last + 1], b.shape)
        for r, val in ((ROW_A, a), (ROW_M, m_row), (ROW_AINTER, a_inter), (ROW_EXPNEG, exp_neg),
                       (ROW_WK, wk), (ROW_DECAY, decay)):
            coef_ref[d, r] = val

    def coef_row(d, r, c):
        return coef_ref[d, r, pl.ds(c, 1), :]

    def chunk_operands(c):
        off = pl.multiple_of(c * CHUNK, CHUNK)
        return off, qt_ref[0, 0, :, pl.ds(off, CHUNK)], k_ref[0, 0, pl.ds(off, CHUNK), :], \
            vt_ref[0, 0, :, pl.ds(off, CHUNK)]

    key_pos = lax.broadcasted_iota(jnp.int32, (CHUNK, CHUNK), 0)
    qry_pos = lax.broadcasted_iota(jnp.int32, (CHUNK, CHUNK), 1)
    visible = (key_pos <= qry_pos, key_pos >= qry_pos)

    def weighted_scores(c):
        _, qt_c, k_c, _ = chunk_operands(c)
        scores = jnp.dot(k_c, qt_c, preferred_element_type=F32)
        for d in range(N_DIRS):
            a_col = jnp.broadcast_to(coef_row(d, ROW_A, c), (CHUNK, CHUNK)).T
            weight = jnp.exp(jnp.where(visible[d], a_col - coef_row(d, ROW_M, c), -jnp.inf))
            st_ref[d, c] = (scores * weight).astype(BF16)

    def state_step(d, c, state):
        _, _, k_c, vt_c = chunk_operands(c)
        sprev_ref[d, c] = state.astype(BF16)
        vw = (vt_c.astype(F32) * coef_row(d, ROW_WK, c)).astype(BF16)
        return coef_row(d, ROW_DECAY, c) * state + jnp.dot(vw, k_c, preferred_element_type=F32)

    def state_body(i, states):
        s_fwd, s_bwd = states
        for j in range(CHUNKS_PER_STEP):
            c = i * CHUNKS_PER_STEP + j
            weighted_scores(c)
            s_fwd = state_step(0, c, s_fwd)
            s_bwd = state_step(1, n_chunks - 1 - c, s_bwd)
        return s_fwd, s_bwd

    zero_state = jnp.zeros((AUG_ROWS, HEAD_DIM), F32)
    lax.fori_loop(0, n_chunks // CHUNKS_PER_STEP, state_body, (zero_state, zero_state))

    def output_chunk(c):
        off, qt_c, _, vt_c = chunk_operands(c)
        hsum = None
        for d in range(N_DIRS):
            qa = (qt_c.astype(F32) * coef_row(d, ROW_AINTER, c)).astype(BF16)
            lhs = jnp.concatenate([vt_c, sprev_ref[d, c]], axis=1)
            rhs = jnp.concatenate([st_ref[d, c], qa], axis=0)
            numden = jnp.dot(lhs, rhs, preferred_element_type=F32)
            den = numden[HEAD_DIM:HEAD_DIM + 1, :]
            ht = numden[0:HEAD_DIM, :] * (1.0 / jnp.maximum(jnp.abs(den), coef_row(d, ROW_EXPNEG, c)))
            hsum = ht if hsum is None else hsum + ht
        out_ref[0, 0, pl.ds(off, CHUNK), :] = hsum.T.astype(BF16)

    def output_body(i, carry):
        for j in range(CHUNKS_PER_STEP):
            output_chunk(i * CHUNKS_PER_STEP + j)
        return carry

    lax.fori_loop(0, n_chunks // CHUNKS_PER_STEP, output_body, 0)


def _mlstm(qt, k, vt, gates, gbias):
    batch, heads, hd, seq = qt.shape
    n_chunks = seq // CHUNK
    assert n_chunks % CHUNKS_PER_STEP == 0
    spec_t = pl.BlockSpec((1, 1, hd, seq), lambda b, h: (b, h, 0, 0))
    spec_n = pl.BlockSpec((1, 1, seq, hd), lambda b, h: (b, h, 0, 0))
    spec_aug = pl.BlockSpec((1, 1, AUG_ROWS, seq), lambda b, h: (b, h, 0, 0))
    return pl.pallas_call(
        _mlstm_kernel,
        grid=(batch, heads),
        in_specs=[spec_t, spec_n, spec_aug,
                  pl.BlockSpec((1, 1, N_GATE_ROWS, n_chunks, CHUNK), lambda b, h: (b, h, 0, 0, 0)),
                  pl.BlockSpec((1, N_GATE_ROWS, CHUNK), lambda b, h: (h, 0, 0))],
        out_specs=spec_n,
        out_shape=jax.ShapeDtypeStruct((batch, heads, seq, hd), BF16),
        scratch_shapes=[pltpu.VMEM((N_DIRS, N_COEF, n_chunks, CHUNK), F32),
                        pltpu.VMEM((N_DIRS, n_chunks, CHUNK), F32),
                        pltpu.VMEM((N_DIRS, n_chunks, AUG_ROWS, hd), BF16),
                        pltpu.VMEM((N_DIRS, n_chunks, CHUNK, CHUNK), BF16)],
        compiler_params=pltpu.CompilerParams(dimension_semantics=("arbitrary", "arbitrary"),
                                             vmem_limit_bytes=VMEM_LIMIT_BYTES),
    )(qt, k, vt, gates, gbias)


def _outffn_kernel(x_ref, hm_ref, o_ref, u_ref, uprev_ref, unext_ref, mod_ref,
                   ghead_ref, pscale_ref, gffn_ref, gfinal_ref,
                   band_ref, wpool_ref, wout_ref, wff1_ref, wff2_ref, out_ref,
                   ubuf_ref, ycat_ref, *, seq, final_norm):
    tm = x_ref.shape[1]
    dm = N_HEADS * HEAD_DIM
    step = pl.program_id(0)
    n_tiles = pl.num_programs(0) - 1
    tiles_per_seq = seq // tm
    tile = jnp.minimum(step, n_tiles - 1) % tiles_per_seq

    @pl.when(step == 0)
    def _():
        ycat_ref[1] = jnp.zeros(ycat_ref.shape[1:], BF16)

    def stages(fill, drain):
        yb = []
        for gi in range(len(POOL_WINDOWS)):
            cols = slice(gi * HEAD_DIM, (gi + 1) * HEAD_DIM)
            mapped = jnp.dot(ycat_ref[drain, :, dm + gi * HEAD_DIM:dm + (gi + 1) * HEAD_DIM], wpool_ref[gi],
                             preferred_element_type=F32)
            yb.append((mapped * pscale_ref[:, cols]).astype(BF16))
        gate1 = mod_ref[0, 2:3, :]
        yproj = (jnp.dot(ycat_ref[drain, :, 0:dm], wout_ref[0:dm, :], preferred_element_type=F32)
                 + jnp.dot(jnp.concatenate(yb, axis=1), wout_ref[dm:, :], preferred_element_type=F32))
        x1 = x_ref[0] + gate1 * yproj

        shift2 = mod_ref[0, 3:4, :]
        scale2 = mod_ref[0, 4:5, :]
        gate2 = mod_ref[0, 5:6, :]
        h2 = ((_rms_scale(x1) * gffn_ref[...]) * (1.0 + scale2) + shift2).astype(BF16)
        acc = jnp.zeros(x1.shape, F32)
        for j in range(wff1_ref.shape[1] // FF_COLS):
            hid = jnp.dot(h2, wff1_ref[:, j * FF_COLS:(j + 1) * FF_COLS], preferred_element_type=F32)
            hid = jnp.square(jnp.maximum(hid, 0.0)).astype(BF16)
            acc = acc + jnp.dot(hid, wff2_ref[j * FF_COLS:(j + 1) * FF_COLS, :], preferred_element_type=F32)
        x2 = x1 + gate2 * acc
        if final_norm:
            x2 = _rms_scale(x2) * gfinal_ref[...]
        out_ref[0] = x2

        for hd in range(N_HEADS):
            cols = slice(hd * HEAD_DIM, (hd + 1) * HEAD_DIM)
            hn = _rms_scale(hm_ref[0, hd].astype(F32)) * ghead_ref[:, cols]
            gate = jax.nn.sigmoid(o_ref[0, :, cols].astype(F32))
            ycat_ref[fill, :, cols] = (gate * hn).astype(BF16)

        ubuf_ref[0:POOL_HALO, :] = jnp.where(tile > 0, uprev_ref[0], jnp.zeros_like(uprev_ref[0]))
        ubuf_ref[POOL_HALO:POOL_HALO + tm, :] = u_ref[0]
        ubuf_ref[POOL_HALO + tm:, :] = jnp.where(tile < tiles_per_seq - 1, unext_ref[0],
                                                 jnp.zeros_like(unext_ref[0]))
        for gi in range(len(POOL_WINDOWS)):
            cols = slice(gi * HEAD_DIM, (gi + 1) * HEAD_DIM)
            for r0 in range(0, tm, POOL_BLOCK):
                ext = ubuf_ref[r0:r0 + POOL_BLOCK + 2 * POOL_HALO, cols]
                mixed = jnp.dot(band_ref[gi], ext, preferred_element_type=F32)
                ycat_ref[fill, r0:r0 + POOL_BLOCK, dm + gi * HEAD_DIM:dm + (gi + 1) * HEAD_DIM] = mixed.astype(BF16)

        def clipped_rows(r0):
            ext = ubuf_ref[r0:r0 + 3 * POOL_HALO, :].astype(F32)
            pos = tile * tm + r0 + lax.broadcasted_iota(jnp.int32, (POOL_HALO, HEAD_DIM), 0)
            for gi, win in enumerate(POOL_WINDOWS):
                half = win // 2
                grp = ext[:, gi * HEAD_DIM:(gi + 1) * HEAD_DIM]
                total = grp[POOL_HALO - half:2 * POOL_HALO - half]
                for j in range(1 - half, half):
                    total = total + grp[POOL_HALO + j:2 * POOL_HALO + j]
                cnt = jnp.minimum(pos + half, seq) - jnp.maximum(pos - half, 0)
                mixed = total / cnt.astype(F32) - grp[POOL_HALO:2 * POOL_HALO]
                ycat_ref[fill, r0:r0 + POOL_HALO, dm + gi * HEAD_DIM:dm + (gi + 1) * HEAD_DIM] = mixed.astype(BF16)

        @pl.when(tile == 0)
        def _():
            clipped_rows(0)

        @pl.when(tile == tiles_per_seq - 1)
        def _():
            clipped_rows(tm - POOL_HALO)

    @pl.when(step % 2 == 0)
    def _():
        stages(0, 1)

    @pl.when(step % 2 == 1)
    def _():
        stages(1, 0)


def _pool_bands():
    row = np.arange(POOL_BLOCK)[:, None]
    col = np.arange(POOL_BLOCK + 2 * POOL_HALO)[None, :] - POOL_HALO
    bands = [((col >= row - win // 2) & (col < row + win // 2)) / win - (col == row) for win in POOL_WINDOWS]
    return jnp.asarray(np.stack(bands), dtype=BF16)


def _outffn(x, hm, o, u, mod, g_head, pool_scale, g_ffn, g_final, w_pool, w_out, w_ff1, w_ff2,
            final_norm):
    batch, seq, d = x.shape
    tm = OUTFFN_ROWS
    dm = N_HEADS * HEAD_DIM
    d_pool = u.shape[2]
    halo_per_tile = tm // POOL_HALO
    n_halo_blocks = seq // POOL_HALO
    tiles_per_seq = seq // tm
    n_tiles = batch * tiles_per_seq

    def mm_tile(g):
        t = jnp.maximum(g - 1, 0)
        return t // tiles_per_seq, t % tiles_per_seq

    def ew_tile(g):
        t = jnp.minimum(g, n_tiles - 1)
        return t // tiles_per_seq, t % tiles_per_seq

    def mm_rows(g):
        b, i = mm_tile(g)
        return b, i, 0

    def ew_rows(g):
        b, i = ew_tile(g)
        return b, i, 0

    def ew_heads(g):
        b, i = ew_tile(g)
        return b, 0, i, 0

    def ew_prev(g):
        b, i = ew_tile(g)
        return b, jnp.maximum(i * halo_per_tile - 1, 0), 0

    def ew_next(g):
        b, i = ew_tile(g)
        return b, jnp.minimum((i + 1) * halo_per_tile, n_halo_blocks - 1), 0

    bands = _pool_bands()
    kernel = functools.partial(_outffn_kernel, seq=seq, final_norm=final_norm)
    return pl.pallas_call(
        kernel,
        grid=(n_tiles + 1,),
        in_specs=[pl.BlockSpec((1, tm, d), mm_rows),
                  pl.BlockSpec((1, N_HEADS, tm, HEAD_DIM), ew_heads),
                  pl.BlockSpec((1, tm, dm), ew_rows),
                  pl.BlockSpec((1, tm, d_pool), ew_rows),
                  pl.BlockSpec((1, POOL_HALO, d_pool), ew_prev),
                  pl.BlockSpec((1, POOL_HALO, d_pool), ew_next),
                  pl.BlockSpec((1,) + mod.shape[1:], lambda g: (mm_tile(g)[0], 0, 0)),
                  _const_spec((1, dm)),
                  _const_spec((1, d_pool)),
                  _const_spec((1, d)),
                  _const_spec((1, d)),
                  _const_spec(bands.shape),
                  _const_spec(w_pool.shape),
                  _const_spec(w_out.shape),
                  _const_spec(w_ff1.shape),
                  _const_spec(w_ff2.shape)],
        out_specs=pl.BlockSpec((1, tm, d), mm_rows),
        out_shape=jax.ShapeDtypeStruct((batch, seq, d), F32),
        scratch_shapes=[pltpu.VMEM((tm + 2 * POOL_HALO, d_pool), BF16),
                        pltpu.VMEM((2, tm, dm + d_pool), BF16)],
        compiler_params=pltpu.CompilerParams(dimension_semantics=("arbitrary",),
                                             vmem_limit_bytes=VMEM_LIMIT_BYTES),
    )(x, hm, o, u, u, u, mod, g_head.reshape(1, dm), pool_scale.reshape(1, d_pool),
      g_ffn.reshape(1, d), g_final.reshape(1, d), bands, w_pool, w_out, w_ff1, w_ff2)


def _pack_w_in(w_in):
    dm = N_HEADS * HEAD_DIM
    n_gates = N_DIRS * 2 * N_HEADS
    main = w_in[:, :4 * dm]
    gates = w_in[:, 4 * dm:4 * dm + n_gates]
    pool = w_in[:, 4 * dm + n_gates:]
    gates = gates.reshape(-1, N_DIRS, 2, N_HEADS).transpose(0, 3, 1, 2).reshape(-1, n_gates)
    gates = jnp.pad(gates, ((0, 0), (0, LANES - n_gates)))
    return jnp.concatenate([main, pool, gates], axis=1).astype(BF16)


def _pack_gate_bias(b_igate, b_fgate):
    rows = jnp.stack([b_igate, b_fgate], axis=1)
    rows = rows.transpose(2, 0, 1).reshape(N_HEADS, N_GATE_ROWS)
    return jnp.broadcast_to(rows[:, :, None], (N_HEADS, N_GATE_ROWS, CHUNK)).astype(F32)


def kernel(x, c, w_ada, b_ada, g_mix, w_in, b_igate, b_fgate, g_head, w_pool, pool_scale, w_out,
           g_ffn, w_ff1, w_ff2, g_final):
    depth = w_ada.shape[0]
    batch, seq, d = x.shape
    d_pool = pool_scale.shape[1]
    assert g_head.shape[1] == N_HEADS * HEAD_DIM and w_pool.shape[1] == len(POOL_WINDOWS)
    assert seq % INPROJ_ROWS == 0 and seq % OUTFFN_ROWS == 0 and w_ff1.shape[2] % FF_COLS == 0
    for l in range(depth):
        mod = _adaln(c, w_ada[l], b_ada[l]).reshape(batch, 6, d)
        qt, k, vt, o, u, gates = _inproj(x, mod, g_mix[l], _pack_w_in(w_in[l]), d_pool)
        hm = _mlstm(qt, k, vt, gates, _pack_gate_bias(b_igate[l], b_fgate[l]))
        x = _outffn(x, hm, o, u, mod, g_head[l], pool_scale[l], g_ffn[l], g_final,
                    w_pool[l].astype(BF16), w_out[l].astype(BF16), w_ff1[l].astype(BF16),
                    w_ff2[l].astype(BF16), final_norm=(l == depth - 1))
    return x
```

```python
import functools

import jax
import jax.numpy as jnp
import numpy as np
from jax import lax
from jax.experimental import pallas as pl
from jax.experimental.pallas import tpu as pltpu

F32 = jnp.float32
BF16 = jnp.bfloat16

LANES = 128
BF16_SUBLANES = 16
VMEM_LIMIT_BYTES = 56 * 1024 * 1024

N_HEADS = 4
HEAD_DIM = 128
CHUNK = 128
POOL_WINDOWS = (2, 4, 8, 16)
POOL_BLOCK = 128
POOL_HALO = 16
N_DIRS = 2
N_GATE_ROWS = 2 * N_DIRS
EPS = 1e-6
LOG2_E = 1.4426950408889634

INPROJ_ROWS = 1024
OUTFFN_ROWS = 512
OUTFFN_SLAB = 512
FF_COLS = 1024
AUG_ROWS = HEAD_DIM + BF16_SUBLANES

ROW_A, ROW_M, ROW_AINTER, ROW_EXPNEG, ROW_WK, ROW_DECAY = range(6)
N_COEF = 6
CHUNKS_PER_STEP = 16


def _const_spec(shape):
    zeros = (0,) * len(shape)
    return pl.BlockSpec(shape, lambda *_: zeros, pipeline_mode=pl.Buffered(1))


def _rms_scale(x):
    return x * lax.rsqrt(jnp.mean(x * x, axis=-1, keepdims=True) + EPS)


def _adaln_kernel(c_ref, w_ref, b_ref, o_ref):
    c = c_ref[...]
    act = c * jax.nn.sigmoid(c)
    o_ref[...] = jnp.dot(act.astype(BF16), w_ref[...].astype(BF16),
                         preferred_element_type=F32) + b_ref[...]


def _adaln(c, w_ada, b_ada):
    batch, d = c.shape
    n = w_ada.shape[1]
    return pl.pallas_call(
        _adaln_kernel,
        grid=(n // d,),
        in_specs=[pl.BlockSpec((batch, d), lambda j: (0, 0)),
                  pl.BlockSpec((d, d), lambda j: (0, j)),
                  pl.BlockSpec((1, d), lambda j: (0, j))],
        out_specs=pl.BlockSpec((batch, d), lambda j: (0, j)),
        out_shape=jax.ShapeDtypeStruct((batch, n), F32),
        compiler_params=pltpu.CompilerParams(dimension_semantics=("arbitrary",),
                                             vmem_limit_bytes=VMEM_LIMIT_BYTES),
    )(c, w_ada, b_ada.reshape(1, n))


def _inproj_kernel(x_ref, mod_ref, g_ref, w_ref, qt_ref, k_ref, vt_ref, o_ref, u_ref, gates_ref):
    dm = N_HEADS * HEAD_DIM
    x = x_ref[0]
    shift = mod_ref[0, 0:1, :]
    scale = mod_ref[0, 1:2, :]
    h = (_rms_scale(x) * g_ref[...]) * (1.0 + scale) + shift
    hb = h.astype(BF16)

    def proj(lo, width):
        return jnp.dot(hb, w_ref[:, lo:lo + width], preferred_element_type=F32)

    q = proj(0, dm) * (HEAD_DIM ** -0.5)
    for hd in range(N_HEADS):
        qt_ref[0, hd] = q[:, hd * HEAD_DIM:(hd + 1) * HEAD_DIM].T.astype(BF16)
    k = proj(dm, dm)
    for hd in range(N_HEADS):
        k_ref[0, hd] = k[:, hd * HEAD_DIM:(hd + 1) * HEAD_DIM].astype(BF16)
    v = proj(2 * dm, dm)
    for hd in range(N_HEADS):
        vt_ref[0, hd, 0:HEAD_DIM, :] = v[:, hd * HEAD_DIM:(hd + 1) * HEAD_DIM].T.astype(BF16)
        vt_ref[0, hd, HEAD_DIM:AUG_ROWS, :] = jnp.ones((AUG_ROWS - HEAD_DIM, x.shape[0]), BF16)
    o_ref[0] = proj(3 * dm, dm).astype(BF16)
    u_ref[0] = proj(4 * dm, u_ref.shape[2]).astype(BF16)
    gt = proj(4 * dm + u_ref.shape[2], LANES).T
    n_chunks = x.shape[0] // CHUNK
    for hd in range(N_HEADS):
        for r in range(N_GATE_ROWS):
            row = hd * N_GATE_ROWS + r
            for j in range(n_chunks):
                gates_ref[0, hd, r, j:j + 1, :] = gt[row:row + 1, j * CHUNK:(j + 1) * CHUNK]


def _inproj(x, mod, g_mix, w_cat, d_pool):
    batch, seq, d = x.shape
    tm = INPROJ_ROWS
    dm = N_HEADS * HEAD_DIM
    grid = (batch, seq // tm)
    head_t = jax.ShapeDtypeStruct((batch, N_HEADS, HEAD_DIM, seq), BF16)
    head_n = jax.ShapeDtypeStruct((batch, N_HEADS, seq, HEAD_DIM), BF16)
    head_aug = jax.ShapeDtypeStruct((batch, N_HEADS, AUG_ROWS, seq), BF16)
    out_shape = (head_t, head_n, head_aug,
                 jax.ShapeDtypeStruct((batch, seq, dm), BF16),
                 jax.ShapeDtypeStruct((batch, seq, d_pool), BF16),
                 jax.ShapeDtypeStruct((batch, N_HEADS, N_GATE_ROWS, seq // CHUNK, CHUNK), F32))
    spec_t = pl.BlockSpec((1, N_HEADS, HEAD_DIM, tm), lambda b, i: (b, 0, 0, i))
    spec_n = pl.BlockSpec((1, N_HEADS, tm, HEAD_DIM), lambda b, i: (b, 0, i, 0))
    spec_aug = pl.BlockSpec((1, N_HEADS, AUG_ROWS, tm), lambda b, i: (b, 0, 0, i))
    out_specs = (spec_t, spec_n, spec_aug,
                 pl.BlockSpec((1, tm, dm), lambda b, i: (b, i, 0)),
                 pl.BlockSpec((1, tm, d_pool), lambda b, i: (b, i, 0)),
                 pl.BlockSpec((1, N_HEADS, N_GATE_ROWS, tm // CHUNK, CHUNK), lambda b, i: (b, 0, 0, i, 0)))
    return pl.pallas_call(
        _inproj_kernel,
        grid=grid,
        in_specs=[pl.BlockSpec((1, tm, d), lambda b, i: (b, i, 0)),
                  pl.BlockSpec((1,) + mod.shape[1:], lambda b, i: (b, 0, 0)),
                  _const_spec((1, d)),
                  _const_spec(w_cat.shape)],
        out_specs=out_specs,
        out_shape=out_shape,
        compiler_params=pltpu.CompilerParams(dimension_semantics=("arbitrary", "arbitrary"),
                                             vmem_limit_bytes=VMEM_LIMIT_BYTES),
    )(x, mod, g_mix.reshape(1, d), w_cat)


def _time_scan(x, op, fill, reverse):
    row = lax.broadcasted_iota(jnp.int32, x.shape, 0)
    step = 1
    while step < CHUNK:
        if reverse:
            shifted = jnp.where(row < CHUNK - step, pltpu.roll(x, CHUNK - step, axis=0), fill)
        else:
            shifted = jnp.where(row >= step, pltpu.roll(x, step, axis=0), fill)
        x = op(x, shifted)
        step *= 2
    return x


def _mlstm_kernel(qt_ref, k_ref, vt_ref, gates_ref, gbias_ref, out_ref,
                  coef_ref, mprev_ref, sprev_ref, st_ref):
    n_chunks = gates_ref.shape[3]
    assert 2 * N_DIRS * n_chunks == LANES

    log_i, log_f = [], []
    for d in range(N_DIRS):
        log_i.append(gates_ref[0, 0, 2 * d] + gbias_ref[0, 2 * d:2 * d + 1, :])
        z = gates_ref[0, 0, 2 * d + 1] + gbias_ref[0, 2 * d + 1:2 * d + 2, :]
        log_f.append(-(jnp.maximum(-z, 0.0) + jnp.log1p(jnp.exp(-jnp.abs(z)))))
    by_time = jnp.concatenate(log_f + log_i, axis=0).T
    lane = lax.broadcasted_iota(jnp.int32, by_time.shape, 1)
    fwd_lane = (lane % (N_DIRS * n_chunks)) < n_chunks
    b_t = jnp.where(fwd_lane, _time_scan(by_time, jnp.add, 0.0, False),
                    _time_scan(by_time, jnp.add, 0.0, True))
    a_t = pltpu.roll(by_time, N_DIRS * n_chunks, axis=1) - b_t
    cm_t = jnp.where(fwd_lane, _time_scan(a_t, jnp.maximum, -jnp.inf, False),
                     _time_scan(a_t, jnp.maximum, -jnp.inf, True))
    b_rows, a_rows, cm_rows = b_t.T, a_t.T, cm_t.T

    for d in range(N_DIRS):
        reverse = d == 1
        last = 0 if reverse else CHUNK - 1
        b = b_rows[d * n_chunks:(d + 1) * n_chunks]
        a = a_rows[d * n_chunks:(d + 1) * n_chunks]
        cm = cm_rows[d * n_chunks:(d + 1) * n_chunks]
        b_last = jnp.broadcast_to(b[:, last:last + 1], b.shape)
        cm_last = jnp.broadcast_to(cm[:, last:last + 1], b.shape)
        m = jnp.zeros((1, CHUNK), F32)
        order = range(n_chunks - 1, -1, -1) if reverse else range(n_chunks)
        for c in order:
            mprev_ref[d, c:c + 1, :] = m
            m = b_last[c:c + 1, :] + jnp.maximum(m, cm_last[c:c + 1, :])
        m_prev = mprev_ref[d]
        m_row = jnp.maximum(m_prev, cm)
        a_inter = jnp.exp(m_prev - m_row)
        exp_neg = jnp.exp(-(b + m_row))
        m_end = jnp.broadcast_to(m_row[:, last:last + 1], b.shape)
        wk = jnp.exp(a - m_end)
        decay = jnp.broadcast_to(a_inter[:, last:last + 1], b.shape)
        for r, val in ((ROW_A, a * LOG2_E), (ROW_M, m_row * LOG2_E), (ROW_AINTER, a_inter), (ROW_EXPNEG, exp_neg),
                       (ROW_WK, wk), (ROW_DECAY, decay)):
            coef_ref[d, r] = val

    def coef_row(d, r, c):
        return coef_ref[d, r, pl.ds(c, 1), :]

    def chunk_operands(c):
        off = pl.multiple_of(c * CHUNK, CHUNK)
        return off, qt_ref[0, 0, :, pl.ds(off, CHUNK)], k_ref[0, 0, pl.ds(off, CHUNK), :], \
            vt_ref[0, 0, :, pl.ds(off, CHUNK)]

    key_pos = lax.broadcasted_iota(jnp.int32, (CHUNK, CHUNK), 0)
    qry_pos = lax.broadcasted_iota(jnp.int32, (CHUNK, CHUNK), 1)
    visible = (key_pos <= qry_pos, key_pos >= qry_pos)

    def weighted_scores(c):
        _, qt_c, k_c, _ = chunk_operands(c)
        scores = jnp.dot(k_c, qt_c, preferred_element_type=F32)
        for d in range(N_DIRS):
            a_col = jnp.broadcast_to(coef_row(d, ROW_A, c), (CHUNK, CHUNK)).T
            weight = jnp.exp2(jnp.where(visible[d], a_col - coef_row(d, ROW_M, c), -jnp.inf))
            st_ref[d, c] = (scores * weight).astype(BF16)

    def state_step(d, c, state):
        _, _, k_c, vt_c = chunk_operands(c)
        sprev_ref[d, c] = state.astype(BF16)
        vw = vt_c * coef_row(d, ROW_WK, c).astype(BF16)
        return coef_row(d, ROW_DECAY, c) * state + jnp.dot(vw, k_c, preferred_element_type=F32)

    def state_body(i, states):
        s_fwd, s_bwd = states
        for j in range(CHUNKS_PER_STEP):
            c = i * CHUNKS_PER_STEP + j
            weighted_scores(c)
            s_fwd = state_step(0, c, s_fwd)
            s_bwd = state_step(1, n_chunks - 1 - c, s_bwd)
        return s_fwd, s_bwd

    zero_state = jnp.zeros((AUG_ROWS, HEAD_DIM), F32)
    lax.fori_loop(0, n_chunks // CHUNKS_PER_STEP, state_body, (zero_state, zero_state))

    def output_chunk(c):
        off, qt_c, _, vt_c = chunk_operands(c)
        hsum = None
        for d in range(N_DIRS):
            qa = (qt_c.astype(F32) * coef_row(d, ROW_AINTER, c)).astype(BF16)
            lhs = jnp.concatenate([vt_c, sprev_ref[d, c]], axis=1)
            rhs = jnp.concatenate([st_ref[d, c], qa], axis=0)
            numden = jnp.dot(lhs, rhs, preferred_element_type=F32)
            den = numden[HEAD_DIM:HEAD_DIM + 1, :]
            ht = numden[0:HEAD_DIM, :] * (1.0 / jnp.maximum(jnp.abs(den), coef_row(d, ROW_EXPNEG, c)))
            hsum = ht if hsum is None else hsum + ht
        out_ref[0, 0, pl.ds(off, CHUNK), :] = hsum.T.astype(BF16)

    def output_body(i, carry):
        for j in range(CHUNKS_PER_STEP):
            output_chunk(i * CHUNKS_PER_STEP + j)
        return carry

    lax.fori_loop(0, n_chunks // CHUNKS_PER_STEP, output_body, 0)


def _mlstm(qt, k, vt, gates, gbias):
    batch, heads, hd, seq = qt.shape
    n_chunks = seq // CHUNK
    assert n_chunks % CHUNKS_PER_STEP == 0
    spec_t = pl.BlockSpec((1, 1, hd, seq), lambda b, h: (b, h, 0, 0))
    spec_n = pl.BlockSpec((1, 1, seq, hd), lambda b, h: (b, h, 0, 0))
    spec_aug = pl.BlockSpec((1, 1, AUG_ROWS, seq), lambda b, h: (b, h, 0, 0))
    return pl.pallas_call(
        _mlstm_kernel,
        grid=(batch, heads),
        in_specs=[spec_t, spec_n, spec_aug,
                  pl.BlockSpec((1, 1, N_GATE_ROWS, n_chunks, CHUNK), lambda b, h: (b, h, 0, 0, 0)),
                  pl.BlockSpec((1, N_GATE_ROWS, CHUNK), lambda b, h: (h, 0, 0))],
        out_specs=spec_n,
        out_shape=jax.ShapeDtypeStruct((batch, heads, seq, hd), BF16),
        scratch_shapes=[pltpu.VMEM((N_DIRS, N_COEF, n_chunks, CHUNK), F32),
                        pltpu.VMEM((N_DIRS, n_chunks, CHUNK), F32),
                        pltpu.VMEM((N_DIRS, n_chunks, AUG_ROWS, hd), BF16),
                        pltpu.VMEM((N_DIRS, n_chunks, CHUNK, CHUNK), BF16)],
        compiler_params=pltpu.CompilerParams(dimension_semantics=("arbitrary", "arbitrary"),
                                             vmem_limit_bytes=VMEM_LIMIT_BYTES),
    )(qt, k, vt, gates, gbias)


def _outffn_kernel(x_ref, hm_ref, o_ref, u_ref, uprev_ref, unext_ref, mod_ref,
                   ghead_ref, pscale_ref, gffn_ref, gfinal_ref,
                   band_ref, wpool_ref, wout_ref, wff1_ref, wff2_ref, out_ref,
                   ubuf_ref, ycat_ref, *, seq, final_norm):
    tm = x_ref.shape[1]
    dm = N_HEADS * HEAD_DIM
    step = pl.program_id(0)
    n_tiles = pl.num_programs(0) - 1
    tiles_per_seq = seq // tm
    tile = jnp.minimum(step, n_tiles - 1) % tiles_per_seq

    @pl.when(step == 0)
    def _():
        ycat_ref[1] = jnp.zeros(ycat_ref.shape[1:], BF16)

    def stages(fill, drain):
        gate1 = mod_ref[0, 2:3, :]
        shift2 = mod_ref[0, 3:4, :]
        scale2 = mod_ref[0, 4:5, :]
        gate2 = mod_ref[0, 5:6, :]
        for r0 in range(0, tm, OUTFFN_SLAB):
            rows = slice(r0, r0 + OUTFFN_SLAB)
            yb = []
            for gi in range(len(POOL_WINDOWS)):
                cols = slice(gi * HEAD_DIM, (gi + 1) * HEAD_DIM)
                mapped = jnp.dot(ycat_ref[drain, rows, dm + gi * HEAD_DIM:dm + (gi + 1) * HEAD_DIM], wpool_ref[gi],
                                 preferred_element_type=F32)
                yb.append((mapped * pscale_ref[:, cols]).astype(BF16))
            yproj = (jnp.dot(ycat_ref[drain, rows, 0:dm], wout_ref[0:dm, :], preferred_element_type=F32)
                     + jnp.dot(jnp.concatenate(yb, axis=1), wout_ref[dm:, :], preferred_element_type=F32))
            x1 = x_ref[0, rows, :] + gate1 * yproj
            h2 = ((_rms_scale(x1) * gffn_ref[...]) * (1.0 + scale2) + shift2).astype(BF16)
            acc = jnp.zeros(x1.shape, F32)
            for j in range(wff1_ref.shape[1] // FF_COLS):
                hid = jnp.dot(h2, wff1_ref[:, j * FF_COLS:(j + 1) * FF_COLS], preferred_element_type=F32)
                hid = jnp.square(jnp.maximum(hid, 0.0)).astype(BF16)
                acc = acc + jnp.dot(hid, wff2_ref[j * FF_COLS:(j + 1) * FF_COLS, :], preferred_element_type=F32)
            x2 = x1 + gate2 * acc
            if final_norm:
                x2 = _rms_scale(x2) * gfinal_ref[...]
            out_ref[0, rows, :] = x2

        for hd in range(N_HEADS):
            cols = slice(hd * HEAD_DIM, (hd + 1) * HEAD_DIM)
            hn = _rms_scale(hm_ref[0, hd].astype(F32)) * ghead_ref[:, cols]
            gate = jax.nn.sigmoid(o_ref[0, :, cols].astype(F32))
            ycat_ref[fill, :, cols] = (gate * hn).astype(BF16)

        ubuf_ref[0:POOL_HALO, :] = jnp.where(tile > 0, uprev_ref[0], jnp.zeros_like(uprev_ref[0]))
        ubuf_ref[POOL_HALO:POOL_HALO + tm, :] = u_ref[0]
        ubuf_ref[POOL_HALO + tm:, :] = jnp.where(tile < tiles_per_seq - 1, unext_ref[0],
                                                 jnp.zeros_like(unext_ref[0]))
        for gi in range(len(POOL_WINDOWS)):
            cols = slice(gi * HEAD_DIM, (gi + 1) * HEAD_DIM)
            for r0 in range(0, tm, POOL_BLOCK):
                ext = ubuf_ref[r0:r0 + POOL_BLOCK + 2 * POOL_HALO, cols]
                mixed = jnp.dot(band_ref[gi], ext, preferred_element_type=F32)
                ycat_ref[fill, r0:r0 + POOL_BLOCK, dm + gi * HEAD_DIM:dm + (gi + 1) * HEAD_DIM] = mixed.astype(BF16)

        def clipped_rows(r0):
            ext = ubuf_ref[r0:r0 + 3 * POOL_HALO, :].astype(F32)
            pos = tile * tm + r0 + lax.broadcasted_iota(jnp.int32, (POOL_HALO, HEAD_DIM), 0)
            for gi, win in enumerate(POOL_WINDOWS):
                half = win // 2
                grp = ext[:, gi * HEAD_DIM:(gi + 1) * HEAD_DIM]
                total = grp[POOL_HALO - half:2 * POOL_HALO - half]
                for j in range(1 - half, half):
                    total = total + grp[POOL_HALO + j:2 * POOL_HALO + j]
                cnt = jnp.minimum(pos + half, seq) - jnp.maximum(pos - half, 0)
                mixed = total / cnt.astype(F32) - grp[POOL_HALO:2 * POOL_HALO]
                ycat_ref[fill, r0:r0 + POOL_HALO, dm + gi * HEAD_DIM:dm + (gi + 1) * HEAD_DIM] = mixed.astype(BF16)

        @pl.when(tile == 0)
        def _():
            clipped_rows(0)

        @pl.when(tile == tiles_per_seq - 1)
        def _():
            clipped_rows(tm - POOL_HALO)

    @pl.when(step % 2 == 0)
    def _():
        stages(0, 1)

    @pl.when(step % 2 == 1)
    def _():
        stages(1, 0)


def _pool_bands():
    row = np.arange(POOL_BLOCK)[:, None]
    col = np.arange(POOL_BLOCK + 2 * POOL_HALO)[None, :] - POOL_HALO
    bands = [((col >= row - win // 2) & (col < row + win // 2)) / win - (col == row) for win in POOL_WINDOWS]
    return jnp.asarray(np.stack(bands), dtype=BF16)


def _outffn(x, hm, o, u, mod, g_head, pool_scale, g_ffn, g_final, w_pool, w_out, w_ff1, w_ff2,
            final_norm):
    batch, seq, d = x.shape
    tm = OUTFFN_ROWS
    dm = N_HEADS * HEAD_DIM
    d_pool = u.shape[2]
    halo_per_tile = tm // POOL_HALO
    n_halo_blocks = seq // POOL_HALO
    tiles_per_seq = seq // tm
    n_tiles = batch * tiles_per_seq

    def mm_tile(g):
        t = jnp.maximum(g - 1, 0)
        return t // tiles_per_seq, t % tiles_per_seq

    def ew_tile(g):
        t = jnp.minimum(g, n_tiles - 1)
        return t // tiles_per_seq, t % tiles_per_seq

    def mm_rows(g):
        b, i = mm_tile(g)
        return b, i, 0

    def ew_rows(g):
        b, i = ew_tile(g)
        return b, i, 0

    def ew_heads(g):
        b, i = ew_tile(g)
        return b, 0, i, 0

    def ew_prev(g):
        b, i = ew_tile(g)
        return b, jnp.maximum(i * halo_per_tile - 1, 0), 0

    def ew_next(g):
        b, i = ew_tile(g)
        return b, jnp.minimum((i + 1) * halo_per_tile, n_halo_blocks - 1), 0

    bands = _pool_bands()
    kernel = functools.partial(_outffn_kernel, seq=seq, final_norm=final_norm)
    return pl.pallas_call(
        kernel,
        grid=(n_tiles + 1,),
        in_specs=[pl.BlockSpec((1, tm, d), mm_rows),
                  pl.BlockSpec((1, N_HEADS, tm, HEAD_DIM), ew_heads),
                  pl.BlockSpec((1, tm, dm), ew_rows),
                  pl.BlockSpec((1, tm, d_pool), ew_rows),
                  pl.BlockSpec((1, POOL_HALO, d_pool), ew_prev),
                  pl.BlockSpec((1, POOL_HALO, d_pool), ew_next),
                  pl.BlockSpec((1,) + mod.shape[1:], lambda g: (mm_tile(g)[0], 0, 0)),
                  _const_spec((1, dm)),
                  _const_spec((1, d_pool)),
                  _const_spec((1, d)),
                  _const_spec((1, d)),
                  _const_spec(bands.shape),
                  _const_spec(w_pool.shape),
                  _const_spec(w_out.shape),
                  _const_spec(w_ff1.shape),
                  _const_spec(w_ff2.shape)],
        out_specs=pl.BlockSpec((1, tm, d), mm_rows),
        out_shape=jax.ShapeDtypeStruct((batch, seq, d), F32),
        scratch_shapes=[pltpu.VMEM((tm + 2 * POOL_HALO, d_pool), BF16),
                        pltpu.VMEM((2, tm, dm + d_pool), BF16)],
        compiler_params=pltpu.CompilerParams(dimension_semantics=("arbitrary",),
                                             vmem_limit_bytes=VMEM_LIMIT_BYTES),
    )(x, hm, o, u, u, u, mod, g_head.reshape(1, dm), pool_scale.reshape(1, d_pool),
      g_ffn.reshape(1, d), g_final.reshape(1, d), bands, w_pool, w_out, w_ff1, w_ff2)


def _pack_w_in(w_in):
    dm = N_HEADS * HEAD_DIM
    n_gates = N_DIRS * 2 * N_HEADS
    main = w_in[:, :4 * dm]
    gates = w_in[:, 4 * dm:4 * dm + n_gates]
    pool = w_in[:, 4 * dm + n_gates:]
    gates = gates.reshape(-1, N_DIRS, 2, N_HEADS).transpose(0, 3, 1, 2).reshape(-1, n_gates)
    gates = jnp.pad(gates, ((0, 0), (0, LANES - n_gates)))
    return jnp.concatenate([main, pool, gates], axis=1).astype(BF16)


def _pack_gate_bias(b_igate, b_fgate):
    rows = jnp.stack([b_igate, b_fgate], axis=1)
    rows = rows.transpose(2, 0, 1).reshape(N_HEADS, N_GATE_ROWS)
    return jnp.broadcast_to(rows[:, :, None], (N_HEADS, N_GATE_ROWS, CHUNK)).astype(F32)


def kernel(x, c, w_ada, b_ada, g_mix, w_in, b_igate, b_fgate, g_head, w_pool, pool_scale, w_out,
           g_ffn, w_ff1, w_ff2, g_final):
    depth = w_ada.shape[0]
    batch, seq, d = x.shape
    d_pool = pool_scale.shape[1]
    assert g_head.shape[1] == N_HEADS * HEAD_DIM and w_pool.shape[1] == len(POOL_WINDOWS)
    assert seq % INPROJ_ROWS == 0 and seq % OUTFFN_ROWS == 0 and w_ff1.shape[2] % FF_COLS == 0
    for l in range(depth):
        mod = _adaln(c, w_ada[l], b_ada[l]).reshape(batch, 6, d)
        qt, k, vt, o, u, gates = _inproj(x, mod, g_mix[l], _pack_w_in(w_in[l]), d_pool)
        hm = _mlstm(qt, k, vt, gates, _pack_gate_bias(b_igate[l], b_fgate[l]))
        x = _outffn(x, hm, o, u, mod, g_head[l], pool_scale[l], g_ffn[l], g_final,
                    w_pool[l].astype(BF16), w_out[l].astype(BF16), w_ff1[l].astype(BF16),
                    w_ff2[l].astype(BF16), final_norm=(l == depth - 1))
    return x
```

```python
import functools

import jax
import jax.numpy as jnp
import numpy as np
from jax import lax
from jax.experimental import pallas as pl
from jax.experimental.pallas import tpu as pltpu

F32 = jnp.float32
BF16 = jnp.bfloat16

LANES = 128
BF16_SUBLANES = 16
VMEM_LIMIT_BYTES = 56 * 1024 * 1024

N_HEADS = 4
HEAD_DIM = 128
CHUNK = 128
POOL_WINDOWS = (2, 4, 8, 16)
POOL_BLOCK = 128
POOL_HALO = 16
N_DIRS = 2
N_GATE_ROWS = 2 * N_DIRS
EPS = 1e-6
LOG2_E = 1.4426950408889634

INPROJ_ROWS = 1024
OUTFFN_ROWS = 512
FF_COLS = 1024
AUG_ROWS = HEAD_DIM + BF16_SUBLANES

ROW_A, ROW_M, ROW_AINTER, ROW_EXPNEG, ROW_WK, ROW_DECAY = range(6)
N_COEF = 6
CHUNKS_PER_STEP = 16


def _const_spec(shape):
    zeros = (0,) * len(shape)
    return pl.BlockSpec(shape, lambda *_: zeros, pipeline_mode=pl.Buffered(1))


def _rms_scale(x):
    return x * lax.rsqrt(jnp.mean(x * x, axis=-1, keepdims=True) + EPS)


def _modulated_norm_bf16(x, gain, shift, scale):
    slope = (gain * (1.0 + scale)).astype(BF16)
    return _rms_scale(x).astype(BF16) * slope + shift.astype(BF16)


def _adaln_kernel(c_ref, w_ref, b_ref, o_ref):
    c = c_ref[...]
    act = c * jax.nn.sigmoid(c)
    o_ref[...] = jnp.dot(act.astype(BF16), w_ref[...].astype(BF16),
                         preferred_element_type=F32) + b_ref[...]


def _adaln(c, w_ada, b_ada):
    batch, d = c.shape
    n = w_ada.shape[1]
    return pl.pallas_call(
        _adaln_kernel,
        grid=(n // d,),
        in_specs=[pl.BlockSpec((batch, d), lambda j: (0, 0)),
                  pl.BlockSpec((d, d), lambda j: (0, j)),
                  pl.BlockSpec((1, d), lambda j: (0, j))],
        out_specs=pl.BlockSpec((batch, d), lambda j: (0, j)),
        out_shape=jax.ShapeDtypeStruct((batch, n), F32),
        compiler_params=pltpu.CompilerParams(dimension_semantics=("arbitrary",),
                                             vmem_limit_bytes=VMEM_LIMIT_BYTES),
    )(c, w_ada, b_ada.reshape(1, n))


def _inproj_kernel(x_ref, mod_ref, g_ref, w_ref, wg_ref, qt_ref, k_ref, vt_ref, o_ref, u_ref, gates_ref):
    dm = N_HEADS * HEAD_DIM
    x = x_ref[0]
    shift = mod_ref[0, 0:1, :]
    scale = mod_ref[0, 1:2, :]
    hb = _modulated_norm_bf16(x, g_ref[...], shift, scale)

    def proj(lo, width):
        return jnp.dot(hb, w_ref[:, lo:lo + width], preferred_element_type=F32)

    q = proj(0, dm) * (HEAD_DIM ** -0.5)
    for hd in range(N_HEADS):
        qt_ref[0, hd] = q[:, hd * HEAD_DIM:(hd + 1) * HEAD_DIM].T.astype(BF16)
    k = proj(dm, dm)
    for hd in range(N_HEADS):
        k_ref[0, hd] = k[:, hd * HEAD_DIM:(hd + 1) * HEAD_DIM].astype(BF16)
    v = proj(2 * dm, dm)
    for hd in range(N_HEADS):
        vt_ref[0, hd, 0:HEAD_DIM, :] = v[:, hd * HEAD_DIM:(hd + 1) * HEAD_DIM].T.astype(BF16)
        vt_ref[0, hd, HEAD_DIM:AUG_ROWS, :] = jnp.ones((AUG_ROWS - HEAD_DIM, x.shape[0]), BF16)
    o_ref[0] = proj(3 * dm, dm).astype(BF16)
    u_ref[0] = proj(4 * dm, u_ref.shape[2]).astype(BF16)
    gt = lax.dot_general(wg_ref[...], hb, (((1,), (1,)), ((), ())), preferred_element_type=F32)
    n_chunks = x.shape[0] // CHUNK
    for hd in range(N_HEADS):
        for r in range(N_GATE_ROWS):
            row = hd * N_GATE_ROWS + r
            for j in range(n_chunks):
                gates_ref[0, hd, r, j:j + 1, :] = gt[row:row + 1, j * CHUNK:(j + 1) * CHUNK]


def _inproj(x, mod, g_mix, w_cat, w_gates_t, d_pool):
    batch, seq, d = x.shape
    tm = INPROJ_ROWS
    dm = N_HEADS * HEAD_DIM
    grid = (batch, seq // tm)
    head_t = jax.ShapeDtypeStruct((batch, N_HEADS, HEAD_DIM, seq), BF16)
    head_n = jax.ShapeDtypeStruct((batch, N_HEADS, seq, HEAD_DIM), BF16)
    head_aug = jax.ShapeDtypeStruct((batch, N_HEADS, AUG_ROWS, seq), BF16)
    out_shape = (head_t, head_n, head_aug,
                 jax.ShapeDtypeStruct((batch, seq, dm), BF16),
                 jax.ShapeDtypeStruct((batch, seq, d_pool), BF16),
                 jax.ShapeDtypeStruct((batch, N_HEADS, N_GATE_ROWS, seq // CHUNK, CHUNK), F32))
    spec_t = pl.BlockSpec((1, N_HEADS, HEAD_DIM, tm), lambda b, i: (b, 0, 0, i))
    spec_n = pl.BlockSpec((1, N_HEADS, tm, HEAD_DIM), lambda b, i: (b, 0, i, 0))
    spec_aug = pl.BlockSpec((1, N_HEADS, AUG_ROWS, tm), lambda b, i: (b, 0, 0, i))
    out_specs = (spec_t, spec_n, spec_aug,
                 pl.BlockSpec((1, tm, dm), lambda b, i: (b, i, 0)),
                 pl.BlockSpec((1, tm, d_pool), lambda b, i: (b, i, 0)),
                 pl.BlockSpec((1, N_HEADS, N_GATE_ROWS, tm // CHUNK, CHUNK), lambda b, i: (b, 0, 0, i, 0)))
    return pl.pallas_call(
        _inproj_kernel,
        grid=grid,
        in_specs=[pl.BlockSpec((1, tm, d), lambda b, i: (b, i, 0)),
                  pl.BlockSpec((1,) + mod.shape[1:], lambda b, i: (b, 0, 0)),
                  _const_spec((1, d)),
                  _const_spec(w_cat.shape),
                  _const_spec(w_gates_t.shape)],
        out_specs=out_specs,
        out_shape=out_shape,
        compiler_params=pltpu.CompilerParams(dimension_semantics=("arbitrary", "arbitrary"),
                                             vmem_limit_bytes=VMEM_LIMIT_BYTES),
    )(x, mod, g_mix.reshape(1, d), w_cat, w_gates_t)


def _time_scan(x, op, fill, reverse):
    row = lax.broadcasted_iota(jnp.int32, x.shape, 0)
    step = 1
    while step < CHUNK:
        if reverse:
            shifted = jnp.where(row < CHUNK - step, pltpu.roll(x, CHUNK - step, axis=0), fill)
        else:
            shifted = jnp.where(row >= step, pltpu.roll(x, step, axis=0), fill)
        x = op(x, shifted)
        step *= 2
    return x


def _mlstm_kernel(qt_ref, k_ref, vt_ref, gates_ref, gbias_ref, out_ref,
                  coef_ref, mprev_ref, sprev_ref, st_ref):
    n_chunks = gates_ref.shape[3]
    assert 2 * N_DIRS * n_chunks == LANES

    log_i, log_f = [], []
    for d in range(N_DIRS):
        log_i.append(gates_ref[0, 0, 2 * d] + gbias_ref[0, 2 * d:2 * d + 1, :])
        z = gates_ref[0, 0, 2 * d + 1] + gbias_ref[0, 2 * d + 1:2 * d + 2, :]
        log_f.append(-(jnp.maximum(-z, 0.0) + jnp.log1p(jnp.exp(-jnp.abs(z)))))
    by_time = jnp.concatenate(log_f + log_i, axis=0).T
    lane = lax.broadcasted_iota(jnp.int32, by_time.shape, 1)
    fwd_lane = (lane % (N_DIRS * n_chunks)) < n_chunks
    b_t = jnp.where(fwd_lane, _time_scan(by_time, jnp.add, 0.0, False),
                    _time_scan(by_time, jnp.add, 0.0, True))
    a_t = pltpu.roll(by_time, N_DIRS * n_chunks, axis=1) - b_t
    cm_t = jnp.where(fwd_lane, _time_scan(a_t, jnp.maximum, -jnp.inf, False),
                     _time_scan(a_t, jnp.maximum, -jnp.inf, True))
    b_rows, a_rows, cm_rows = b_t.T, a_t.T, cm_t.T

    for d in range(N_DIRS):
        reverse = d == 1
        last = 0 if reverse else CHUNK - 1
        b = b_rows[d * n_chunks:(d + 1) * n_chunks]
        a = a_rows[d * n_chunks:(d + 1) * n_chunks]
        cm = cm_rows[d * n_chunks:(d + 1) * n_chunks]
        b_last = jnp.broadcast_to(b[:, last:last + 1], b.shape)
        cm_last = jnp.broadcast_to(cm[:, last:last + 1], b.shape)
        m = jnp.zeros((1, CHUNK), F32)
        order = range(n_chunks - 1, -1, -1) if reverse else range(n_chunks)
        for c in order:
            mprev_ref[d, c:c + 1, :] = m
            m = b_last[c:c + 1, :] + jnp.maximum(m, cm_last[c:c + 1, :])
        m_prev = mprev_ref[d]
        m_row = jnp.maximum(m_prev, cm)
        a_inter = jnp.exp(m_prev - m_row)
        exp_neg = jnp.exp(-(b + m_row))
        m_end = jnp.broadcast_to(m_row[:, last:last + 1], b.shape)
        wk = jnp.exp(a - m_end)
        decay = jnp.broadcast_to(a_inter[:, last:last + 1], b.shape)
        for r, val in ((ROW_A, a * LOG2_E), (ROW_M, m_row * LOG2_E), (ROW_AINTER, a_inter), (ROW_EXPNEG, exp_neg),
                       (ROW_WK, wk), (ROW_DECAY, decay)):
            coef_ref[d, r] = val

    def coef_row(d, r, c):
        return coef_ref[d, r, pl.ds(c, 1), :]

    def chunk_operands(c):
        off = pl.multiple_of(c * CHUNK, CHUNK)
        return off, qt_ref[0, 0, :, pl.ds(off, CHUNK)], k_ref[0, 0, pl.ds(off, CHUNK), :], \
            vt_ref[0, 0, :, pl.ds(off, CHUNK)]

    key_pos = lax.broadcasted_iota(jnp.int32, (CHUNK, CHUNK), 0)
    qry_pos = lax.broadcasted_iota(jnp.int32, (CHUNK, CHUNK), 1)
    visible = (key_pos <= qry_pos, key_pos >= qry_pos)

    def weighted_scores(c):
        _, qt_c, k_c, _ = chunk_operands(c)
        scores = jnp.dot(k_c, qt_c, preferred_element_type=F32)
        for d in range(N_DIRS):
            a_col = jnp.broadcast_to(coef_row(d, ROW_A, c), (CHUNK, CHUNK)).T
            weight = jnp.exp2(jnp.where(visible[d], a_col - coef_row(d, ROW_M, c), -jnp.inf))
            st_ref[d, c] = (scores * weight).astype(BF16)

    def state_step(d, c, state):
        _, _, k_c, vt_c = chunk_operands(c)
        sprev_ref[d, c] = state.astype(BF16)
        vw = vt_c * coef_row(d, ROW_WK, c).astype(BF16)
        return coef_row(d, ROW_DECAY, c) * state + jnp.dot(vw, k_c, preferred_element_type=F32)

    def state_body(i, states):
        s_fwd, s_bwd = states
        for j in range(CHUNKS_PER_STEP):
            c = i * CHUNKS_PER_STEP + j
            weighted_scores(c)
            s_fwd = state_step(0, c, s_fwd)
            s_bwd = state_step(1, n_chunks - 1 - c, s_bwd)
        return s_fwd, s_bwd

    zero_state = jnp.zeros((AUG_ROWS, HEAD_DIM), F32)
    lax.fori_loop(0, n_chunks // CHUNKS_PER_STEP, state_body, (zero_state, zero_state))

    def output_chunk(c):
        off, qt_c, _, vt_c = chunk_operands(c)
        hsum = None
        for d in range(N_DIRS):
            qa = (qt_c.astype(F32) * coef_row(d, ROW_AINTER, c)).astype(BF16)
            lhs = jnp.concatenate([vt_c, sprev_ref[d, c]], axis=1)
            rhs = jnp.concatenate([st_ref[d, c], qa], axis=0)
            numden = jnp.dot(lhs, rhs, preferred_element_type=F32)
            den = numden[HEAD_DIM:HEAD_DIM + 1, :]
            ht = numden[0:HEAD_DIM, :] * (1.0 / jnp.maximum(jnp.abs(den), coef_row(d, ROW_EXPNEG, c)))
            hsum = ht if hsum is None else hsum + ht
        out_ref[0, 0, pl.ds(off, CHUNK), :] = hsum.T.astype(BF16)

    def output_body(i, carry):
        for j in range(CHUNKS_PER_STEP):
            output_chunk(i * CHUNKS_PER_STEP + j)
        return carry

    lax.fori_loop(0, n_chunks // CHUNKS_PER_STEP, output_body, 0)


def _mlstm(qt, k, vt, gates, gbias):
    batch, heads, hd, seq = qt.shape
    n_chunks = seq // CHUNK
    assert n_chunks % CHUNKS_PER_STEP == 0
    spec_t = pl.BlockSpec((1, 1, hd, seq), lambda b, h: (b, h, 0, 0))
    spec_n = pl.BlockSpec((1, 1, seq, hd), lambda b, h: (b, h, 0, 0))
    spec_aug = pl.BlockSpec((1, 1, AUG_ROWS, seq), lambda b, h: (b, h, 0, 0))
    return pl.pallas_call(
        _mlstm_kernel,
        grid=(batch, heads),
        in_specs=[spec_t, spec_n, spec_aug,
                  pl.BlockSpec((1, 1, N_GATE_ROWS, n_chunks, CHUNK), lambda b, h: (b, h, 0, 0, 0)),
                  pl.BlockSpec((1, N_GATE_ROWS, CHUNK), lambda b, h: (h, 0, 0))],
        out_specs=spec_n,
        out_shape=jax.ShapeDtypeStruct((batch, heads, seq, hd), BF16),
        scratch_shapes=[pltpu.VMEM((N_DIRS, N_COEF, n_chunks, CHUNK), F32),
                        pltpu.VMEM((N_DIRS, n_chunks, CHUNK), F32),
                        pltpu.VMEM((N_DIRS, n_chunks, AUG_ROWS, hd), BF16),
                        pltpu.VMEM((N_DIRS, n_chunks, CHUNK, CHUNK), BF16)],
        compiler_params=pltpu.CompilerParams(dimension_semantics=("arbitrary", "arbitrary"),
                                             vmem_limit_bytes=VMEM_LIMIT_BYTES),
    )(qt, k, vt, gates, gbias)


def _outffn_kernel(x_ref, hm_ref, o_ref, u_ref, uprev_ref, unext_ref, mod_ref,
                   ghead_ref, pscale_ref, gffn_ref, gfinal_ref,
                   band_ref, wpool_ref, wout_ref, wff1_ref, wff2_ref, out_ref,
                   ubuf_ref, ycat_ref, *, seq, final_norm):
    tm = x_ref.shape[1]
    dm = N_HEADS * HEAD_DIM
    step = pl.program_id(0)
    n_tiles = pl.num_programs(0) - 1
    tiles_per_seq = seq // tm
    tile = jnp.minimum(step, n_tiles - 1) % tiles_per_seq

    @pl.when(step == 0)
    def _():
        ycat_ref[1] = jnp.zeros(ycat_ref.shape[1:], BF16)

    def stages(fill, drain):
        yb = []
        for gi in range(len(POOL_WINDOWS)):
            cols = slice(gi * HEAD_DIM, (gi + 1) * HEAD_DIM)
            mapped = jnp.dot(ycat_ref[drain, :, dm + gi * HEAD_DIM:dm + (gi + 1) * HEAD_DIM], wpool_ref[gi],
                             preferred_element_type=F32)
            yb.append((mapped * pscale_ref[:, cols]).astype(BF16))
        gate1 = mod_ref[0, 2:3, :]
        yproj = (jnp.dot(ycat_ref[drain, :, 0:dm], wout_ref[0:dm, :], preferred_element_type=F32)
                 + jnp.dot(jnp.concatenate(yb, axis=1), wout_ref[dm:, :], preferred_element_type=F32))
        x1 = x_ref[0] + gate1 * yproj

        shift2 = mod_ref[0, 3:4, :]
        scale2 = mod_ref[0, 4:5, :]
        gate2 = mod_ref[0, 5:6, :]
        h2 = ((_rms_scale(x1) * gffn_ref[...]) * (1.0 + scale2) + shift2).astype(BF16)
        acc = jnp.zeros(x1.shape, F32)
        for j in range(wff1_ref.shape[1] // FF_COLS):
            hid = jnp.dot(h2, wff1_ref[:, j * FF_COLS:(j + 1) * FF_COLS], preferred_element_type=F32)
            hid = jnp.square(jnp.maximum(hid, 0.0)).astype(BF16)
            acc = acc + jnp.dot(hid, wff2_ref[j * FF_COLS:(j + 1) * FF_COLS, :], preferred_element_type=F32)
        x2 = x1 + gate2 * acc
        if final_norm:
            x2 = _rms_scale(x2) * gfinal_ref[...]
        out_ref[0] = x2

        for hd in range(N_HEADS):
            cols = slice(hd * HEAD_DIM, (hd + 1) * HEAD_DIM)
            hn = _rms_scale(hm_ref[0, hd].astype(F32)) * ghead_ref[:, cols]
            gate = jax.nn.sigmoid(o_ref[0, :, cols].astype(F32))
            ycat_ref[fill, :, cols] = (gate * hn).astype(BF16)

        ubuf_ref[0:POOL_HALO, :] = jnp.where(tile > 0, uprev_ref[0], jnp.zeros_like(uprev_ref[0]))
        ubuf_ref[POOL_HALO:POOL_HALO + tm, :] = u_ref[0]
        ubuf_ref[POOL_HALO + tm:, :] = jnp.where(tile < tiles_per_seq - 1, unext_ref[0],
                                                 jnp.zeros_like(unext_ref[0]))
        for gi in range(len(POOL_WINDOWS)):
            cols = slice(gi * HEAD_DIM, (gi + 1) * HEAD_DIM)
            for r0 in range(0, tm, POOL_BLOCK):
                ext = ubuf_ref[r0:r0 + POOL_BLOCK + 2 * POOL_HALO, cols]
                mixed = jnp.dot(band_ref[gi], ext, preferred_element_type=F32)
                ycat_ref[fill, r0:r0 + POOL_BLOCK, dm + gi * HEAD_DIM:dm + (gi + 1) * HEAD_DIM] = mixed.astype(BF16)

        def clipped_rows(r0):
            ext = ubuf_ref[r0:r0 + 3 * POOL_HALO, :].astype(F32)
            pos = tile * tm + r0 + lax.broadcasted_iota(jnp.int32, (POOL_HALO, HEAD_DIM), 0)
            for gi, win in enumerate(POOL_WINDOWS):
                half = win // 2
                grp = ext[:, gi * HEAD_DIM:(gi + 1) * HEAD_DIM]
                total = grp[POOL_HALO - half:2 * POOL_HALO - half]
                for j in range(1 - half, half):
                    total = total + grp[POOL_HALO + j:2 * POOL_HALO + j]
                cnt = jnp.minimum(pos + half, seq) - jnp.maximum(pos - half, 0)
                mixed = total / cnt.astype(F32) - grp[POOL_HALO:2 * POOL_HALO]
                ycat_ref[fill, r0:r0 + POOL_HALO, dm + gi * HEAD_DIM:dm + (gi + 1) * HEAD_DIM] = mixed.astype(BF16)

        @pl.when(tile == 0)
        def _():
            clipped_rows(0)

        @pl.when(tile == tiles_per_seq - 1)
        def _():
            clipped_rows(tm - POOL_HALO)

    @pl.when(step % 2 == 0)
    def _():
        stages(0, 1)

    @pl.when(step % 2 == 1)
    def _():
        stages(1, 0)


def _pool_bands():
    row = np.arange(POOL_BLOCK)[:, None]
    col = np.arange(POOL_BLOCK + 2 * POOL_HALO)[None, :] - POOL_HALO
    bands = [((col >= row - win // 2) & (col < row + win // 2)) / win - (col == row) for win in POOL_WINDOWS]
    return jnp.asarray(np.stack(bands), dtype=BF16)


def _outffn(x, hm, o, u, mod, g_head, pool_scale, g_ffn, g_final, w_pool, w_out, w_ff1, w_ff2,
            final_norm):
    batch, seq, d = x.shape
    tm = OUTFFN_ROWS
    dm = N_HEADS * HEAD_DIM
    d_pool = u.shape[2]
    halo_per_tile = tm // POOL_HALO
    n_halo_blocks = seq // POOL_HALO
    tiles_per_seq = seq // tm
    n_tiles = batch * tiles_per_seq

    def mm_tile(g):
        t = jnp.maximum(g - 1, 0)
        return t // tiles_per_seq, t % tiles_per_seq

    def ew_tile(g):
        t = jnp.minimum(g, n_tiles - 1)
        return t // tiles_per_seq, t % tiles_per_seq

    def mm_rows(g):
        b, i = mm_tile(g)
        return b, i, 0

    def ew_rows(g):
        b, i = ew_tile(g)
        return b, i, 0

    def ew_heads(g):
        b, i = ew_tile(g)
        return b, 0, i, 0

    def ew_prev(g):
        b, i = ew_tile(g)
        return b, jnp.maximum(i * halo_per_tile - 1, 0), 0

    def ew_next(g):
        b, i = ew_tile(g)
        return b, jnp.minimum((i + 1) * halo_per_tile, n_halo_blocks - 1), 0

    bands = _pool_bands()
    kernel = functools.partial(_outffn_kernel, seq=seq, final_norm=final_norm)
    return pl.pallas_call(
        kernel,
        grid=(n_tiles + 1,),
        in_specs=[pl.BlockSpec((1, tm, d), mm_rows),
                  pl.BlockSpec((1, N_HEADS, tm, HEAD_DIM), ew_heads),
                  pl.BlockSpec((1, tm, dm), ew_rows),
                  pl.BlockSpec((1, tm, d_pool), ew_rows),
                  pl.BlockSpec((1, POOL_HALO, d_pool), ew_prev),
                  pl.BlockSpec((1, POOL_HALO, d_pool), ew_next),
                  pl.BlockSpec((1,) + mod.shape[1:], lambda g: (mm_tile(g)[0], 0, 0)),
                  _const_spec((1, dm)),
                  _const_spec((1, d_pool)),
                  _const_spec((1, d)),
                  _const_spec((1, d)),
                  _const_spec(bands.shape),
                  _const_spec(w_pool.shape),
                  _const_spec(w_out.shape),
                  _const_spec(w_ff1.shape),
                  _const_spec(w_ff2.shape)],
        out_specs=pl.BlockSpec((1, tm, d), mm_rows),
        out_shape=jax.ShapeDtypeStruct((batch, seq, d), F32),
        scratch_shapes=[pltpu.VMEM((tm + 2 * POOL_HALO, d_pool), BF16),
                        pltpu.VMEM((2, tm, dm + d_pool), BF16)],
        compiler_params=pltpu.CompilerParams(dimension_semantics=("arbitrary",),
                                             vmem_limit_bytes=VMEM_LIMIT_BYTES),
    )(x, hm, o, u, u, u, mod, g_head.reshape(1, dm), pool_scale.reshape(1, d_pool),
      g_ffn.reshape(1, d), g_final.reshape(1, d), bands, w_pool, w_out, w_ff1, w_ff2)


def _pack_w_in(w_in):
    dm = N_HEADS * HEAD_DIM
    n_gates = N_DIRS * 2 * N_HEADS
    main = w_in[:, :4 * dm]
    gates = w_in[:, 4 * dm:4 * dm + n_gates]
    pool = w_in[:, 4 * dm + n_gates:]
    gates = gates.reshape(-1, N_DIRS, 2, N_HEADS).transpose(3, 1, 2, 0).reshape(n_gates, -1)
    return jnp.concatenate([main, pool], axis=1).astype(BF16), gates.astype(BF16)


def _pack_gate_bias(b_igate, b_fgate):
    rows = jnp.stack([b_igate, b_fgate], axis=1)
    rows = rows.transpose(2, 0, 1).reshape(N_HEADS, N_GATE_ROWS)
    return jnp.broadcast_to(rows[:, :, None], (N_HEADS, N_GATE_ROWS, CHUNK)).astype(F32)


def kernel(x, c, w_ada, b_ada, g_mix, w_in, b_igate, b_fgate, g_head, w_pool, pool_scale, w_out,
           g_ffn, w_ff1, w_ff2, g_final):
    depth = w_ada.shape[0]
    batch, seq, d = x.shape
    d_pool = pool_scale.shape[1]
    assert g_head.shape[1] == N_HEADS * HEAD_DIM and w_pool.shape[1] == len(POOL_WINDOWS)
    assert seq % INPROJ_ROWS == 0 and seq % OUTFFN_ROWS == 0 and w_ff1.shape[2] % FF_COLS == 0
    for l in range(depth):
        mod = _adaln(c, w_ada[l], b_ada[l]).reshape(batch, 6, d)
        qt, k, vt, o, u, gates = _inproj(x, mod, g_mix[l], *_pack_w_in(w_in[l]), d_pool)
        hm = _mlstm(qt, k, vt, gates, _pack_gate_bias(b_igate[l], b_fgate[l]))
        x = _outffn(x, hm, o, u, mod, g_head[l], pool_scale[l], g_ffn[l], g_final,
                    w_pool[l].astype(BF16), w_out[l].astype(BF16), w_ff1[l].astype(BF16),
                    w_ff2[l].astype(BF16), final_norm=(l == depth - 1))
    return x
```

```python
import functools

import jax
import jax.numpy as jnp
import numpy as np
from jax import lax
from jax.experimental import pallas as pl
from jax.experimental.pallas import tpu as pltpu

F32 = jnp.float32
BF16 = jnp.bfloat16

LANES = 128
BF16_SUBLANES = 16
VMEM_LIMIT_BYTES = 56 * 1024 * 1024

N_HEADS = 4
HEAD_DIM = 128
CHUNK = 128
POOL_WINDOWS = (2, 4, 8, 16)
POOL_BLOCK = 128
POOL_HALO = 16
N_DIRS = 2
N_GATE_ROWS = 2 * N_DIRS
EPS = 1e-6
LOG2_E = 1.4426950408889634

INPROJ_ROWS = 1024
OUTFFN_ROWS = 512
FF_COLS = 1024
AUG_ROWS = HEAD_DIM + BF16_SUBLANES

ROW_A, ROW_M, ROW_AINTER, ROW_EXPNEG, ROW_WK, ROW_DECAY = range(6)
N_COEF = 6
CHUNKS_PER_STEP = 32


def _const_spec(shape):
    zeros = (0,) * len(shape)
    return pl.BlockSpec(shape, lambda *_: zeros, pipeline_mode=pl.Buffered(1))


def _rms_scale(x):
    return x * lax.rsqrt(jnp.mean(x * x, axis=-1, keepdims=True) + EPS)


def _modulated_norm_bf16(x, gain, shift, scale):
    slope = (gain * (1.0 + scale)).astype(BF16)
    return _rms_scale(x).astype(BF16) * slope + shift.astype(BF16)


def _adaln_kernel(c_ref, w_ref, b_ref, o_ref):
    c = c_ref[...]
    act = c * jax.nn.sigmoid(c)
    o_ref[...] = jnp.dot(act.astype(BF16), w_ref[...].astype(BF16),
                         preferred_element_type=F32) + b_ref[...]


def _adaln(c, w_ada, b_ada):
    batch, d = c.shape
    n = w_ada.shape[1]
    return pl.pallas_call(
        _adaln_kernel,
        grid=(n // d,),
        in_specs=[pl.BlockSpec((batch, d), lambda j: (0, 0)),
                  pl.BlockSpec((d, d), lambda j: (0, j)),
                  pl.BlockSpec((1, d), lambda j: (0, j))],
        out_specs=pl.BlockSpec((batch, d), lambda j: (0, j)),
        out_shape=jax.ShapeDtypeStruct((batch, n), F32),
        compiler_params=pltpu.CompilerParams(dimension_semantics=("arbitrary",),
                                             vmem_limit_bytes=VMEM_LIMIT_BYTES),
    )(c, w_ada, b_ada.reshape(1, n))


def _inproj_kernel(x_ref, mod_ref, g_ref, w_ref, wg_ref, qt_ref, k_ref, vt_ref, o_ref, u_ref, gates_ref):
    dm = N_HEADS * HEAD_DIM
    x = x_ref[0]
    shift = mod_ref[0, 0:1, :]
    scale = mod_ref[0, 1:2, :]
    hb = _modulated_norm_bf16(x, g_ref[...], shift, scale)

    def proj(lo, width):
        return jnp.dot(hb, w_ref[:, lo:lo + width], preferred_element_type=F32)

    q = proj(0, dm) * (HEAD_DIM ** -0.5)
    for hd in range(N_HEADS):
        qt_ref[0, hd] = q[:, hd * HEAD_DIM:(hd + 1) * HEAD_DIM].T.astype(BF16)
    k = proj(dm, dm)
    for hd in range(N_HEADS):
        k_ref[0, hd] = k[:, hd * HEAD_DIM:(hd + 1) * HEAD_DIM].astype(BF16)
    v = proj(2 * dm, dm)
    for hd in range(N_HEADS):
        vt_ref[0, hd, 0:HEAD_DIM, :] = v[:, hd * HEAD_DIM:(hd + 1) * HEAD_DIM].T.astype(BF16)
        vt_ref[0, hd, HEAD_DIM:AUG_ROWS, :] = jnp.ones((AUG_ROWS - HEAD_DIM, x.shape[0]), BF16)
    o_ref[0] = proj(3 * dm, dm).astype(BF16)
    u_ref[0] = proj(4 * dm, u_ref.shape[2]).astype(BF16)
    gt = lax.dot_general(wg_ref[...], hb, (((1,), (1,)), ((), ())), preferred_element_type=F32)
    n_chunks = x.shape[0] // CHUNK
    for hd in range(N_HEADS):
        for r in range(N_GATE_ROWS):
            row = hd * N_GATE_ROWS + r
            for j in range(n_chunks):
                gates_ref[0, hd, r, j:j + 1, :] = gt[row:row + 1, j * CHUNK:(j + 1) * CHUNK]


def _inproj(x, mod, g_mix, w_cat, w_gates_t, d_pool):
    batch, seq, d = x.shape
    tm = INPROJ_ROWS
    dm = N_HEADS * HEAD_DIM
    grid = (batch, seq // tm)
    head_t = jax.ShapeDtypeStruct((batch, N_HEADS, HEAD_DIM, seq), BF16)
    head_n = jax.ShapeDtypeStruct((batch, N_HEADS, seq, HEAD_DIM), BF16)
    head_aug = jax.ShapeDtypeStruct((batch, N_HEADS, AUG_ROWS, seq), BF16)
    out_shape = (head_t, head_n, head_aug,
                 jax.ShapeDtypeStruct((batch, seq, dm), BF16),
                 jax.ShapeDtypeStruct((batch, seq, d_pool), BF16),
                 jax.ShapeDtypeStruct((batch, N_HEADS, N_GATE_ROWS, seq // CHUNK, CHUNK), F32))
    spec_t = pl.BlockSpec((1, N_HEADS, HEAD_DIM, tm), lambda b, i: (b, 0, 0, i))
    spec_n = pl.BlockSpec((1, N_HEADS, tm, HEAD_DIM), lambda b, i: (b, 0, i, 0))
    spec_aug = pl.BlockSpec((1, N_HEADS, AUG_ROWS, tm), lambda b, i: (b, 0, 0, i))
    out_specs = (spec_t, spec_n, spec_aug,
                 pl.BlockSpec((1, tm, dm), lambda b, i: (b, i, 0)),
                 pl.BlockSpec((1, tm, d_pool), lambda b, i: (b, i, 0)),
                 pl.BlockSpec((1, N_HEADS, N_GATE_ROWS, tm // CHUNK, CHUNK), lambda b, i: (b, 0, 0, i, 0)))
    return pl.pallas_call(
        _inproj_kernel,
        grid=grid,
        in_specs=[pl.BlockSpec((1, tm, d), lambda b, i: (b, i, 0)),
                  pl.BlockSpec((1,) + mod.shape[1:], lambda b, i: (b, 0, 0)),
                  _const_spec((1, d)),
                  _const_spec(w_cat.shape),
                  _const_spec(w_gates_t.shape)],
        out_specs=out_specs,
        out_shape=out_shape,
        compiler_params=pltpu.CompilerParams(dimension_semantics=("arbitrary", "arbitrary"),
                                             vmem_limit_bytes=VMEM_LIMIT_BYTES),
    )(x, mod, g_mix.reshape(1, d), w_cat, w_gates_t)


def _time_scan(x, op, fill, reverse):
    row = lax.broadcasted_iota(jnp.int32, x.shape, 0)
    step = 1
    while step < CHUNK:
        if reverse:
            shifted = jnp.where(row < CHUNK - step, pltpu.roll(x, CHUNK - step, axis=0), fill)
        else:
            shifted = jnp.where(row >= step, pltpu.roll(x, step, axis=0), fill)
        x = op(x, shifted)
        step *= 2
    return x


def _mlstm_kernel(qt_ref, k_ref, vt_ref, gates_ref, gbias_ref, out_ref,
                  coef_ref, mprev_ref, sprev_ref, st_ref):
    n_chunks = gates_ref.shape[3]
    assert 2 * N_DIRS * n_chunks == LANES

    log_i, log_f = [], []
    for d in range(N_DIRS):
        log_i.append(gates_ref[0, 0, 2 * d] + gbias_ref[0, 2 * d:2 * d + 1, :])
        z = gates_ref[0, 0, 2 * d + 1] + gbias_ref[0, 2 * d + 1:2 * d + 2, :]
        log_f.append(-(jnp.maximum(-z, 0.0) + jnp.log1p(jnp.exp(-jnp.abs(z)))))
    by_time = jnp.concatenate(log_f + log_i, axis=0).T
    lane = lax.broadcasted_iota(jnp.int32, by_time.shape, 1)
    fwd_lane = (lane % (N_DIRS * n_chunks)) < n_chunks
    b_t = jnp.where(fwd_lane, _time_scan(by_time, jnp.add, 0.0, False),
                    _time_scan(by_time, jnp.add, 0.0, True))
    a_t = pltpu.roll(by_time, N_DIRS * n_chunks, axis=1) - b_t
    cm_t = jnp.where(fwd_lane, _time_scan(a_t, jnp.maximum, -jnp.inf, False),
                     _time_scan(a_t, jnp.maximum, -jnp.inf, True))
    b_rows, a_rows, cm_rows = b_t.T, a_t.T, cm_t.T

    for d in range(N_DIRS):
        reverse = d == 1
        last = 0 if reverse else CHUNK - 1
        b = b_rows[d * n_chunks:(d + 1) * n_chunks]
        a = a_rows[d * n_chunks:(d + 1) * n_chunks]
        cm = cm_rows[d * n_chunks:(d + 1) * n_chunks]
        b_last = jnp.broadcast_to(b[:, last:last + 1], b.shape)
        cm_last = jnp.broadcast_to(cm[:, last:last + 1], b.shape)
        m = jnp.zeros((1, CHUNK), F32)
        order = range(n_chunks - 1, -1, -1) if reverse else range(n_chunks)
        for c in order:
            mprev_ref[d, c:c + 1, :] = m
            m = b_last[c:c + 1, :] + jnp.maximum(m, cm_last[c:c + 1, :])
        m_prev = mprev_ref[d]
        m_row = jnp.maximum(m_prev, cm)
        a_inter = jnp.exp(m_prev - m_row)
        exp_neg = jnp.exp(-(b + m_row))
        m_end = jnp.broadcast_to(m_row[:, last:last + 1], b.shape)
        wk = jnp.exp(a - m_end)
        decay = jnp.broadcast_to(a_inter[:, last:last + 1], b.shape)
        for r, val in ((ROW_A, a * LOG2_E), (ROW_M, m_row * LOG2_E), (ROW_AINTER, a_inter), (ROW_EXPNEG, exp_neg),
                       (ROW_WK, wk), (ROW_DECAY, decay)):
            coef_ref[d, r] = val

    def coef_row(d, r, c):
        return coef_ref[d, r, pl.ds(c, 1), :]

    def chunk_operands(c):
        off = pl.multiple_of(c * CHUNK, CHUNK)
        return off, qt_ref[0, 0, :, pl.ds(off, CHUNK)], k_ref[0, 0, pl.ds(off, CHUNK), :], \
            vt_ref[0, 0, :, pl.ds(off, CHUNK)]

    key_pos = lax.broadcasted_iota(jnp.int32, (CHUNK, CHUNK), 0)
    qry_pos = lax.broadcasted_iota(jnp.int32, (CHUNK, CHUNK), 1)
    visible = (key_pos <= qry_pos, key_pos >= qry_pos)

    def weighted_scores(c):
        _, qt_c, k_c, _ = chunk_operands(c)
        scores = jnp.dot(k_c, qt_c, preferred_element_type=F32)
        for d in range(N_DIRS):
            a_col = jnp.broadcast_to(coef_row(d, ROW_A, c), (CHUNK, CHUNK)).T
            weight = jnp.exp2(jnp.where(visible[d], a_col - coef_row(d, ROW_M, c), -jnp.inf))
            st_ref[d, c] = (scores * weight).astype(BF16)

    def state_step(d, c, state):
        _, _, k_c, vt_c = chunk_operands(c)
        sprev_ref[d, c] = state.astype(BF16)
        vw = vt_c * coef_row(d, ROW_WK, c).astype(BF16)
        return coef_row(d, ROW_DECAY, c) * state + jnp.dot(vw, k_c, preferred_element_type=F32)

    def state_body(i, states):
        s_fwd, s_bwd = states
        for j in range(CHUNKS_PER_STEP):
            c = i * CHUNKS_PER_STEP + j
            weighted_scores(c)
            s_fwd = state_step(0, c, s_fwd)
            s_bwd = state_step(1, n_chunks - 1 - c, s_bwd)
        return s_fwd, s_bwd

    zero_state = jnp.zeros((AUG_ROWS, HEAD_DIM), F32)
    lax.fori_loop(0, n_chunks // CHUNKS_PER_STEP, state_body, (zero_state, zero_state))

    def output_chunk(c):
        off, qt_c, _, vt_c = chunk_operands(c)
        hsum = None
        for d in range(N_DIRS):
            qa = (qt_c.astype(F32) * coef_row(d, ROW_AINTER, c)).astype(BF16)
            lhs = jnp.concatenate([vt_c, sprev_ref[d, c]], axis=1)
            rhs = jnp.concatenate([st_ref[d, c], qa], axis=0)
            numden = jnp.dot(lhs, rhs, preferred_element_type=F32)
            den = numden[HEAD_DIM:HEAD_DIM + 1, :]
            ht = numden[0:HEAD_DIM, :] * (1.0 / jnp.maximum(jnp.abs(den), coef_row(d, ROW_EXPNEG, c)))
            hsum = ht if hsum is None else hsum + ht
        out_ref[0, 0, pl.ds(off, CHUNK), :] = hsum.T.astype(BF16)

    def output_body(i, carry):
        for j in range(CHUNKS_PER_STEP):
            output_chunk(i * CHUNKS_PER_STEP + j)
        return carry

    lax.fori_loop(0, n_chunks // CHUNKS_PER_STEP, output_body, 0)


def _mlstm(qt, k, vt, gates, gbias):
    batch, heads, hd, seq = qt.shape
    n_chunks = seq // CHUNK
    assert n_chunks % CHUNKS_PER_STEP == 0
    spec_t = pl.BlockSpec((1, 1, hd, seq), lambda b, h: (b, h, 0, 0))
    spec_n = pl.BlockSpec((1, 1, seq, hd), lambda b, h: (b, h, 0, 0))
    spec_aug = pl.BlockSpec((1, 1, AUG_ROWS, seq), lambda b, h: (b, h, 0, 0))
    return pl.pallas_call(
        _mlstm_kernel,
        grid=(batch, heads),
        in_specs=[spec_t, spec_n, spec_aug,
                  pl.BlockSpec((1, 1, N_GATE_ROWS, n_chunks, CHUNK), lambda b, h: (b, h, 0, 0, 0)),
                  pl.BlockSpec((1, N_GATE_ROWS, CHUNK), lambda b, h: (h, 0, 0))],
        out_specs=spec_n,
        out_shape=jax.ShapeDtypeStruct((batch, heads, seq, hd), BF16),
        scratch_shapes=[pltpu.VMEM((N_DIRS, N_COEF, n_chunks, CHUNK), F32),
                        pltpu.VMEM((N_DIRS, n_chunks, CHUNK), F32),
                        pltpu.VMEM((N_DIRS, n_chunks, AUG_ROWS, hd), BF16),
                        pltpu.VMEM((N_DIRS, n_chunks, CHUNK, CHUNK), BF16)],
        compiler_params=pltpu.CompilerParams(dimension_semantics=("arbitrary", "arbitrary"),
                                             vmem_limit_bytes=VMEM_LIMIT_BYTES),
    )(qt, k, vt, gates, gbias)


def _outffn_kernel(x_ref, hm_ref, o_ref, u_ref, uprev_ref, unext_ref, mod_ref,
                   ghead_ref, pscale_ref, gffn_ref, gfinal_ref,
                   band_ref, wpool_ref, wout_ref, wff1_ref, wff2_ref, out_ref,
                   ubuf_ref, ycat_ref, *, seq, n_tiles, final_norm):
    tm = x_ref.shape[1]
    dm = N_HEADS * HEAD_DIM
    step = pl.program_id(0)
    tiles_per_seq = seq // tm
    tile = step % tiles_per_seq

    def matmul_stage(drain):
        yb = []
        for gi in range(len(POOL_WINDOWS)):
            cols = slice(gi * HEAD_DIM, (gi + 1) * HEAD_DIM)
            mapped = jnp.dot(ycat_ref[drain, :, dm + gi * HEAD_DIM:dm + (gi + 1) * HEAD_DIM], wpool_ref[gi],
                             preferred_element_type=F32)
            yb.append((mapped * pscale_ref[:, cols]).astype(BF16))
        gate1 = mod_ref[0, 2:3, :]
        yproj = (jnp.dot(ycat_ref[drain, :, 0:dm], wout_ref[0:dm, :], preferred_element_type=F32)
                 + jnp.dot(jnp.concatenate(yb, axis=1), wout_ref[dm:, :], preferred_element_type=F32))
        x1 = x_ref[0] + gate1 * yproj

        shift2 = mod_ref[0, 3:4, :]
        scale2 = mod_ref[0, 4:5, :]
        gate2 = mod_ref[0, 5:6, :]
        h2 = ((_rms_scale(x1) * gffn_ref[...]) * (1.0 + scale2) + shift2).astype(BF16)
        acc = jnp.zeros(x1.shape, F32)
        for j in range(wff1_ref.shape[1] // FF_COLS):
            hid = jnp.dot(h2, wff1_ref[:, j * FF_COLS:(j + 1) * FF_COLS], preferred_element_type=F32)
            hid = jnp.square(jnp.maximum(hid, 0.0)).astype(BF16)
            acc = acc + jnp.dot(hid, wff2_ref[j * FF_COLS:(j + 1) * FF_COLS, :], preferred_element_type=F32)
        x2 = x1 + gate2 * acc
        if final_norm:
            x2 = _rms_scale(x2) * gfinal_ref[...]
        out_ref[0] = x2

    def elementwise_stage(fill):
        for hd in range(N_HEADS):
            cols = slice(hd * HEAD_DIM, (hd + 1) * HEAD_DIM)
            hn = _rms_scale(hm_ref[0, hd].astype(F32)) * ghead_ref[:, cols]
            gate = jax.nn.sigmoid(o_ref[0, :, cols].astype(F32))
            ycat_ref[fill, :, cols] = (gate * hn).astype(BF16)

        ubuf_ref[0:POOL_HALO, :] = jnp.where(tile > 0, uprev_ref[0], jnp.zeros_like(uprev_ref[0]))
        ubuf_ref[POOL_HALO:POOL_HALO + tm, :] = u_ref[0]
        ubuf_ref[POOL_HALO + tm:, :] = jnp.where(tile < tiles_per_seq - 1, unext_ref[0],
                                                 jnp.zeros_like(unext_ref[0]))
        for gi in range(len(POOL_WINDOWS)):
            cols = slice(gi * HEAD_DIM, (gi + 1) * HEAD_DIM)
            for r0 in range(0, tm, POOL_BLOCK):
                ext = ubuf_ref[r0:r0 + POOL_BLOCK + 2 * POOL_HALO, cols]
                mixed = jnp.dot(band_ref[gi], ext, preferred_element_type=F32)
                ycat_ref[fill, r0:r0 + POOL_BLOCK, dm + gi * HEAD_DIM:dm + (gi + 1) * HEAD_DIM] = mixed.astype(BF16)

    def sequence_end_rows(fill):
        def clipped_rows(r0):
            ext = ubuf_ref[r0:r0 + 3 * POOL_HALO, :].astype(F32)
            pos = tile * tm + r0 + lax.broadcasted_iota(jnp.int32, (POOL_HALO, HEAD_DIM), 0)
            for gi, win in enumerate(POOL_WINDOWS):
                half = win // 2
                grp = ext[:, gi * HEAD_DIM:(gi + 1) * HEAD_DIM]
                total = grp[POOL_HALO - half:2 * POOL_HALO - half]
                for j in range(1 - half, half):
                    total = total + grp[POOL_HALO + j:2 * POOL_HALO + j]
                cnt = jnp.minimum(pos + half, seq) - jnp.maximum(pos - half, 0)
                mixed = total / cnt.astype(F32) - grp[POOL_HALO:2 * POOL_HALO]
                ycat_ref[fill, r0:r0 + POOL_HALO, dm + gi * HEAD_DIM:dm + (gi + 1) * HEAD_DIM] = mixed.astype(BF16)

        @pl.when(tile == 0)
        def _():
            clipped_rows(0)

        @pl.when(tile == tiles_per_seq - 1)
        def _():
            clipped_rows(tm - POOL_HALO)

    @pl.when(step == 0)
    def _():
        elementwise_stage(0)
        sequence_end_rows(0)

    @pl.when(step == n_tiles)
    def _():
        matmul_stage((n_tiles - 1) % 2)

    for fill in range(2):
        @pl.when((step > 0) & (step < n_tiles) & (step % 2 == fill))
        def _():
            matmul_stage(1 - fill)
            elementwise_stage(fill)
            sequence_end_rows(fill)


def _pool_bands():
    row = np.arange(POOL_BLOCK)[:, None]
    col = np.arange(POOL_BLOCK + 2 * POOL_HALO)[None, :] - POOL_HALO
    bands = [((col >= row - win // 2) & (col < row + win // 2)) / win - (col == row) for win in POOL_WINDOWS]
    return jnp.asarray(np.stack(bands), dtype=BF16)


def _outffn(x, hm, o, u, mod, g_head, pool_scale, g_ffn, g_final, w_pool, w_out, w_ff1, w_ff2,
            final_norm):
    batch, seq, d = x.shape
    tm = OUTFFN_ROWS
    dm = N_HEADS * HEAD_DIM
    d_pool = u.shape[2]
    halo_per_tile = tm // POOL_HALO
    n_halo_blocks = seq // POOL_HALO
    tiles_per_seq = seq // tm
    n_tiles = batch * tiles_per_seq

    def mm_tile(g):
        t = jnp.maximum(g - 1, 0)
        return t // tiles_per_seq, t % tiles_per_seq

    def ew_tile(g):
        t = jnp.minimum(g, n_tiles - 1)
        return t // tiles_per_seq, t % tiles_per_seq

    def mm_rows(g):
        b, i = mm_tile(g)
        return b, i, 0

    def ew_rows(g):
        b, i = ew_tile(g)
        return b, i, 0

    def ew_heads(g):
        b, i = ew_tile(g)
        return b, 0, i, 0

    def ew_prev(g):
        b, i = ew_tile(g)
        return b, jnp.maximum(i * halo_per_tile - 1, 0), 0

    def ew_next(g):
        b, i = ew_tile(g)
        return b, jnp.minimum((i + 1) * halo_per_tile, n_halo_blocks - 1), 0

    bands = _pool_bands()
    kernel = functools.partial(_outffn_kernel, seq=seq, n_tiles=n_tiles, final_norm=final_norm)
    return pl.pallas_call(
        kernel,
        grid=(n_tiles + 1,),
        in_specs=[pl.BlockSpec((1, tm, d), mm_rows),
                  pl.BlockSpec((1, N_HEADS, tm, HEAD_DIM), ew_heads),
                  pl.BlockSpec((1, tm, dm), ew_rows),
                  pl.BlockSpec((1, tm, d_pool), ew_rows),
                  pl.BlockSpec((1, POOL_HALO, d_pool), ew_prev),
                  pl.BlockSpec((1, POOL_HALO, d_pool), ew_next),
                  pl.BlockSpec((1,) + mod.shape[1:], lambda g: (mm_tile(g)[0], 0, 0)),
                  _const_spec((1, dm)),
                  _const_spec((1, d_pool)),
                  _const_spec((1, d)),
                  _const_spec((1, d)),
                  _const_spec(bands.shape),
                  _const_spec(w_pool.shape),
                  _const_spec(w_out.shape),
                  _const_spec(w_ff1.shape),
                  _const_spec(w_ff2.shape)],
        out_specs=pl.BlockSpec((1, tm, d), mm_rows),
        out_shape=jax.ShapeDtypeStruct((batch, seq, d), F32),
        scratch_shapes=[pltpu.VMEM((tm + 2 * POOL_HALO, d_pool), BF16),
                        pltpu.VMEM((2, tm, dm + d_pool), BF16)],
        compiler_params=pltpu.CompilerParams(dimension_semantics=("arbitrary",),
                                             vmem_limit_bytes=VMEM_LIMIT_BYTES),
    )(x, hm, o, u, u, u, mod, g_head.reshape(1, dm), pool_scale.reshape(1, d_pool),
      g_ffn.reshape(1, d), g_final.reshape(1, d), bands, w_pool, w_out, w_ff1, w_ff2)


def _pack_w_in(w_in):
    dm = N_HEADS * HEAD_DIM
    n_gates = N_DIRS * 2 * N_HEADS
    main = w_in[:, :4 * dm]
    gates = w_in[:, 4 * dm:4 * dm + n_gates]
    pool = w_in[:, 4 * dm + n_gates:]
    gates = gates.reshape(-1, N_DIRS, 2, N_HEADS).transpose(3, 1, 2, 0).reshape(n_gates, -1)
    return jnp.concatenate([main, pool], axis=1).astype(BF16), gates.astype(BF16)


def _pack_gate_bias(b_igate, b_fgate):
    rows = jnp.stack([b_igate, b_fgate], axis=1)
    rows = rows.transpose(2, 0, 1).reshape(N_HEADS, N_GATE_ROWS)
    return jnp.broadcast_to(rows[:, :, None], (N_HEADS, N_GATE_ROWS, CHUNK)).astype(F32)


def kernel(x, c, w_ada, b_ada, g_mix, w_in, b_igate, b_fgate, g_head, w_pool, pool_scale, w_out,
           g_ffn, w_ff1, w_ff2, g_final):
    depth = w_ada.shape[0]
    batch, seq, d = x.shape
    d_pool = pool_scale.shape[1]
    assert g_head.shape[1] == N_HEADS * HEAD_DIM and w_pool.shape[1] == len(POOL_WINDOWS)
    assert seq % INPROJ_ROWS == 0 and seq % OUTFFN_ROWS == 0 and w_ff1.shape[2] % FF_COLS == 0
    for l in range(depth):
        mod = _adaln(c, w_ada[l], b_ada[l]).reshape(batch, 6, d)
        qt, k, vt, o, u, gates = _inproj(x, mod, g_mix[l], *_pack_w_in(w_in[l]), d_pool)
        hm = _mlstm(qt, k, vt, gates, _pack_gate_bias(b_igate[l], b_fgate[l]))
        x = _outffn(x, hm, o, u, mod, g_head[l], pool_scale[l], g_ffn[l], g_final,
                    w_pool[l].astype(BF16), w_out[l].astype(BF16), w_ff1[l].astype(BF16),
                    w_ff2[l].astype(BF16), final_norm=(l == depth - 1))
    return x
```

```python
import functools

import jax
import jax.numpy as jnp
import numpy as np
from jax import lax
from jax.experimental import pallas as pl
from jax.experimental.pallas import tpu as pltpu

F32 = jnp.float32
BF16 = jnp.bfloat16

LANES = 128
BF16_SUBLANES = 16
VMEM_LIMIT_BYTES = 56 * 1024 * 1024

N_HEADS = 4
HEAD_DIM = 128
CHUNK = 128
POOL_WINDOWS = (2, 4, 8, 16)
POOL_BLOCK = 128
POOL_HALO = 16
N_DIRS = 2
N_GATE_ROWS = 2 * N_DIRS
EPS = 1e-6
LOG2_E = 1.4426950408889634

INPROJ_ROWS = 1024
OUTFFN_ROWS = 512
FF_COLS = 1024
AUG_ROWS = HEAD_DIM + BF16_SUBLANES

ROW_A, ROW_M, ROW_AINTER, ROW_EXPNEG, ROW_WK, ROW_DECAY = range(6)
N_COEF = 6
CHUNKS_PER_STEP = 32


def _const_spec(shape):
    zeros = (0,) * len(shape)
    return pl.BlockSpec(shape, lambda *_: zeros, pipeline_mode=pl.Buffered(1))


def _rms_scale(x):
    return x * lax.rsqrt(jnp.mean(x * x, axis=-1, keepdims=True) + EPS)


def _modulated_norm_bf16(x, gain, shift, scale):
    slope = (gain * (1.0 + scale)).astype(BF16)
    return _rms_scale(x).astype(BF16) * slope + shift.astype(BF16)


def _adaln_kernel(c_ref, w_ref, b_ref, o_ref):
    c = c_ref[...]
    act = c * jax.nn.sigmoid(c)
    o_ref[...] = jnp.dot(act.astype(BF16), w_ref[...].astype(BF16),
                         preferred_element_type=F32) + b_ref[...]


def _adaln(c, w_ada, b_ada):
    batch, d = c.shape
    n = w_ada.shape[1]
    return pl.pallas_call(
        _adaln_kernel,
        grid=(n // d,),
        in_specs=[pl.BlockSpec((batch, d), lambda j: (0, 0)),
                  pl.BlockSpec((d, d), lambda j: (0, j)),
                  pl.BlockSpec((1, d), lambda j: (0, j))],
        out_specs=pl.BlockSpec((batch, d), lambda j: (0, j)),
        out_shape=jax.ShapeDtypeStruct((batch, n), F32),
        compiler_params=pltpu.CompilerParams(dimension_semantics=("arbitrary",),
                                             vmem_limit_bytes=VMEM_LIMIT_BYTES),
    )(c, w_ada, b_ada.reshape(1, n))


def _inproj_kernel(x_ref, mod_ref, g_ref, w_ref, wt_ref, qt_ref, k_ref, vt_ref, o_ref, u_ref, gates_ref):
    dm = N_HEADS * HEAD_DIM
    x = x_ref[0]
    shift = mod_ref[0, 0:1, :]
    scale = mod_ref[0, 1:2, :]
    hb = _modulated_norm_bf16(x, g_ref[...], shift, scale)

    by_feature = lax.dot_general(wt_ref[...], hb, (((1,), (1,)), ((), ())), preferred_element_type=F32)
    for hd in range(N_HEADS):
        rows = slice(hd * HEAD_DIM, (hd + 1) * HEAD_DIM)
        qt_ref[0, hd] = (by_feature[rows, :] * (HEAD_DIM ** -0.5)).astype(BF16)
        vt_ref[0, hd, 0:HEAD_DIM, :] = by_feature[dm + hd * HEAD_DIM:dm + (hd + 1) * HEAD_DIM, :].astype(BF16)
        vt_ref[0, hd, HEAD_DIM:AUG_ROWS, :] = jnp.ones((AUG_ROWS - HEAD_DIM, x.shape[0]), BF16)
    gt = by_feature[2 * dm:, :]
    by_token = jnp.dot(hb, w_ref[...], preferred_element_type=F32)
    for hd in range(N_HEADS):
        k_ref[0, hd] = by_token[:, hd * HEAD_DIM:(hd + 1) * HEAD_DIM].astype(BF16)
    o_ref[0] = by_token[:, dm:2 * dm].astype(BF16)
    u_ref[0] = by_token[:, 2 * dm:].astype(BF16)
    n_chunks = x.shape[0] // CHUNK
    for hd in range(N_HEADS):
        for r in range(N_GATE_ROWS):
            row = hd * N_GATE_ROWS + r
            for j in range(n_chunks):
                gates_ref[0, hd, r, j:j + 1, :] = gt[row:row + 1, j * CHUNK:(j + 1) * CHUNK]


def _inproj(x, mod, g_mix, w_cat, w_gates_t, d_pool):
    batch, seq, d = x.shape
    tm = INPROJ_ROWS
    dm = N_HEADS * HEAD_DIM
    grid = (batch, seq // tm)
    head_t = jax.ShapeDtypeStruct((batch, N_HEADS, HEAD_DIM, seq), BF16)
    head_n = jax.ShapeDtypeStruct((batch, N_HEADS, seq, HEAD_DIM), BF16)
    head_aug = jax.ShapeDtypeStruct((batch, N_HEADS, AUG_ROWS, seq), BF16)
    out_shape = (head_t, head_n, head_aug,
                 jax.ShapeDtypeStruct((batch, seq, dm), BF16),
                 jax.ShapeDtypeStruct((batch, seq, d_pool), BF16),
                 jax.ShapeDtypeStruct((batch, N_HEADS, N_GATE_ROWS, seq // CHUNK, CHUNK), F32))
    spec_t = pl.BlockSpec((1, N_HEADS, HEAD_DIM, tm), lambda b, i: (b, 0, 0, i))
    spec_n = pl.BlockSpec((1, N_HEADS, tm, HEAD_DIM), lambda b, i: (b, 0, i, 0))
    spec_aug = pl.BlockSpec((1, N_HEADS, AUG_ROWS, tm), lambda b, i: (b, 0, 0, i))
    out_specs = (spec_t, spec_n, spec_aug,
                 pl.BlockSpec((1, tm, dm), lambda b, i: (b, i, 0)),
                 pl.BlockSpec((1, tm, d_pool), lambda b, i: (b, i, 0)),
                 pl.BlockSpec((1, N_HEADS, N_GATE_ROWS, tm // CHUNK, CHUNK), lambda b, i: (b, 0, 0, i, 0)))
    return pl.pallas_call(
        _inproj_kernel,
        grid=grid,
        in_specs=[pl.BlockSpec((1, tm, d), lambda b, i: (b, i, 0)),
                  pl.BlockSpec((1,) + mod.shape[1:], lambda b, i: (b, 0, 0)),
                  _const_spec((1, d)),
                  _const_spec(w_cat.shape),
                  _const_spec(w_gates_t.shape)],
        out_specs=out_specs,
        out_shape=out_shape,
        compiler_params=pltpu.CompilerParams(dimension_semantics=("arbitrary", "arbitrary"),
                                             vmem_limit_bytes=VMEM_LIMIT_BYTES),
    )(x, mod, g_mix.reshape(1, d), w_cat, w_gates_t)


def _time_scan(x, op, fill, reverse):
    row = lax.broadcasted_iota(jnp.int32, x.shape, 0)
    step = 1
    while step < CHUNK:
        if reverse:
            shifted = jnp.where(row < CHUNK - step, pltpu.roll(x, CHUNK - step, axis=0), fill)
        else:
            shifted = jnp.where(row >= step, pltpu.roll(x, step, axis=0), fill)
        x = op(x, shifted)
        step *= 2
    return x


def _mlstm_kernel(qt_ref, k_ref, vt_ref, gates_ref, gbias_ref, out_ref,
                  coef_ref, mprev_ref, sprev_ref, st_ref):
    n_chunks = gates_ref.shape[3]
    assert 2 * N_DIRS * n_chunks == LANES

    log_i, log_f = [], []
    for d in range(N_DIRS):
        log_i.append(gates_ref[0, 0, 2 * d] + gbias_ref[0, 2 * d:2 * d + 1, :])
        z = gates_ref[0, 0, 2 * d + 1] + gbias_ref[0, 2 * d + 1:2 * d + 2, :]
        log_f.append(-(jnp.maximum(-z, 0.0) + jnp.log1p(jnp.exp(-jnp.abs(z)))))
    by_time = jnp.concatenate(log_f + log_i, axis=0).T
    lane = lax.broadcasted_iota(jnp.int32, by_time.shape, 1)
    fwd_lane = (lane % (N_DIRS * n_chunks)) < n_chunks
    b_t = jnp.where(fwd_lane, _time_scan(by_time, jnp.add, 0.0, False),
                    _time_scan(by_time, jnp.add, 0.0, True))
    a_t = pltpu.roll(by_time, N_DIRS * n_chunks, axis=1) - b_t
    cm_t = jnp.where(fwd_lane, _time_scan(a_t, jnp.maximum, -jnp.inf, False),
                     _time_scan(a_t, jnp.maximum, -jnp.inf, True))
    b_rows, a_rows, cm_rows = b_t.T, a_t.T, cm_t.T

    for d in range(N_DIRS):
        reverse = d == 1
        last = 0 if reverse else CHUNK - 1
        b = b_rows[d * n_chunks:(d + 1) * n_chunks]
        a = a_rows[d * n_chunks:(d + 1) * n_chunks]
        cm = cm_rows[d * n_chunks:(d + 1) * n_chunks]
        b_last = jnp.broadcast_to(b[:, last:last + 1], b.shape)
        cm_last = jnp.broadcast_to(cm[:, last:last + 1], b.shape)
        m = jnp.zeros((1, CHUNK), F32)
        order = range(n_chunks - 1, -1, -1) if reverse else range(n_chunks)
        for c in order:
            mprev_ref[d, c:c + 1, :] = m
            m = b_last[c:c + 1, :] + jnp.maximum(m, cm_last[c:c + 1, :])
        m_prev = mprev_ref[d]
        m_row = jnp.maximum(m_prev, cm)
        a_inter = jnp.exp(m_prev - m_row)
        exp_neg = jnp.exp(-(b + m_row))
        m_end = jnp.broadcast_to(m_row[:, last:last + 1], b.shape)
        wk = jnp.exp(a - m_end)
        decay = jnp.broadcast_to(a_inter[:, last:last + 1], b.shape)
        for r, val in ((ROW_A, a * LOG2_E), (ROW_M, m_row * LOG2_E), (ROW_AINTER, a_inter), (ROW_EXPNEG, exp_neg),
                       (ROW_WK, wk), (ROW_DECAY, decay)):
            coef_ref[d, r] = val

    def coef_row(d, r, c):
        return coef_ref[d, r, pl.ds(c, 1), :]

    def chunk_operands(c):
        off = pl.multiple_of(c * CHUNK, CHUNK)
        return off, qt_ref[0, 0, :, pl.ds(off, CHUNK)], k_ref[0, 0, pl.ds(off, CHUNK), :], \
            vt_ref[0, 0, :, pl.ds(off, CHUNK)]

    key_pos = lax.broadcasted_iota(jnp.int32, (CHUNK, CHUNK), 0)
    qry_pos = lax.broadcasted_iota(jnp.int32, (CHUNK, CHUNK), 1)
    visible = (key_pos <= qry_pos, key_pos >= qry_pos)

    def weighted_scores(c):
        _, qt_c, k_c, _ = chunk_operands(c)
        scores = jnp.dot(k_c, qt_c, preferred_element_type=F32)
        for d in range(N_DIRS):
            a_col = jnp.broadcast_to(coef_row(d, ROW_A, c), (CHUNK, CHUNK)).T
            weight = jnp.exp2(jnp.where(visible[d], a_col - coef_row(d, ROW_M, c), -jnp.inf))
            st_ref[d, c] = (scores * weight).astype(BF16)

    def state_step(d, c, state):
        _, _, k_c, vt_c = chunk_operands(c)
        sprev_ref[d, c] = state.astype(BF16)
        vw = vt_c * coef_row(d, ROW_WK, c).astype(BF16)
        return coef_row(d, ROW_DECAY, c) * state + jnp.dot(vw, k_c, preferred_element_type=F32)

    def state_body(i, states):
        s_fwd, s_bwd = states
        for j in range(CHUNKS_PER_STEP):
            c = i * CHUNKS_PER_STEP + j
            weighted_scores(c)
            s_fwd = state_step(0, c, s_fwd)
            s_bwd = state_step(1, n_chunks - 1 - c, s_bwd)
        return s_fwd, s_bwd

    zero_state = jnp.zeros((AUG_ROWS, HEAD_DIM), F32)
    lax.fori_loop(0, n_chunks // CHUNKS_PER_STEP, state_body, (zero_state, zero_state))

    def output_chunk(c):
        off, qt_c, _, vt_c = chunk_operands(c)
        hsum = None
        for d in range(N_DIRS):
            qa = (qt_c.astype(F32) * coef_row(d, ROW_AINTER, c)).astype(BF16)
            lhs = jnp.concatenate([vt_c, sprev_ref[d, c]], axis=1)
            rhs = jnp.concatenate([st_ref[d, c], qa], axis=0)
            numden = jnp.dot(lhs, rhs, preferred_element_type=F32)
            den = numden[HEAD_DIM:HEAD_DIM + 1, :]
            ht = numden[0:HEAD_DIM, :] * (1.0 / jnp.maximum(jnp.abs(den), coef_row(d, ROW_EXPNEG, c)))
            hsum = ht if hsum is None else hsum + ht
        out_ref[0, 0, pl.ds(off, CHUNK), :] = hsum.T.astype(BF16)

    def output_body(i, carry):
        for j in range(CHUNKS_PER_STEP):
            output_chunk(i * CHUNKS_PER_STEP + j)
        return carry

    lax.fori_loop(0, n_chunks // CHUNKS_PER_STEP, output_body, 0)


def _mlstm(qt, k, vt, gates, gbias):
    batch, heads, hd, seq = qt.shape
    n_chunks = seq // CHUNK
    assert n_chunks % CHUNKS_PER_STEP == 0
    spec_t = pl.BlockSpec((1, 1, hd, seq), lambda b, h: (b, h, 0, 0))
    spec_n = pl.BlockSpec((1, 1, seq, hd), lambda b, h: (b, h, 0, 0))
    spec_aug = pl.BlockSpec((1, 1, AUG_ROWS, seq), lambda b, h: (b, h, 0, 0))
    return pl.pallas_call(
        _mlstm_kernel,
        grid=(batch, heads),
        in_specs=[spec_t, spec_n, spec_aug,
                  pl.BlockSpec((1, 1, N_GATE_ROWS, n_chunks, CHUNK), lambda b, h: (b, h, 0, 0, 0)),
                  pl.BlockSpec((1, N_GATE_ROWS, CHUNK), lambda b, h: (h, 0, 0))],
        out_specs=spec_n,
        out_shape=jax.ShapeDtypeStruct((batch, heads, seq, hd), BF16),
        scratch_shapes=[pltpu.VMEM((N_DIRS, N_COEF, n_chunks, CHUNK), F32),
                        pltpu.VMEM((N_DIRS, n_chunks, CHUNK), F32),
                        pltpu.VMEM((N_DIRS, n_chunks, AUG_ROWS, hd), BF16),
                        pltpu.VMEM((N_DIRS, n_chunks, CHUNK, CHUNK), BF16)],
        compiler_params=pltpu.CompilerParams(dimension_semantics=("arbitrary", "arbitrary"),
                                             vmem_limit_bytes=VMEM_LIMIT_BYTES),
    )(qt, k, vt, gates, gbias)


def _outffn_kernel(x_ref, hm_ref, o_ref, u_ref, uprev_ref, unext_ref, mod_ref,
                   ghead_ref, pscale_ref, gffn_ref, gfinal_ref,
                   band_ref, wpool_ref, wout_ref, wff1_ref, wff2_ref, out_ref,
                   ubuf_ref, ycat_ref, *, seq, n_tiles, final_norm):
    tm = x_ref.shape[1]
    dm = N_HEADS * HEAD_DIM
    step = pl.program_id(0)
    tiles_per_seq = seq // tm
    tile = step % tiles_per_seq

    def matmul_stage(drain):
        yb = []
        for gi in range(len(POOL_WINDOWS)):
            cols = slice(gi * HEAD_DIM, (gi + 1) * HEAD_DIM)
            mapped = jnp.dot(ycat_ref[drain, :, dm + gi * HEAD_DIM:dm + (gi + 1) * HEAD_DIM], wpool_ref[gi],
                             preferred_element_type=F32)
            yb.append((mapped * pscale_ref[:, cols]).astype(BF16))
        gate1 = mod_ref[0, 2:3, :]
        yproj = (jnp.dot(ycat_ref[drain, :, 0:dm], wout_ref[0:dm, :], preferred_element_type=F32)
                 + jnp.dot(jnp.concatenate(yb, axis=1), wout_ref[dm:, :], preferred_element_type=F32))
        x1 = x_ref[0] + gate1 * yproj

        shift2 = mod_ref[0, 3:4, :]
        scale2 = mod_ref[0, 4:5, :]
        gate2 = mod_ref[0, 5:6, :]
        h2 = ((_rms_scale(x1) * gffn_ref[...]) * (1.0 + scale2) + shift2).astype(BF16)
        acc = jnp.zeros(x1.shape, F32)
        for j in range(wff1_ref.shape[1] // FF_COLS):
            hid = jnp.dot(h2, wff1_ref[:, j * FF_COLS:(j + 1) * FF_COLS], preferred_element_type=F32)
            hid = jnp.square(jnp.maximum(hid, 0.0)).astype(BF16)
            acc = acc + jnp.dot(hid, wff2_ref[j * FF_COLS:(j + 1) * FF_COLS, :], preferred_element_type=F32)
        x2 = x1 + gate2 * acc
        if final_norm:
            x2 = _rms_scale(x2) * gfinal_ref[...]
        out_ref[0] = x2

    def elementwise_stage(fill):
        for hd in range(N_HEADS):
            cols = slice(hd * HEAD_DIM, (hd + 1) * HEAD_DIM)
            hn = _rms_scale(hm_ref[0, hd].astype(F32)) * ghead_ref[:, cols]
            gate = jax.nn.sigmoid(o_ref[0, :, cols].astype(F32))
            ycat_ref[fill, :, cols] = (gate * hn).astype(BF16)

        ubuf_ref[0:POOL_HALO, :] = jnp.where(tile > 0, uprev_ref[0], jnp.zeros_like(uprev_ref[0]))
        ubuf_ref[POOL_HALO:POOL_HALO + tm, :] = u_ref[0]
        ubuf_ref[POOL_HALO + tm:, :] = jnp.where(tile < tiles_per_seq - 1, unext_ref[0],
                                                 jnp.zeros_like(unext_ref[0]))
        for gi in range(len(POOL_WINDOWS)):
            cols = slice(gi * HEAD_DIM, (gi + 1) * HEAD_DIM)
            for r0 in range(0, tm, POOL_BLOCK):
                ext = ubuf_ref[r0:r0 + POOL_BLOCK + 2 * POOL_HALO, cols]
                mixed = jnp.dot(band_ref[gi], ext, preferred_element_type=F32)
                ycat_ref[fill, r0:r0 + POOL_BLOCK, dm + gi * HEAD_DIM:dm + (gi + 1) * HEAD_DIM] = mixed.astype(BF16)

    def sequence_end_rows(fill):
        def clipped_rows(r0):
            ext = ubuf_ref[r0:r0 + 3 * POOL_HALO, :].astype(F32)
            pos = tile * tm + r0 + lax.broadcasted_iota(jnp.int32, (POOL_HALO, HEAD_DIM), 0)
            for gi, win in enumerate(POOL_WINDOWS):
                half = win // 2
                grp = ext[:, gi * HEAD_DIM:(gi + 1) * HEAD_DIM]
                total = grp[POOL_HALO - half:2 * POOL_HALO - half]
                for j in range(1 - half, half):
                    total = total + grp[POOL_HALO + j:2 * POOL_HALO + j]
                cnt = jnp.minimum(pos + half, seq) - jnp.maximum(pos - half, 0)
                mixed = total / cnt.astype(F32) - grp[POOL_HALO:2 * POOL_HALO]
                ycat_ref[fill, r0:r0 + POOL_HALO, dm + gi * HEAD_DIM:dm + (gi + 1) * HEAD_DIM] = mixed.astype(BF16)

        @pl.when(tile == 0)
        def _():
            clipped_rows(0)

        @pl.when(tile == tiles_per_seq - 1)
        def _():
            clipped_rows(tm - POOL_HALO)

    @pl.when(step == 0)
    def _():
        elementwise_stage(0)
        sequence_end_rows(0)

    @pl.when(step == n_tiles)
    def _():
        matmul_stage((n_tiles - 1) % 2)

    for fill in range(2):
        @pl.when((step > 0) & (step < n_tiles) & (step % 2 == fill))
        def _():
            matmul_stage(1 - fill)
            elementwise_stage(fill)
            sequence_end_rows(fill)


def _pool_bands():
    row = np.arange(POOL_BLOCK)[:, None]
    col = np.arange(POOL_BLOCK + 2 * POOL_HALO)[None, :] - POOL_HALO
    bands = [((col >= row - win // 2) & (col < row + win // 2)) / win - (col == row) for win in POOL_WINDOWS]
    return jnp.asarray(np.stack(bands), dtype=BF16)


def _outffn(x, hm, o, u, mod, g_head, pool_scale, g_ffn, g_final, w_pool, w_out, w_ff1, w_ff2,
            final_norm):
    batch, seq, d = x.shape
    tm = OUTFFN_ROWS
    dm = N_HEADS * HEAD_DIM
    d_pool = u.shape[2]
    halo_per_tile = tm // POOL_HALO
    n_halo_blocks = seq // POOL_HALO
    tiles_per_seq = seq // tm
    n_tiles = batch * tiles_per_seq

    def mm_tile(g):
        t = jnp.maximum(g - 1, 0)
        return t // tiles_per_seq, t % tiles_per_seq

    def ew_tile(g):
        t = jnp.minimum(g, n_tiles - 1)
        return t // tiles_per_seq, t % tiles_per_seq

    def mm_rows(g):
        b, i = mm_tile(g)
        return b, i, 0

    def ew_rows(g):
        b, i = ew_tile(g)
        return b, i, 0

    def ew_heads(g):
        b, i = ew_tile(g)
        return b, 0, i, 0

    def ew_prev(g):
        b, i = ew_tile(g)
        return b, jnp.maximum(i * halo_per_tile - 1, 0), 0

    def ew_next(g):
        b, i = ew_tile(g)
        return b, jnp.minimum((i + 1) * halo_per_tile, n_halo_blocks - 1), 0

    bands = _pool_bands()
    kernel = functools.partial(_outffn_kernel, seq=seq, n_tiles=n_tiles, final_norm=final_norm)
    return pl.pallas_call(
        kernel,
        grid=(n_tiles + 1,),
        in_specs=[pl.BlockSpec((1, tm, d), mm_rows),
                  pl.BlockSpec((1, N_HEADS, tm, HEAD_DIM), ew_heads),
                  pl.BlockSpec((1, tm, dm), ew_rows),
                  pl.BlockSpec((1, tm, d_pool), ew_rows),
                  pl.BlockSpec((1, POOL_HALO, d_pool), ew_prev),
                  pl.BlockSpec((1, POOL_HALO, d_pool), ew_next),
                  pl.BlockSpec((1,) + mod.shape[1:], lambda g: (mm_tile(g)[0], 0, 0)),
                  _const_spec((1, dm)),
                  _const_spec((1, d_pool)),
                  _const_spec((1, d)),
                  _const_spec((1, d)),
                  _const_spec(bands.shape),
                  _const_spec(w_pool.shape),
                  _const_spec(w_out.shape),
                  _const_spec(w_ff1.shape),
                  _const_spec(w_ff2.shape)],
        out_specs=pl.BlockSpec((1, tm, d), mm_rows),
        out_shape=jax.ShapeDtypeStruct((batch, seq, d), F32),
        scratch_shapes=[pltpu.VMEM((tm + 2 * POOL_HALO, d_pool), BF16),
                        pltpu.VMEM((2, tm, dm + d_pool), BF16)],
        compiler_params=pltpu.CompilerParams(dimension_semantics=("arbitrary",),
                                             vmem_limit_bytes=VMEM_LIMIT_BYTES),
    )(x, hm, o, u, u, u, mod, g_head.reshape(1, dm), pool_scale.reshape(1, d_pool),
      g_ffn.reshape(1, d), g_final.reshape(1, d), bands, w_pool, w_out, w_ff1, w_ff2)


def _pack_w_in(w_in):
    dm = N_HEADS * HEAD_DIM
    n_gates = N_DIRS * 2 * N_HEADS
    q, k, v, o = (w_in[:, i * dm:(i + 1) * dm] for i in range(4))
    gates = w_in[:, 4 * dm:4 * dm + n_gates]
    pool = w_in[:, 4 * dm + n_gates:]
    gates_t = gates.reshape(-1, N_DIRS, 2, N_HEADS).transpose(3, 1, 2, 0).reshape(n_gates, -1)
    cols = jnp.concatenate([k, o, pool], axis=1).astype(BF16)
    rows = jnp.concatenate([q.T, v.T, gates_t], axis=0).astype(BF16)
    return cols, rows


def _pack_gate_bias(b_igate, b_fgate):
    rows = jnp.stack([b_igate, b_fgate], axis=1)
    rows = rows.transpose(2, 0, 1).reshape(N_HEADS, N_GATE_ROWS)
    return jnp.broadcast_to(rows[:, :, None], (N_HEADS, N_GATE_ROWS, CHUNK)).astype(F32)


def kernel(x, c, w_ada, b_ada, g_mix, w_in, b_igate, b_fgate, g_head, w_pool, pool_scale, w_out,
           g_ffn, w_ff1, w_ff2, g_final):
    depth = w_ada.shape[0]
    batch, seq, d = x.shape
    d_pool = pool_scale.shape[1]
    assert g_head.shape[1] == N_HEADS * HEAD_DIM and w_pool.shape[1] == len(POOL_WINDOWS)
    assert seq % INPROJ_ROWS == 0 and seq % OUTFFN_ROWS == 0 and w_ff1.shape[2] % FF_COLS == 0
    for l in range(depth):
        mod = _adaln(c, w_ada[l], b_ada[l]).reshape(batch, 6, d)
        qt, k, vt, o, u, gates = _inproj(x, mod, g_mix[l], *_pack_w_in(w_in[l]), d_pool)
        hm = _mlstm(qt, k, vt, gates, _pack_gate_bias(b_igate[l], b_fgate[l]))
        x = _outffn(x, hm, o, u, mod, g_head[l], pool_scale[l], g_ffn[l], g_final,
                    w_pool[l].astype(BF16), w_out[l].astype(BF16), w_ff1[l].astype(BF16),
                    w_ff2[l].astype(BF16), final_norm=(l == depth - 1))
    return x
```

```python
import functools

import jax
import jax.numpy as jnp
import numpy as np
from jax import lax
from jax.experimental import pallas as pl
from jax.experimental.pallas import tpu as pltpu

F32 = jnp.float32
BF16 = jnp.bfloat16

LANES = 128
BF16_SUBLANES = 16
VMEM_LIMIT_BYTES = 56 * 1024 * 1024

N_HEADS = 4
HEAD_DIM = 128
CHUNK = 128
POOL_WINDOWS = (2, 4, 8, 16)
POOL_BLOCK = 128
POOL_HALO = 16
N_DIRS = 2
N_GATE_ROWS = 2 * N_DIRS
EPS = 1e-6
LOG2_E = 1.4426950408889634

INPROJ_ROWS = 1024
OUTFFN_ROWS = 512
FF_COLS = 1024
AUG_ROWS = HEAD_DIM + BF16_SUBLANES

ROW_A, ROW_M, ROW_AINTER, ROW_EXPNEG, ROW_WK, ROW_DECAY = range(6)
N_COEF = 6
CHUNKS_PER_STEP = 32


def _const_spec(shape):
    zeros = (0,) * len(shape)
    return pl.BlockSpec(shape, lambda *_: zeros, pipeline_mode=pl.Buffered(1))


def _rms_scale(x):
    return x * lax.rsqrt(jnp.mean(x * x, axis=-1, keepdims=True) + EPS)


def _modulated_norm_bf16(x, gain, shift, scale):
    slope = (gain * (1.0 + scale)).astype(BF16)
    return _rms_scale(x).astype(BF16) * slope + shift.astype(BF16)


def _adaln_kernel(c_ref, w_ref, b_ref, o_ref):
    c = c_ref[...]
    act = c * jax.nn.sigmoid(c)
    o_ref[...] = jnp.dot(act.astype(BF16), w_ref[...].astype(BF16),
                         preferred_element_type=F32) + b_ref[...]


def _adaln(c, w_ada, b_ada):
    batch, d = c.shape
    n = w_ada.shape[1]
    return pl.pallas_call(
        _adaln_kernel,
        grid=(n // d,),
        in_specs=[pl.BlockSpec((batch, d), lambda j: (0, 0)),
                  pl.BlockSpec((d, d), lambda j: (0, j)),
                  pl.BlockSpec((1, d), lambda j: (0, j))],
        out_specs=pl.BlockSpec((batch, d), lambda j: (0, j)),
        out_shape=jax.ShapeDtypeStruct((batch, n), F32),
        compiler_params=pltpu.CompilerParams(dimension_semantics=("arbitrary",),
                                             vmem_limit_bytes=VMEM_LIMIT_BYTES),
    )(c, w_ada, b_ada.reshape(1, n))


def _inproj_kernel(x_ref, mod_ref, g_ref, win_ref, wpool_ref, qt_ref, k_ref, vt_ref, o_ref, u_ref, gates_ref,
                   w_ref, wt_ref):
    dm = N_HEADS * HEAD_DIM
    n_gates = N_HEADS * N_GATE_ROWS

    @pl.when((pl.program_id(0) == 0) & (pl.program_id(1) == 0))
    def _():
        w_ref[:, 0:dm] = win_ref[:, dm:2 * dm].astype(BF16)
        w_ref[:, dm:2 * dm] = win_ref[:, 3 * dm:4 * dm].astype(BF16)
        w_ref[:, 2 * dm:] = wpool_ref[...].astype(BF16)
        for part, col0 in enumerate((0, 2 * dm)):
            for j in range(N_HEADS):
                cols = slice(col0 + j * HEAD_DIM, col0 + (j + 1) * HEAD_DIM)
                wt_ref[part * dm + j * HEAD_DIM:part * dm + (j + 1) * HEAD_DIM, :] = win_ref[:, cols].T.astype(BF16)
        gates_t = win_ref[:, 4 * dm:4 * dm + LANES].T
        rows = []
        for hd in range(N_HEADS):
            for dg in range(N_GATE_ROWS):
                src = dg * N_HEADS + hd
                rows.append(gates_t[src:src + 1, :])
        wt_ref[2 * dm:2 * dm + n_gates, :] = jnp.concatenate(rows, axis=0).astype(BF16)

    x = x_ref[0]
    shift = mod_ref[0, 0:1, :]
    scale = mod_ref[0, 1:2, :]
    hb = _modulated_norm_bf16(x, g_ref[...], shift, scale)

    by_token = jnp.dot(hb, w_ref[...], preferred_element_type=F32)
    for hd in range(N_HEADS):
        k_ref[0, hd] = by_token[:, hd * HEAD_DIM:(hd + 1) * HEAD_DIM].astype(BF16)
    o_ref[0] = by_token[:, dm:2 * dm].astype(BF16)
    u_ref[0] = by_token[:, 2 * dm:].astype(BF16)
    by_feature = lax.dot_general(wt_ref[...], hb, (((1,), (1,)), ((), ())), preferred_element_type=F32)
    for hd in range(N_HEADS):
        rows = slice(hd * HEAD_DIM, (hd + 1) * HEAD_DIM)
        qt_ref[0, hd] = (by_feature[rows, :] * (HEAD_DIM ** -0.5)).astype(BF16)
        vt_ref[0, hd, 0:HEAD_DIM, :] = by_feature[dm + hd * HEAD_DIM:dm + (hd + 1) * HEAD_DIM, :].astype(BF16)
        vt_ref[0, hd, HEAD_DIM:AUG_ROWS, :] = jnp.ones((AUG_ROWS - HEAD_DIM, x.shape[0]), BF16)
    gt = by_feature[2 * dm:, :]
    n_chunks = x.shape[0] // CHUNK
    for hd in range(N_HEADS):
        for r in range(N_GATE_ROWS):
            row = hd * N_GATE_ROWS + r
            for j in range(n_chunks):
                gates_ref[0, hd, r, j:j + 1, :] = gt[row:row + 1, j * CHUNK:(j + 1) * CHUNK]


def _inproj(x, mod, g_mix, w_in, d_pool):
    batch, seq, d = x.shape
    tm = INPROJ_ROWS
    dm = N_HEADS * HEAD_DIM
    n_gates = N_HEADS * N_GATE_ROWS
    assert w_in.shape[1] == 4 * dm + n_gates + d_pool and n_gates == BF16_SUBLANES
    w_pool_in = w_in[:, 4 * dm + n_gates:]
    grid = (batch, seq // tm)
    head_t = jax.ShapeDtypeStruct((batch, N_HEADS, HEAD_DIM, seq), BF16)
    head_n = jax.ShapeDtypeStruct((batch, N_HEADS, seq, HEAD_DIM), BF16)
    head_aug = jax.ShapeDtypeStruct((batch, N_HEADS, AUG_ROWS, seq), BF16)
    out_shape = (head_t, head_n, head_aug,
                 jax.ShapeDtypeStruct((batch, seq, dm), BF16),
                 jax.ShapeDtypeStruct((batch, seq, d_pool), BF16),
                 jax.ShapeDtypeStruct((batch, N_HEADS, N_GATE_ROWS, seq // CHUNK, CHUNK), F32))
    spec_t = pl.BlockSpec((1, N_HEADS, HEAD_DIM, tm), lambda b, i: (b, 0, 0, i))
    spec_n = pl.BlockSpec((1, N_HEADS, tm, HEAD_DIM), lambda b, i: (b, 0, i, 0))
    spec_aug = pl.BlockSpec((1, N_HEADS, AUG_ROWS, tm), lambda b, i: (b, 0, 0, i))
    out_specs = (spec_t, spec_n, spec_aug,
                 pl.BlockSpec((1, tm, dm), lambda b, i: (b, i, 0)),
                 pl.BlockSpec((1, tm, d_pool), lambda b, i: (b, i, 0)),
                 pl.BlockSpec((1, N_HEADS, N_GATE_ROWS, tm // CHUNK, CHUNK), lambda b, i: (b, 0, 0, i, 0)))
    return pl.pallas_call(
        _inproj_kernel,
        grid=grid,
        in_specs=[pl.BlockSpec((1, tm, d), lambda b, i: (b, i, 0)),
                  pl.BlockSpec((1,) + mod.shape[1:], lambda b, i: (b, 0, 0)),
                  _const_spec((1, d)),
                  _const_spec(w_in.shape),
                  _const_spec(w_pool_in.shape)],
        out_specs=out_specs,
        out_shape=out_shape,
        scratch_shapes=[pltpu.VMEM((d, 2 * dm + d_pool), BF16),
                        pltpu.VMEM((2 * dm + n_gates, d), BF16)],
        compiler_params=pltpu.CompilerParams(dimension_semantics=("arbitrary", "arbitrary"),
                                             vmem_limit_bytes=VMEM_LIMIT_BYTES),
    )(x, mod, g_mix.reshape(1, d), w_in, w_pool_in)


def _time_scan(x, op, fill, reverse):
    row = lax.broadcasted_iota(jnp.int32, x.shape, 0)
    step = 1
    while step < CHUNK:
        if reverse:
            shifted = jnp.where(row < CHUNK - step, pltpu.roll(x, CHUNK - step, axis=0), fill)
        else:
            shifted = jnp.where(row >= step, pltpu.roll(x, step, axis=0), fill)
        x = op(x, shifted)
        step *= 2
    return x


def _chunk_carry(decay_log, peak, reverse):
    n = decay_log.shape[0]
    row = lax.broadcasted_iota(jnp.int32, decay_log.shape, 0)

    def earlier(x, dist, fill):
        if reverse:
            return jnp.where(row < n - dist, pltpu.roll(x, n - dist, axis=0), fill)
        return jnp.where(row >= dist, pltpu.roll(x, dist, axis=0), fill)

    p, q = decay_log, decay_log + peak
    dist = 1
    while dist < n:
        q = jnp.maximum(earlier(q, dist, -jnp.inf) + p, q)
        p = earlier(p, dist, 0.0) + p
        dist *= 2
    return earlier(jnp.maximum(p, q), 1, 0.0)


def _mlstm_kernel(qt_ref, k_ref, vt_ref, gates_ref, gbias_ref, *refs, n_casts, n_units):
    cast_in, out_ref, cast_out = refs[:n_casts], refs[n_casts], refs[n_casts + 1:2 * n_casts + 1]
    coef_ref, sprev_ref, st_ref = refs[2 * n_casts + 1:]
    for src, dst in zip(cast_in, cast_out):
        dst[...] = src[...].astype(BF16)

    n_chunks = gates_ref.shape[3]
    assert 2 * N_DIRS * n_chunks == LANES
    step = pl.program_id(0)

    def prologue(slot):
        log_i, log_f = [], []
        for d in range(N_DIRS):
            log_i.append(gates_ref[0, 0, 2 * d] + gbias_ref[0, 2 * d:2 * d + 1, :])
            z = gates_ref[0, 0, 2 * d + 1] + gbias_ref[0, 2 * d + 1:2 * d + 2, :]
            log_f.append(-(jnp.maximum(-z, 0.0) + jnp.log1p(jnp.exp(-jnp.abs(z)))))
        by_time = jnp.concatenate(log_f + log_i, axis=0).T
        lane = lax.broadcasted_iota(jnp.int32, by_time.shape, 1)
        fwd_lane = (lane % (N_DIRS * n_chunks)) < n_chunks
        b_t = jnp.where(fwd_lane, _time_scan(by_time, jnp.add, 0.0, False),
                        _time_scan(by_time, jnp.add, 0.0, True))
        a_t = pltpu.roll(by_time, N_DIRS * n_chunks, axis=1) - b_t
        cm_t = jnp.where(fwd_lane, _time_scan(a_t, jnp.maximum, -jnp.inf, False),
                         _time_scan(a_t, jnp.maximum, -jnp.inf, True))
        b_rows, a_rows, cm_rows = b_t.T, a_t.T, cm_t.T

        for d in range(N_DIRS):
            reverse = d == 1
            last = 0 if reverse else CHUNK - 1
            b = b_rows[d * n_chunks:(d + 1) * n_chunks]
            a = a_rows[d * n_chunks:(d + 1) * n_chunks]
            cm = cm_rows[d * n_chunks:(d + 1) * n_chunks]
            b_last = jnp.broadcast_to(b[:, last:last + 1], b.shape)
            cm_last = jnp.broadcast_to(cm[:, last:last + 1], b.shape)
            m_prev = _chunk_carry(b_last, cm_last, reverse)
            m_row = jnp.maximum(m_prev, cm)
            a_inter = jnp.exp(m_prev - m_row)
            exp_neg = jnp.exp(-(b + m_row))
            m_end = jnp.broadcast_to(m_row[:, last:last + 1], b.shape)
            wk = jnp.exp(a - m_end)
            decay = jnp.broadcast_to(a_inter[:, last:last + 1], b.shape)
            for r, val in ((ROW_A, a * LOG2_E), (ROW_M, m_row * LOG2_E), (ROW_AINTER, a_inter),
                           (ROW_EXPNEG, exp_neg), (ROW_WK, wk), (ROW_DECAY, decay)):
                coef_ref[slot, d, r] = val

    def passes(slot):
        def coef_row(d, r, c):
            return coef_ref[slot, d, r, pl.ds(c, 1), :]

        def chunk_operands(c):
            off = pl.multiple_of(c * CHUNK, CHUNK)
            return off, qt_ref[0, 0, :, pl.ds(off, CHUNK)], k_ref[0, 0, pl.ds(off, CHUNK), :], \
                vt_ref[0, 0, :, pl.ds(off, CHUNK)]

        key_pos = lax.broadcasted_iota(jnp.int32, (CHUNK, CHUNK), 0)
        qry_pos = lax.broadcasted_iota(jnp.int32, (CHUNK, CHUNK), 1)
        visible = (key_pos <= qry_pos, key_pos >= qry_pos)

        def weighted_scores(c):
            _, qt_c, k_c, _ = chunk_operands(c)
            scores = jnp.dot(k_c, qt_c, preferred_element_type=F32)
            for d in range(N_DIRS):
                a_col = jnp.broadcast_to(coef_row(d, ROW_A, c), (CHUNK, CHUNK)).T
                weight = jnp.exp2(jnp.where(visible[d], a_col - coef_row(d, ROW_M, c), -jnp.inf))
                st_ref[d, c] = (scores * weight).astype(BF16)

        def state_step(d, c, state):
            _, _, k_c, vt_c = chunk_operands(c)
            sprev_ref[d, c] = state.astype(BF16)
            vw = vt_c * coef_row(d, ROW_WK, c).astype(BF16)
            return coef_row(d, ROW_DECAY, c) * state + jnp.dot(vw, k_c, preferred_element_type=F32)

        def state_body(i, states):
            s_fwd, s_bwd = states
            for j in range(CHUNKS_PER_STEP):
                c = i * CHUNKS_PER_STEP + j
                weighted_scores(c)
                s_fwd = state_step(0, c, s_fwd)
                s_bwd = state_step(1, n_chunks - 1 - c, s_bwd)
            return s_fwd, s_bwd

        zero_state = jnp.zeros((AUG_ROWS, HEAD_DIM), F32)
        lax.fori_loop(0, n_chunks // CHUNKS_PER_STEP, state_body, (zero_state, zero_state))

        def output_chunk(c):
            off, qt_c, _, vt_c = chunk_operands(c)
            hsum = None
            for d in range(N_DIRS):
                qa = (qt_c.astype(F32) * coef_row(d, ROW_AINTER, c)).astype(BF16)
                lhs = jnp.concatenate([vt_c, sprev_ref[d, c]], axis=1)
                rhs = jnp.concatenate([st_ref[d, c], qa], axis=0)
                numden = jnp.dot(lhs, rhs, preferred_element_type=F32)
                den = numden[HEAD_DIM:HEAD_DIM + 1, :]
                ht = numden[0:HEAD_DIM, :] * (1.0 / jnp.maximum(jnp.abs(den), coef_row(d, ROW_EXPNEG, c)))
                hsum = ht if hsum is None else hsum + ht
            out_ref[0, 0, pl.ds(off, CHUNK), :] = hsum.T.astype(BF16)

        def output_body(i, carry):
            for j in range(CHUNKS_PER_STEP):
                output_chunk(i * CHUNKS_PER_STEP + j)
            return carry

        lax.fori_loop(0, n_chunks // CHUNKS_PER_STEP, output_body, 0)

    @pl.when(step == 0)
    def _():
        prologue(0)

    @pl.when(step == n_units)
    def _():
        passes((n_units - 1) % 2)

    for fill in range(2):
        @pl.when((step > 0) & (step < n_units) & (step % 2 == fill))
        def _():
            passes(1 - fill)
            prologue(fill)


def _mlstm(qt, k, vt, gates, gbias, weights_to_cast):
    batch, heads, hd, seq = qt.shape
    n_chunks = seq // CHUNK
    assert n_chunks % CHUNKS_PER_STEP == 0
    n_units = batch * heads

    def pass_unit(g):
        u = jnp.maximum(g - 1, 0)
        return u // heads, u % heads, 0, 0

    def prologue_unit(g):
        return jnp.minimum(g, n_units - 1)

    spec_t = pl.BlockSpec((1, 1, hd, seq), pass_unit)
    spec_n = pl.BlockSpec((1, 1, seq, hd), pass_unit)
    spec_aug = pl.BlockSpec((1, 1, AUG_ROWS, seq), pass_unit)
    cast_specs = []
    for w in weights_to_cast:
        assert w.shape[0] % (n_units * BF16_SUBLANES) == 0
        cast_specs.append(pl.BlockSpec((w.shape[0] // n_units, w.shape[1]), lambda g: (prologue_unit(g), 0)))
    outs = pl.pallas_call(
        functools.partial(_mlstm_kernel, n_casts=len(weights_to_cast), n_units=n_units),
        grid=(n_units + 1,),
        in_specs=[spec_t, spec_n, spec_aug,
                  pl.BlockSpec((1, 1, N_GATE_ROWS, n_chunks, CHUNK),
                               lambda g: (prologue_unit(g) // heads, prologue_unit(g) % heads, 0, 0, 0)),
                  pl.BlockSpec((1, N_GATE_ROWS, CHUNK), lambda g: (prologue_unit(g) % heads, 0, 0))] + cast_specs,
        out_specs=[spec_n] + cast_specs,
        out_shape=[jax.ShapeDtypeStruct((batch, heads, seq, hd), BF16)]
        + [jax.ShapeDtypeStruct(w.shape, BF16) for w in weights_to_cast],
        scratch_shapes=[pltpu.VMEM((2, N_DIRS, N_COEF, n_chunks, CHUNK), F32),
                        pltpu.VMEM((N_DIRS, n_chunks, AUG_ROWS, hd), BF16),
                        pltpu.VMEM((N_DIRS, n_chunks, CHUNK, CHUNK), BF16)],
        compiler_params=pltpu.CompilerParams(dimension_semantics=("arbitrary",),
                                             vmem_limit_bytes=VMEM_LIMIT_BYTES),
    )(qt, k, vt, gates, gbias, *weights_to_cast)
    return outs[0], outs[1:]


def _outffn_kernel(x_ref, hm_ref, o_ref, u_ref, uprev_ref, unext_ref, mod_ref,
                   ghead_ref, pscale_ref, gffn_ref, gfinal_ref,
                   band_ref, wpool_ref, wout_ref, wff1_ref, wff2_ref, out_ref,
                   ubuf_ref, ycat_ref, *, seq, n_tiles, final_norm):
    tm = x_ref.shape[1]
    dm = N_HEADS * HEAD_DIM
    step = pl.program_id(0)
    tiles_per_seq = seq // tm
    tile = step % tiles_per_seq

    def matmul_stage(drain):
        yb = []
        for gi in range(len(POOL_WINDOWS)):
            cols = slice(gi * HEAD_DIM, (gi + 1) * HEAD_DIM)
            mapped = jnp.dot(ycat_ref[drain, :, dm + gi * HEAD_DIM:dm + (gi + 1) * HEAD_DIM], wpool_ref[gi],
                             preferred_element_type=F32)
            yb.append((mapped * pscale_ref[:, cols]).astype(BF16))
        gate1 = mod_ref[0, 2:3, :]
        yproj = (jnp.dot(ycat_ref[drain, :, 0:dm], wout_ref[0:dm, :], preferred_element_type=F32)
                 + jnp.dot(jnp.concatenate(yb, axis=1), wout_ref[dm:, :], preferred_element_type=F32))
        x1 = x_ref[0] + gate1 * yproj

        shift2 = mod_ref[0, 3:4, :]
        scale2 = mod_ref[0, 4:5, :]
        gate2 = mod_ref[0, 5:6, :]
        h2 = ((_rms_scale(x1) * gffn_ref[...]) * (1.0 + scale2) + shift2).astype(BF16)
        acc = jnp.zeros(x1.shape, F32)
        for j in range(wff1_ref.shape[1] // FF_COLS):
            hid = jnp.dot(h2, wff1_ref[:, j * FF_COLS:(j + 1) * FF_COLS], preferred_element_type=F32)
            hid = jnp.square(jnp.maximum(hid, 0.0)).astype(BF16)
            acc = acc + jnp.dot(hid, wff2_ref[j * FF_COLS:(j + 1) * FF_COLS, :], preferred_element_type=F32)
        x2 = x1 + gate2 * acc
        if final_norm:
            x2 = _rms_scale(x2) * gfinal_ref[...]
        out_ref[0] = x2

    def elementwise_stage(fill):
        for hd in range(N_HEADS):
            cols = slice(hd * HEAD_DIM, (hd + 1) * HEAD_DIM)
            hn = _rms_scale(hm_ref[0, hd].astype(F32)) * ghead_ref[:, cols]
            gate = jax.nn.sigmoid(o_ref[0, :, cols].astype(F32))
            ycat_ref[fill, :, cols] = (gate * hn).astype(BF16)

        ubuf_ref[0:POOL_HALO, :] = jnp.where(tile > 0, uprev_ref[0], jnp.zeros_like(uprev_ref[0]))
        ubuf_ref[POOL_HALO:POOL_HALO + tm, :] = u_ref[0]
        ubuf_ref[POOL_HALO + tm:, :] = jnp.where(tile < tiles_per_seq - 1, unext_ref[0],
                                                 jnp.zeros_like(unext_ref[0]))
        for gi in range(len(POOL_WINDOWS)):
            cols = slice(gi * HEAD_DIM, (gi + 1) * HEAD_DIM)
            for r0 in range(0, tm, POOL_BLOCK):
                ext = ubuf_ref[r0:r0 + POOL_BLOCK + 2 * POOL_HALO, cols]
                mixed = jnp.dot(band_ref[gi], ext, preferred_element_type=F32)
                ycat_ref[fill, r0:r0 + POOL_BLOCK, dm + gi * HEAD_DIM:dm + (gi + 1) * HEAD_DIM] = mixed.astype(BF16)

    def sequence_end_rows(fill):
        def clipped_rows(r0):
            ext = ubuf_ref[r0:r0 + 3 * POOL_HALO, :].astype(F32)
            pos = tile * tm + r0 + lax.broadcasted_iota(jnp.int32, (POOL_HALO, HEAD_DIM), 0)
            for gi, win in enumerate(POOL_WINDOWS):
                half = win // 2
                grp = ext[:, gi * HEAD_DIM:(gi + 1) * HEAD_DIM]
                total = grp[POOL_HALO - half:2 * POOL_HALO - half]
                for j in range(1 - half, half):
                    total = total + grp[POOL_HALO + j:2 * POOL_HALO + j]
                cnt = jnp.minimum(pos + half, seq) - jnp.maximum(pos - half, 0)
                mixed = total / cnt.astype(F32) - grp[POOL_HALO:2 * POOL_HALO]
                ycat_ref[fill, r0:r0 + POOL_HALO, dm + gi * HEAD_DIM:dm + (gi + 1) * HEAD_DIM] = mixed.astype(BF16)

        @pl.when(tile == 0)
        def _():
            clipped_rows(0)

        @pl.when(tile == tiles_per_seq - 1)
        def _():
            clipped_rows(tm - POOL_HALO)

    @pl.when(step == 0)
    def _():
        elementwise_stage(0)
        sequence_end_rows(0)

    @pl.when(step == n_tiles)
    def _():
        matmul_stage((n_tiles - 1) % 2)

    for fill in range(2):
        @pl.when((step > 0) & (step < n_tiles) & (step % 2 == fill))
        def _():
            matmul_stage(1 - fill)
            elementwise_stage(fill)
            sequence_end_rows(fill)


def _pool_bands():
    row = np.arange(POOL_BLOCK)[:, None]
    col = np.arange(POOL_BLOCK + 2 * POOL_HALO)[None, :] - POOL_HALO
    bands = [((col >= row - win // 2) & (col < row + win // 2)) / win - (col == row) for win in POOL_WINDOWS]
    return jnp.asarray(np.stack(bands), dtype=BF16)


def _outffn(x, hm, o, u, mod, g_head, pool_scale, g_ffn, g_final, w_pool, w_out, w_ff1, w_ff2,
            final_norm):
    batch, seq, d = x.shape
    tm = OUTFFN_ROWS
    dm = N_HEADS * HEAD_DIM
    d_pool = u.shape[2]
    halo_per_tile = tm // POOL_HALO
    n_halo_blocks = seq // POOL_HALO
    tiles_per_seq = seq // tm
    n_tiles = batch * tiles_per_seq

    def mm_tile(g):
        t = jnp.maximum(g - 1, 0)
        return t // tiles_per_seq, t % tiles_per_seq

    def ew_tile(g):
        t = jnp.minimum(g, n_tiles - 1)
        return t // tiles_per_seq, t % tiles_per_seq

    def mm_rows(g):
        b, i = mm_tile(g)
        return b, i, 0

    def ew_rows(g):
        b, i = ew_tile(g)
        return b, i, 0

    def ew_heads(g):
        b, i = ew_tile(g)
        return b, 0, i, 0

    def ew_prev(g):
        b, i = ew_tile(g)
        return b, jnp.maximum(i * halo_per_tile - 1, 0), 0

    def ew_next(g):
        b, i = ew_tile(g)
        return b, jnp.minimum((i + 1) * halo_per_tile, n_halo_blocks - 1), 0

    bands = _pool_bands()
    kernel = functools.partial(_outffn_kernel, seq=seq, n_tiles=n_tiles, final_norm=final_norm)
    return pl.pallas_call(
        kernel,
        grid=(n_tiles + 1,),
        in_specs=[pl.BlockSpec((1, tm, d), mm_rows),
                  pl.BlockSpec((1, N_HEADS, tm, HEAD_DIM), ew_heads),
                  pl.BlockSpec((1, tm, dm), ew_rows),
                  pl.BlockSpec((1, tm, d_pool), ew_rows),
                  pl.BlockSpec((1, POOL_HALO, d_pool), ew_prev),
                  pl.BlockSpec((1, POOL_HALO, d_pool), ew_next),
                  pl.BlockSpec((1,) + mod.shape[1:], lambda g: (mm_tile(g)[0], 0, 0)),
                  _const_spec((1, dm)),
                  _const_spec((1, d_pool)),
                  _const_spec((1, d)),
                  _const_spec((1, d)),
                  _const_spec(bands.shape),
                  _const_spec(w_pool.shape),
                  _const_spec(w_out.shape),
                  _const_spec(w_ff1.shape),
                  _const_spec(w_ff2.shape)],
        out_specs=pl.BlockSpec((1, tm, d), mm_rows),
        out_shape=jax.ShapeDtypeStruct((batch, seq, d), F32),
        scratch_shapes=[pltpu.VMEM((tm + 2 * POOL_HALO, d_pool), BF16),
                        pltpu.VMEM((2, tm, dm + d_pool), BF16)],
        compiler_params=pltpu.CompilerParams(dimension_semantics=("arbitrary",),
                                             vmem_limit_bytes=VMEM_LIMIT_BYTES),
    )(x, hm, o, u, u, u, mod, g_head.reshape(1, dm), pool_scale.reshape(1, d_pool),
      g_ffn.reshape(1, d), g_final.reshape(1, d), bands, w_pool, w_out, w_ff1, w_ff2)


def _pack_gate_bias(b_igate, b_fgate):
    rows = jnp.stack([b_igate, b_fgate], axis=1)
    rows = rows.transpose(2, 0, 1).reshape(N_HEADS, N_GATE_ROWS)
    return jnp.broadcast_to(rows[:, :, None], (N_HEADS, N_GATE_ROWS, CHUNK)).astype(F32)


def kernel(x, c, w_ada, b_ada, g_mix, w_in, b_igate, b_fgate, g_head, w_pool, pool_scale, w_out,
           g_ffn, w_ff1, w_ff2, g_final):
    depth = w_ada.shape[0]
    batch, seq, d = x.shape
    d_pool = pool_scale.shape[1]
    assert g_head.shape[1] == N_HEADS * HEAD_DIM and w_pool.shape[1] == len(POOL_WINDOWS)
    assert seq % INPROJ_ROWS == 0 and seq % OUTFFN_ROWS == 0 and w_ff1.shape[2] % FF_COLS == 0
    for l in range(depth):
        mod = _adaln(c, w_ada[l], b_ada[l]).reshape(batch, 6, d)
        qt, k, vt, o, u, gates = _inproj(x, mod, g_mix[l], w_in[l], d_pool)
        gdim = w_pool.shape[2]
        hm, (w_pool_b, w_out_b, w_ff1_b, w_ff2_b) = _mlstm(
            qt, k, vt, gates, _pack_gate_bias(b_igate[l], b_fgate[l]),
            (w_pool[l].reshape(-1, gdim), w_out[l], w_ff1[l], w_ff2[l]))
        x = _outffn(x, hm, o, u, mod, g_head[l], pool_scale[l], g_ffn[l], g_final,
                    w_pool_b.reshape(-1, gdim, gdim), w_out_b, w_ff1_b, w_ff2_b, final_norm=(l == depth - 1))
    return x
```

```python
import functools

import jax
import jax.numpy as jnp
import numpy as np
from jax import lax
from jax.experimental import pallas as pl
from jax.experimental.pallas import tpu as pltpu

F32 = jnp.float32
BF16 = jnp.bfloat16

LANES = 128
BF16_SUBLANES = 16
VMEM_LIMIT_BYTES = 56 * 1024 * 1024

N_HEADS = 4
HEAD_DIM = 128
CHUNK = 128
POOL_WINDOWS = (2, 4, 8, 16)
POOL_BLOCK = 128
POOL_HALO = 16
N_DIRS = 2
N_GATE_ROWS = 2 * N_DIRS
EPS = 1e-6
LOG2_E = 1.4426950408889634

INPROJ_ROWS = 1024
OUTFFN_ROWS = 512
FF_COLS = 1024
AUG_ROWS = HEAD_DIM + BF16_SUBLANES

ROW_A, ROW_M, ROW_AINTER, ROW_EXPNEG, ROW_WK, ROW_DECAY = range(6)
N_COEF = 6
CHUNKS_PER_STEP = 32


def _const_spec(shape):
    zeros = (0,) * len(shape)
    return pl.BlockSpec(shape, lambda *_: zeros, pipeline_mode=pl.Buffered(1))


def _rms_scale(x):
    return x * lax.rsqrt(jnp.mean(x * x, axis=-1, keepdims=True) + EPS)


def _modulated_norm_bf16(x, gain, shift, scale):
    slope = (gain * (1.0 + scale)).astype(BF16)
    return _rms_scale(x).astype(BF16) * slope + shift.astype(BF16)


def _adaln_kernel(c_ref, w_ref, b_ref, o_ref):
    c = c_ref[...]
    act = c * jax.nn.sigmoid(c)
    o_ref[...] = jnp.dot(act.astype(BF16), w_ref[...].astype(BF16),
                         preferred_element_type=F32) + b_ref[...]


def _adaln(c, w_ada, b_ada):
    batch, d = c.shape
    n = w_ada.shape[1]
    return pl.pallas_call(
        _adaln_kernel,
        grid=(n // d,),
        in_specs=[pl.BlockSpec((batch, d), lambda j: (0, 0)),
                  pl.BlockSpec((d, d), lambda j: (0, j)),
                  pl.BlockSpec((1, d), lambda j: (0, j))],
        out_specs=pl.BlockSpec((batch, d), lambda j: (0, j)),
        out_shape=jax.ShapeDtypeStruct((batch, n), F32),
        compiler_params=pltpu.CompilerParams(dimension_semantics=("arbitrary",),
                                             vmem_limit_bytes=VMEM_LIMIT_BYTES),
    )(c, w_ada, b_ada.reshape(1, n))


def _inproj_kernel(x_ref, mod_ref, g_ref, w_ref, wt_ref, *refs, n_casts):
    cast_in, cast_out = refs[:n_casts], refs[n_casts + 6:]
    qt_ref, k_ref, vt_ref, o_ref, u_ref, gates_ref = refs[n_casts:n_casts + 6]
    for src, dst in zip(cast_in, cast_out):
        dst[...] = src[...].astype(BF16)

    dm = N_HEADS * HEAD_DIM
    x = x_ref[0]
    shift = mod_ref[0, 0:1, :]
    scale = mod_ref[0, 1:2, :]
    hb = _modulated_norm_bf16(x, g_ref[...], shift, scale)

    by_token = jnp.dot(hb, w_ref[...], preferred_element_type=F32)
    for hd in range(N_HEADS):
        k_ref[0, hd] = by_token[:, hd * HEAD_DIM:(hd + 1) * HEAD_DIM].astype(BF16)
    o_ref[0] = by_token[:, dm:2 * dm].astype(BF16)
    u_ref[0] = by_token[:, 2 * dm:].astype(BF16)
    by_feature = lax.dot_general(wt_ref[...], hb, (((1,), (1,)), ((), ())), preferred_element_type=F32)
    for hd in range(N_HEADS):
        rows = slice(hd * HEAD_DIM, (hd + 1) * HEAD_DIM)
        qt_ref[0, hd] = (by_feature[rows, :] * (HEAD_DIM ** -0.5)).astype(BF16)
        vt_ref[0, hd, 0:HEAD_DIM, :] = by_feature[dm + hd * HEAD_DIM:dm + (hd + 1) * HEAD_DIM, :].astype(BF16)
        vt_ref[0, hd, HEAD_DIM:AUG_ROWS, :] = jnp.ones((AUG_ROWS - HEAD_DIM, x.shape[0]), BF16)
    gt = by_feature[2 * dm:, :]
    n_chunks = x.shape[0] // CHUNK
    for hd in range(N_HEADS):
        for r in range(N_GATE_ROWS):
            row = hd * N_GATE_ROWS + r
            for j in range(n_chunks):
                gates_ref[0, hd, r, j:j + 1, :] = gt[row:row + 1, j * CHUNK:(j + 1) * CHUNK]


def _inproj(x, mod, g_mix, w_cat, w_gates_t, d_pool, weights_to_cast):
    batch, seq, d = x.shape
    tm = INPROJ_ROWS
    dm = N_HEADS * HEAD_DIM
    grid = (batch, seq // tm)
    tiles_per_seq = seq // tm
    n_steps = batch * tiles_per_seq
    cast_specs = []
    for w in weights_to_cast:
        assert w.shape[0] % (n_steps * BF16_SUBLANES) == 0
        cast_specs.append(pl.BlockSpec((w.shape[0] // n_steps, w.shape[1]),
                                       lambda b, i: (b * tiles_per_seq + i, 0)))
    head_t = jax.ShapeDtypeStruct((batch, N_HEADS, HEAD_DIM, seq), BF16)
    head_n = jax.ShapeDtypeStruct((batch, N_HEADS, seq, HEAD_DIM), BF16)
    head_aug = jax.ShapeDtypeStruct((batch, N_HEADS, AUG_ROWS, seq), BF16)
    out_shape = (head_t, head_n, head_aug,
                 jax.ShapeDtypeStruct((batch, seq, dm), BF16),
                 jax.ShapeDtypeStruct((batch, seq, d_pool), BF16),
                 jax.ShapeDtypeStruct((batch, N_HEADS, N_GATE_ROWS, seq // CHUNK, CHUNK), F32))
    spec_t = pl.BlockSpec((1, N_HEADS, HEAD_DIM, tm), lambda b, i: (b, 0, 0, i))
    spec_n = pl.BlockSpec((1, N_HEADS, tm, HEAD_DIM), lambda b, i: (b, 0, i, 0))
    spec_aug = pl.BlockSpec((1, N_HEADS, AUG_ROWS, tm), lambda b, i: (b, 0, 0, i))
    out_specs = (spec_t, spec_n, spec_aug,
                 pl.BlockSpec((1, tm, dm), lambda b, i: (b, i, 0)),
                 pl.BlockSpec((1, tm, d_pool), lambda b, i: (b, i, 0)),
                 pl.BlockSpec((1, N_HEADS, N_GATE_ROWS, tm // CHUNK, CHUNK), lambda b, i: (b, 0, 0, i, 0)))
    outs = pl.pallas_call(
        functools.partial(_inproj_kernel, n_casts=len(weights_to_cast)),
        grid=grid,
        in_specs=[pl.BlockSpec((1, tm, d), lambda b, i: (b, i, 0)),
                  pl.BlockSpec((1,) + mod.shape[1:], lambda b, i: (b, 0, 0)),
                  _const_spec((1, d)),
                  _const_spec(w_cat.shape),
                  _const_spec(w_gates_t.shape)] + cast_specs,
        out_specs=list(out_specs) + cast_specs,
        out_shape=list(out_shape) + [jax.ShapeDtypeStruct(w.shape, BF16) for w in weights_to_cast],
        compiler_params=pltpu.CompilerParams(dimension_semantics=("arbitrary", "arbitrary"),
                                             vmem_limit_bytes=VMEM_LIMIT_BYTES),
    )(x, mod, g_mix.reshape(1, d), w_cat, w_gates_t, *weights_to_cast)
    return outs[:6], outs[6:]


def _time_scan(x, op, fill, reverse):
    row = lax.broadcasted_iota(jnp.int32, x.shape, 0)
    step = 1
    while step < CHUNK:
        if reverse:
            shifted = jnp.where(row < CHUNK - step, pltpu.roll(x, CHUNK - step, axis=0), fill)
        else:
            shifted = jnp.where(row >= step, pltpu.roll(x, step, axis=0), fill)
        x = op(x, shifted)
        step *= 2
    return x


def _chunk_carry(decay_log, peak, reverse):
    n = decay_log.shape[0]
    row = lax.broadcasted_iota(jnp.int32, decay_log.shape, 0)

    def earlier(x, dist, fill):
        if reverse:
            return jnp.where(row < n - dist, pltpu.roll(x, n - dist, axis=0), fill)
        return jnp.where(row >= dist, pltpu.roll(x, dist, axis=0), fill)

    p, q = decay_log, decay_log + peak
    dist = 1
    while dist < n:
        q = jnp.maximum(earlier(q, dist, -jnp.inf) + p, q)
        p = earlier(p, dist, 0.0) + p
        dist *= 2
    return earlier(jnp.maximum(p, q), 1, 0.0)


def _mlstm_kernel(qt_ref, k_ref, vt_ref, gates_ref, gbias_ref, *refs, n_casts, n_units):
    cast_in, out_ref, cast_out = refs[:n_casts], refs[n_casts], refs[n_casts + 1:2 * n_casts + 1]
    coef_ref, sprev_ref, st_ref = refs[2 * n_casts + 1:]
    for src, dst in zip(cast_in, cast_out):
        dst[...] = src[...].astype(BF16)

    n_chunks = gates_ref.shape[3]
    assert 2 * N_DIRS * n_chunks == LANES
    step = pl.program_id(0)

    def prologue(slot):
        log_i, log_f = [], []
        for d in range(N_DIRS):
            log_i.append(gates_ref[0, 0, 2 * d] + gbias_ref[0, 2 * d:2 * d + 1, :])
            z = gates_ref[0, 0, 2 * d + 1] + gbias_ref[0, 2 * d + 1:2 * d + 2, :]
            log_f.append(-(jnp.maximum(-z, 0.0) + jnp.log1p(jnp.exp(-jnp.abs(z)))))
        by_time = jnp.concatenate(log_f + log_i, axis=0).T
        lane = lax.broadcasted_iota(jnp.int32, by_time.shape, 1)
        fwd_lane = (lane % (N_DIRS * n_chunks)) < n_chunks
        b_t = jnp.where(fwd_lane, _time_scan(by_time, jnp.add, 0.0, False),
                        _time_scan(by_time, jnp.add, 0.0, True))
        a_t = pltpu.roll(by_time, N_DIRS * n_chunks, axis=1) - b_t
        cm_t = jnp.where(fwd_lane, _time_scan(a_t, jnp.maximum, -jnp.inf, False),
                         _time_scan(a_t, jnp.maximum, -jnp.inf, True))
        b_rows, a_rows, cm_rows = b_t.T, a_t.T, cm_t.T

        for d in range(N_DIRS):
            reverse = d == 1
            last = 0 if reverse else CHUNK - 1
            b = b_rows[d * n_chunks:(d + 1) * n_chunks]
            a = a_rows[d * n_chunks:(d + 1) * n_chunks]
            cm = cm_rows[d * n_chunks:(d + 1) * n_chunks]
            b_last = jnp.broadcast_to(b[:, last:last + 1], b.shape)
            cm_last = jnp.broadcast_to(cm[:, last:last + 1], b.shape)
            m_prev = _chunk_carry(b_last, cm_last, reverse)
            m_row = jnp.maximum(m_prev, cm)
            a_inter = jnp.exp(m_prev - m_row)
            exp_neg = jnp.exp(-(b + m_row))
            m_end = jnp.broadcast_to(m_row[:, last:last + 1], b.shape)
            wk = jnp.exp(a - m_end)
            decay = jnp.broadcast_to(a_inter[:, last:last + 1], b.shape)
            for r, val in ((ROW_A, a * LOG2_E), (ROW_M, m_row * LOG2_E), (ROW_AINTER, a_inter),
                           (ROW_EXPNEG, exp_neg), (ROW_WK, wk), (ROW_DECAY, decay)):
                coef_ref[slot, d, r] = val

    def passes(slot):
        def coef_row(d, r, c):
            return coef_ref[slot, d, r, pl.ds(c, 1), :]

        def chunk_operands(c):
            off = pl.multiple_of(c * CHUNK, CHUNK)
            return off, qt_ref[0, 0, :, pl.ds(off, CHUNK)], k_ref[0, 0, pl.ds(off, CHUNK), :], \
                vt_ref[0, 0, :, pl.ds(off, CHUNK)]

        key_pos = lax.broadcasted_iota(jnp.int32, (CHUNK, CHUNK), 0)
        qry_pos = lax.broadcasted_iota(jnp.int32, (CHUNK, CHUNK), 1)
        visible = (key_pos <= qry_pos, key_pos >= qry_pos)

        def weighted_scores(c):
            _, qt_c, k_c, _ = chunk_operands(c)
            scores = jnp.dot(k_c, qt_c, preferred_element_type=F32)
            for d in range(N_DIRS):
                a_col = jnp.broadcast_to(coef_row(d, ROW_A, c), (CHUNK, CHUNK)).T
                weight = jnp.exp2(jnp.where(visible[d], a_col - coef_row(d, ROW_M, c), -jnp.inf))
                st_ref[d, c] = (scores * weight).astype(BF16)

        def state_step(d, c, state):
            _, _, k_c, vt_c = chunk_operands(c)
            sprev_ref[d, c] = state.astype(BF16)
            vw = vt_c * coef_row(d, ROW_WK, c).astype(BF16)
            return coef_row(d, ROW_DECAY, c) * state + jnp.dot(vw, k_c, preferred_element_type=F32)

        def state_body(i, states):
            s_fwd, s_bwd = states
            for j in range(CHUNKS_PER_STEP):
                c = i * CHUNKS_PER_STEP + j
                weighted_scores(c)
                s_fwd = state_step(0, c, s_fwd)
                s_bwd = state_step(1, n_chunks - 1 - c, s_bwd)
            return s_fwd, s_bwd

        zero_state = jnp.zeros((AUG_ROWS, HEAD_DIM), F32)
        lax.fori_loop(0, n_chunks // CHUNKS_PER_STEP, state_body, (zero_state, zero_state))

        def output_chunk(c):
            off, qt_c, _, vt_c = chunk_operands(c)
            hsum = None
            for d in range(N_DIRS):
                qa = (qt_c.astype(F32) * coef_row(d, ROW_AINTER, c)).astype(BF16)
                lhs = jnp.concatenate([vt_c, sprev_ref[d, c]], axis=1)
                rhs = jnp.concatenate([st_ref[d, c], qa], axis=0)
                numden = jnp.dot(lhs, rhs, preferred_element_type=F32)
                den = numden[HEAD_DIM:HEAD_DIM + 1, :]
                ht = numden[0:HEAD_DIM, :] * (1.0 / jnp.maximum(jnp.abs(den), coef_row(d, ROW_EXPNEG, c)))
                hsum = ht if hsum is None else hsum + ht
            out_ref[0, 0, pl.ds(off, CHUNK), :] = hsum.T.astype(BF16)

        def output_body(i, carry):
            for j in range(CHUNKS_PER_STEP):
                output_chunk(i * CHUNKS_PER_STEP + j)
            return carry

        lax.fori_loop(0, n_chunks // CHUNKS_PER_STEP, output_body, 0)

    @pl.when(step == 0)
    def _():
        prologue(0)

    @pl.when(step == n_units)
    def _():
        passes((n_units - 1) % 2)

    for fill in range(2):
        @pl.when((step > 0) & (step < n_units) & (step % 2 == fill))
        def _():
            passes(1 - fill)
            prologue(fill)


def _mlstm(qt, k, vt, gates, gbias, weights_to_cast):
    batch, heads, hd, seq = qt.shape
    n_chunks = seq // CHUNK
    assert n_chunks % CHUNKS_PER_STEP == 0
    n_units = batch * heads

    def pass_unit(g):
        u = jnp.maximum(g - 1, 0)
        return u // heads, u % heads, 0, 0

    def prologue_unit(g):
        return jnp.minimum(g, n_units - 1)

    spec_t = pl.BlockSpec((1, 1, hd, seq), pass_unit)
    spec_n = pl.BlockSpec((1, 1, seq, hd), pass_unit)
    spec_aug = pl.BlockSpec((1, 1, AUG_ROWS, seq), pass_unit)
    cast_specs = []
    for w in weights_to_cast:
        assert w.shape[0] % (n_units * BF16_SUBLANES) == 0
        cast_specs.append(pl.BlockSpec((w.shape[0] // n_units, w.shape[1]), lambda g: (prologue_unit(g), 0)))
    outs = pl.pallas_call(
        functools.partial(_mlstm_kernel, n_casts=len(weights_to_cast), n_units=n_units),
        grid=(n_units + 1,),
        in_specs=[spec_t, spec_n, spec_aug,
                  pl.BlockSpec((1, 1, N_GATE_ROWS, n_chunks, CHUNK),
                               lambda g: (prologue_unit(g) // heads, prologue_unit(g) % heads, 0, 0, 0)),
                  pl.BlockSpec((1, N_GATE_ROWS, CHUNK), lambda g: (prologue_unit(g) % heads, 0, 0))] + cast_specs,
        out_specs=[spec_n] + cast_specs,
        out_shape=[jax.ShapeDtypeStruct((batch, heads, seq, hd), BF16)]
        + [jax.ShapeDtypeStruct(w.shape, BF16) for w in weights_to_cast],
        scratch_shapes=[pltpu.VMEM((2, N_DIRS, N_COEF, n_chunks, CHUNK), F32),
                        pltpu.VMEM((N_DIRS, n_chunks, AUG_ROWS, hd), BF16),
                        pltpu.VMEM((N_DIRS, n_chunks, CHUNK, CHUNK), BF16)],
        compiler_params=pltpu.CompilerParams(dimension_semantics=("arbitrary",),
                                             vmem_limit_bytes=VMEM_LIMIT_BYTES),
    )(qt, k, vt, gates, gbias, *weights_to_cast)
    return outs[0], outs[1:]


def _outffn_kernel(x_ref, hm_ref, o_ref, u_ref, uprev_ref, unext_ref, mod_ref,
                   ghead_ref, pscale_ref, gffn_ref, gfinal_ref,
                   band_ref, wpool_ref, wout_ref, wff1_ref, wff2_ref, out_ref,
                   ubuf_ref, ycat_ref, *, seq, n_tiles, final_norm):
    tm = x_ref.shape[1]
    dm = N_HEADS * HEAD_DIM
    step = pl.program_id(0)
    tiles_per_seq = seq // tm
    tile = step % tiles_per_seq

    def matmul_stage(drain):
        yb = []
        for gi in range(len(POOL_WINDOWS)):
            cols = slice(gi * HEAD_DIM, (gi + 1) * HEAD_DIM)
            mapped = jnp.dot(ycat_ref[drain, :, dm + gi * HEAD_DIM:dm + (gi + 1) * HEAD_DIM], wpool_ref[gi],
                             preferred_element_type=F32)
            yb.append((mapped * pscale_ref[:, cols]).astype(BF16))
        gate1 = mod_ref[0, 2:3, :]
        yproj = (jnp.dot(ycat_ref[drain, :, 0:dm], wout_ref[0:dm, :], preferred_element_type=F32)
                 + jnp.dot(jnp.concatenate(yb, axis=1), wout_ref[dm:, :], preferred_element_type=F32))
        x1 = x_ref[0] + gate1 * yproj

        shift2 = mod_ref[0, 3:4, :]
        scale2 = mod_ref[0, 4:5, :]
        gate2 = mod_ref[0, 5:6, :]
        h2 = ((_rms_scale(x1) * gffn_ref[...]) * (1.0 + scale2) + shift2).astype(BF16)
        acc = jnp.zeros(x1.shape, F32)
        for j in range(wff1_ref.shape[1] // FF_COLS):
            hid = jnp.dot(h2, wff1_ref[:, j * FF_COLS:(j + 1) * FF_COLS], preferred_element_type=F32)
            hid = jnp.square(jnp.maximum(hid, 0.0)).astype(BF16)
            acc = acc + jnp.dot(hid, wff2_ref[j * FF_COLS:(j + 1) * FF_COLS, :], preferred_element_type=F32)
        x2 = x1 + gate2 * acc
        if final_norm:
            x2 = _rms_scale(x2) * gfinal_ref[...]
        out_ref[0] = x2

    def elementwise_stage(fill):
        for hd in range(N_HEADS):
            cols = slice(hd * HEAD_DIM, (hd + 1) * HEAD_DIM)
            hn = _rms_scale(hm_ref[0, hd].astype(F32)) * ghead_ref[:, cols]
            gate = jax.nn.sigmoid(o_ref[0, :, cols].astype(F32))
            ycat_ref[fill, :, cols] = (gate * hn).astype(BF16)

        ubuf_ref[0:POOL_HALO, :] = jnp.where(tile > 0, uprev_ref[0], jnp.zeros_like(uprev_ref[0]))
        ubuf_ref[POOL_HALO:POOL_HALO + tm, :] = u_ref[0]
        ubuf_ref[POOL_HALO + tm:, :] = jnp.where(tile < tiles_per_seq - 1, unext_ref[0],
                                                 jnp.zeros_like(unext_ref[0]))
        for gi in range(len(POOL_WINDOWS)):
            cols = slice(gi * HEAD_DIM, (gi + 1) * HEAD_DIM)
            for r0 in range(0, tm, POOL_BLOCK):
                ext = ubuf_ref[r0:r0 + POOL_BLOCK + 2 * POOL_HALO, cols]
                mixed = jnp.dot(band_ref[gi], ext, preferred_element_type=F32)
                ycat_ref[fill, r0:r0 + POOL_BLOCK, dm + gi * HEAD_DIM:dm + (gi + 1) * HEAD_DIM] = mixed.astype(BF16)

    def sequence_end_rows(fill):
        def clipped_rows(r0):
            ext = ubuf_ref[r0:r0 + 3 * POOL_HALO, :].astype(F32)
            pos = tile * tm + r0 + lax.broadcasted_iota(jnp.int32, (POOL_HALO, HEAD_DIM), 0)
            for gi, win in enumerate(POOL_WINDOWS):
                half = win // 2
                grp = ext[:, gi * HEAD_DIM:(gi + 1) * HEAD_DIM]
                total = grp[POOL_HALO - half:2 * POOL_HALO - half]
                for j in range(1 - half, half):
                    total = total + grp[POOL_HALO + j:2 * POOL_HALO + j]
                cnt = jnp.minimum(pos + half, seq) - jnp.maximum(pos - half, 0)
                mixed = total / cnt.astype(F32) - grp[POOL_HALO:2 * POOL_HALO]
                ycat_ref[fill, r0:r0 + POOL_HALO, dm + gi * HEAD_DIM:dm + (gi + 1) * HEAD_DIM] = mixed.astype(BF16)

        @pl.when(tile == 0)
        def _():
            clipped_rows(0)

        @pl.when(tile == tiles_per_seq - 1)
        def _():
            clipped_rows(tm - POOL_HALO)

    @pl.when(step == 0)
    def _():
        elementwise_stage(0)
        sequence_end_rows(0)

    @pl.when(step == n_tiles)
    def _():
        matmul_stage((n_tiles - 1) % 2)

    for fill in range(2):
        @pl.when((step > 0) & (step < n_tiles) & (step % 2 == fill))
        def _():
            matmul_stage(1 - fill)
            elementwise_stage(fill)
            sequence_end_rows(fill)


def _pool_bands():
    row = np.arange(POOL_BLOCK)[:, None]
    col = np.arange(POOL_BLOCK + 2 * POOL_HALO)[None, :] - POOL_HALO
    bands = [((col >= row - win // 2) & (col < row + win // 2)) / win - (col == row) for win in POOL_WINDOWS]
    return jnp.asarray(np.stack(bands), dtype=BF16)


def _outffn(x, hm, o, u, mod, g_head, pool_scale, g_ffn, g_final, w_pool, w_out, w_ff1, w_ff2,
            final_norm):
    batch, seq, d = x.shape
    tm = OUTFFN_ROWS
    dm = N_HEADS * HEAD_DIM
    d_pool = u.shape[2]
    halo_per_tile = tm // POOL_HALO
    n_halo_blocks = seq // POOL_HALO
    tiles_per_seq = seq // tm
    n_tiles = batch * tiles_per_seq

    def mm_tile(g):
        t = jnp.maximum(g - 1, 0)
        return t // tiles_per_seq, t % tiles_per_seq

    def ew_tile(g):
        t = jnp.minimum(g, n_tiles - 1)
        return t // tiles_per_seq, t % tiles_per_seq

    def mm_rows(g):
        b, i = mm_tile(g)
        return b, i, 0

    def ew_rows(g):
        b, i = ew_tile(g)
        return b, i, 0

    def ew_heads(g):
        b, i = ew_tile(g)
        return b, 0, i, 0

    def ew_prev(g):
        b, i = ew_tile(g)
        return b, jnp.maximum(i * halo_per_tile - 1, 0), 0

    def ew_next(g):
        b, i = ew_tile(g)
        return b, jnp.minimum((i + 1) * halo_per_tile, n_halo_blocks - 1), 0

    bands = _pool_bands()
    kernel = functools.partial(_outffn_kernel, seq=seq, n_tiles=n_tiles, final_norm=final_norm)
    return pl.pallas_call(
        kernel,
        grid=(n_tiles + 1,),
        in_specs=[pl.BlockSpec((1, tm, d), mm_rows),
                  pl.BlockSpec((1, N_HEADS, tm, HEAD_DIM), ew_heads),
                  pl.BlockSpec((1, tm, dm), ew_rows),
                  pl.BlockSpec((1, tm, d_pool), ew_rows),
                  pl.BlockSpec((1, POOL_HALO, d_pool), ew_prev),
                  pl.BlockSpec((1, POOL_HALO, d_pool), ew_next),
                  pl.BlockSpec((1,) + mod.shape[1:], lambda g: (mm_tile(g)[0], 0, 0)),
                  _const_spec((1, dm)),
                  _const_spec((1, d_pool)),
                  _const_spec((1, d)),
                  _const_spec((1, d)),
                  _const_spec(bands.shape),
                  _const_spec(w_pool.shape),
                  _const_spec(w_out.shape),
                  _const_spec(w_ff1.shape),
                  _const_spec(w_ff2.shape)],
        out_specs=pl.BlockSpec((1, tm, d), mm_rows),
        out_shape=jax.ShapeDtypeStruct((batch, seq, d), F32),
        scratch_shapes=[pltpu.VMEM((tm + 2 * POOL_HALO, d_pool), BF16),
                        pltpu.VMEM((2, tm, dm + d_pool), BF16)],
        compiler_params=pltpu.CompilerParams(dimension_semantics=("arbitrary",),
                                             vmem_limit_bytes=VMEM_LIMIT_BYTES),
    )(x, hm, o, u, u, u, mod, g_head.reshape(1, dm), pool_scale.reshape(1, d_pool),
      g_ffn.reshape(1, d), g_final.reshape(1, d), bands, w_pool, w_out, w_ff1, w_ff2)


def _pack_w_in(w_in):
    dm = N_HEADS * HEAD_DIM
    n_gates = N_DIRS * 2 * N_HEADS
    q, k, v, o = (w_in[:, i * dm:(i + 1) * dm] for i in range(4))
    gates = w_in[:, 4 * dm:4 * dm + n_gates]
    pool = w_in[:, 4 * dm + n_gates:]
    gates_t = gates.reshape(-1, N_DIRS, 2, N_HEADS).transpose(3, 1, 2, 0).reshape(n_gates, -1)
    cols = jnp.concatenate([k, o, pool], axis=1).astype(BF16)
    rows = jnp.concatenate([q.T, v.T, gates_t], axis=0).astype(BF16)
    return cols, rows


def _pack_gate_bias(b_igate, b_fgate):
    rows = jnp.stack([b_igate, b_fgate], axis=1)
    rows = rows.transpose(2, 0, 1).reshape(N_HEADS, N_GATE_ROWS)
    return jnp.broadcast_to(rows[:, :, None], (N_HEADS, N_GATE_ROWS, CHUNK)).astype(F32)


def kernel(x, c, w_ada, b_ada, g_mix, w_in, b_igate, b_fgate, g_head, w_pool, pool_scale, w_out,
           g_ffn, w_ff1, w_ff2, g_final):
    depth = w_ada.shape[0]
    batch, seq, d = x.shape
    d_pool = pool_scale.shape[1]
    assert g_head.shape[1] == N_HEADS * HEAD_DIM and w_pool.shape[1] == len(POOL_WINDOWS)
    assert seq % INPROJ_ROWS == 0 and seq % OUTFFN_ROWS == 0 and w_ff1.shape[2] % FF_COLS == 0
    for l in range(depth):
        mod = _adaln(c, w_ada[l], b_ada[l]).reshape(batch, 6, d)
        gdim = w_pool.shape[2]
        (qt, k, vt, o, u, gates), (w_out_b, w_ff1_b) = _inproj(
            x, mod, g_mix[l], *_pack_w_in(w_in[l]), d_pool, (w_out[l], w_ff1[l]))
        hm, (w_pool_b, w_ff2_b) = _mlstm(
            qt, k, vt, gates, _pack_gate_bias(b_igate[l], b_fgate[l]),
            (w_pool[l].reshape(-1, gdim), w_ff2[l]))
        x = _outffn(x, hm, o, u, mod, g_head[l], pool_scale[l], g_ffn[l], g_final,
                    w_pool_b.reshape(-1, gdim, gdim), w_out_b, w_ff1_b, w_ff2_b, final_norm=(l == depth - 1))
    return x
```

```python
import functools

import jax
import jax.numpy as jnp
import numpy as np
from jax import lax
from jax.experimental import pallas as pl
from jax.experimental.pallas import tpu as pltpu

F32 = jnp.float32
BF16 = jnp.bfloat16

LANES = 128
BF16_SUBLANES = 16
VMEM_LIMIT_BYTES = 56 * 1024 * 1024

N_HEADS = 4
HEAD_DIM = 128
CHUNK = 128
POOL_WINDOWS = (2, 4, 8, 16)
POOL_BLOCK = 128
POOL_HALO = 16
N_DIRS = 2
N_GATE_ROWS = 2 * N_DIRS
EPS = 1e-6
LOG2_E = 1.4426950408889634

INPROJ_ROWS = 1024
OUTFFN_ROWS = 512
FF_COLS = 1024
AUG_ROWS = HEAD_DIM + BF16_SUBLANES

ROW_A, ROW_M, ROW_AINTER, ROW_EXPNEG, ROW_WK, ROW_DECAY = range(6)
N_COEF = 6
CHUNKS_PER_STEP = 32


def _const_spec(shape):
    zeros = (0,) * len(shape)
    return pl.BlockSpec(shape, lambda *_: zeros, pipeline_mode=pl.Buffered(1))


def _rms_scale(x):
    return x * lax.rsqrt(jnp.mean(x * x, axis=-1, keepdims=True) + EPS)


def _modulated_norm_bf16(x, gain, shift, scale):
    slope = (gain * (1.0 + scale)).astype(BF16)
    return _rms_scale(x).astype(BF16) * slope + shift.astype(BF16)


def _adaln_kernel(c_ref, w_ref, b_ref, o_ref):
    c = c_ref[...]
    act = c * jax.nn.sigmoid(c)
    o_ref[...] = jnp.dot(act.astype(BF16), w_ref[...].astype(BF16),
                         preferred_element_type=F32) + b_ref[...]


def _adaln(c, w_ada, b_ada):
    batch, d = c.shape
    n = w_ada.shape[1]
    return pl.pallas_call(
        _adaln_kernel,
        grid=(n // d,),
        in_specs=[pl.BlockSpec((batch, d), lambda j: (0, 0)),
                  pl.BlockSpec((d, d), lambda j: (0, j)),
                  pl.BlockSpec((1, d), lambda j: (0, j))],
        out_specs=pl.BlockSpec((batch, d), lambda j: (0, j)),
        out_shape=jax.ShapeDtypeStruct((batch, n), F32),
        compiler_params=pltpu.CompilerParams(dimension_semantics=("arbitrary",),
                                             vmem_limit_bytes=VMEM_LIMIT_BYTES),
    )(c, w_ada, b_ada.reshape(1, n))


def _inproj_kernel(x_ref, mod_ref, g_ref, w_ref, wt_ref, *refs, n_casts):
    cast_in, cast_out = refs[:n_casts], refs[n_casts + 6:]
    qt_ref, k_ref, vt_ref, o_ref, u_ref, gates_ref = refs[n_casts:n_casts + 6]
    for src, dst in zip(cast_in, cast_out):
        dst[...] = src[...].astype(BF16)

    dm = N_HEADS * HEAD_DIM
    x = x_ref[0]
    shift = mod_ref[0, 0:1, :]
    scale = mod_ref[0, 1:2, :]
    hb = _modulated_norm_bf16(x, g_ref[...], shift, scale)

    by_token = jnp.dot(hb, w_ref[...], preferred_element_type=F32)
    for hd in range(N_HEADS):
        k_ref[0, hd] = by_token[:, hd * HEAD_DIM:(hd + 1) * HEAD_DIM].astype(BF16)
    o_ref[0] = by_token[:, dm:2 * dm].astype(BF16)
    u_ref[0] = by_token[:, 2 * dm:].astype(BF16)
    by_feature = lax.dot_general(wt_ref[...], hb, (((1,), (1,)), ((), ())), preferred_element_type=F32)
    for hd in range(N_HEADS):
        rows = slice(hd * HEAD_DIM, (hd + 1) * HEAD_DIM)
        qt_ref[0, hd] = (by_feature[rows, :] * (HEAD_DIM ** -0.5)).astype(BF16)
        vt_ref[0, hd, 0:HEAD_DIM, :] = by_feature[dm + hd * HEAD_DIM:dm + (hd + 1) * HEAD_DIM, :].astype(BF16)
        vt_ref[0, hd, HEAD_DIM:AUG_ROWS, :] = jnp.ones((AUG_ROWS - HEAD_DIM, x.shape[0]), BF16)
    gt = by_feature[2 * dm:, :]
    n_chunks = x.shape[0] // CHUNK
    for hd in range(N_HEADS):
        for r in range(N_GATE_ROWS):
            row = hd * N_GATE_ROWS + r
            for j in range(n_chunks):
                gates_ref[0, hd, r, j:j + 1, :] = gt[row:row + 1, j * CHUNK:(j + 1) * CHUNK]


def _inproj(x, mod, g_mix, w_cat, w_gates_t, d_pool, weights_to_cast):
    batch, seq, d = x.shape
    tm = INPROJ_ROWS
    dm = N_HEADS * HEAD_DIM
    grid = (batch, seq // tm)
    tiles_per_seq = seq // tm
    n_steps = batch * tiles_per_seq
    cast_specs = []
    for w in weights_to_cast:
        assert w.shape[0] % (n_steps * BF16_SUBLANES) == 0
        cast_specs.append(pl.BlockSpec((w.shape[0] // n_steps, w.shape[1]),
                                       lambda b, i: (b * tiles_per_seq + i, 0)))
    head_t = jax.ShapeDtypeStruct((batch, N_HEADS, HEAD_DIM, seq), BF16)
    head_n = jax.ShapeDtypeStruct((batch, N_HEADS, seq, HEAD_DIM), BF16)
    head_aug = jax.ShapeDtypeStruct((batch, N_HEADS, AUG_ROWS, seq), BF16)
    out_shape = (head_t, head_n, head_aug,
                 jax.ShapeDtypeStruct((batch, seq, dm), BF16),
                 jax.ShapeDtypeStruct((batch, seq, d_pool), BF16),
                 jax.ShapeDtypeStruct((batch, N_HEADS, N_GATE_ROWS, seq // CHUNK, CHUNK), F32))
    spec_t = pl.BlockSpec((1, N_HEADS, HEAD_DIM, tm), lambda b, i: (b, 0, 0, i))
    spec_n = pl.BlockSpec((1, N_HEADS, tm, HEAD_DIM), lambda b, i: (b, 0, i, 0))
    spec_aug = pl.BlockSpec((1, N_HEADS, AUG_ROWS, tm), lambda b, i: (b, 0, 0, i))
    out_specs = (spec_t, spec_n, spec_aug,
                 pl.BlockSpec((1, tm, dm), lambda b, i: (b, i, 0)),
                 pl.BlockSpec((1, tm, d_pool), lambda b, i: (b, i, 0)),
                 pl.BlockSpec((1, N_HEADS, N_GATE_ROWS, tm // CHUNK, CHUNK), lambda b, i: (b, 0, 0, i, 0)))
    outs = pl.pallas_call(
        functools.partial(_inproj_kernel, n_casts=len(weights_to_cast)),
        grid=grid,
        in_specs=[pl.BlockSpec((1, tm, d), lambda b, i: (b, i, 0)),
                  pl.BlockSpec((1,) + mod.shape[1:], lambda b, i: (b, 0, 0)),
                  _const_spec((1, d)),
                  _const_spec(w_cat.shape),
                  _const_spec(w_gates_t.shape)] + cast_specs,
        out_specs=list(out_specs) + cast_specs,
        out_shape=list(out_shape) + [jax.ShapeDtypeStruct(w.shape, BF16) for w in weights_to_cast],
        compiler_params=pltpu.CompilerParams(dimension_semantics=("arbitrary", "arbitrary"),
                                             vmem_limit_bytes=VMEM_LIMIT_BYTES),
    )(x, mod, g_mix.reshape(1, d), w_cat, w_gates_t, *weights_to_cast)
    return outs[:6], outs[6:]


def _time_scan(x, op, fill, reverse):
    row = lax.broadcasted_iota(jnp.int32, x.shape, 0)
    step = 1
    while step < CHUNK:
        if reverse:
            shifted = jnp.where(row < CHUNK - step, pltpu.roll(x, CHUNK - step, axis=0), fill)
        else:
            shifted = jnp.where(row >= step, pltpu.roll(x, step, axis=0), fill)
        x = op(x, shifted)
        step *= 2
    return x


def _chunk_carry(decay_log, peak, reverse):
    n = decay_log.shape[0]
    row = lax.broadcasted_iota(jnp.int32, decay_log.shape, 0)

    def earlier(x, dist, fill):
        if reverse:
            return jnp.where(row < n - dist, pltpu.roll(x, n - dist, axis=0), fill)
        return jnp.where(row >= dist, pltpu.roll(x, dist, axis=0), fill)

    p, q = decay_log, decay_log + peak
    dist = 1
    while dist < n:
        q = jnp.maximum(earlier(q, dist, -jnp.inf) + p, q)
        p = earlier(p, dist, 0.0) + p
        dist *= 2
    return earlier(jnp.maximum(p, q), 1, 0.0)


def _mlstm_kernel(qt_ref, k_ref, vt_ref, gates_ref, gbias_ref, *refs, n_casts, n_units):
    cast_in, out_ref, cast_out = refs[:n_casts], refs[n_casts], refs[n_casts + 1:2 * n_casts + 1]
    coef_ref, sprev_ref, st_ref = refs[2 * n_casts + 1:]
    for src, dst in zip(cast_in, cast_out):
        dst[...] = src[...].astype(BF16)

    n_chunks = gates_ref.shape[3]
    assert 2 * N_DIRS * n_chunks == LANES
    step = pl.program_id(0)

    def prologue(slot):
        log_i, log_f = [], []
        for d in range(N_DIRS):
            log_i.append(gates_ref[0, 0, 2 * d] + gbias_ref[0, 2 * d:2 * d + 1, :])
            z = gates_ref[0, 0, 2 * d + 1] + gbias_ref[0, 2 * d + 1:2 * d + 2, :]
            log_f.append(-(jnp.maximum(-z, 0.0) + jnp.log1p(jnp.exp(-jnp.abs(z)))))
        by_time = jnp.concatenate(log_f + log_i, axis=0).T
        lane = lax.broadcasted_iota(jnp.int32, by_time.shape, 1)
        fwd_lane = (lane % (N_DIRS * n_chunks)) < n_chunks
        b_t = jnp.where(fwd_lane, _time_scan(by_time, jnp.add, 0.0, False),
                        _time_scan(by_time, jnp.add, 0.0, True))
        a_t = pltpu.roll(by_time, N_DIRS * n_chunks, axis=1) - b_t
        cm_t = jnp.where(fwd_lane, _time_scan(a_t, jnp.maximum, -jnp.inf, False),
                         _time_scan(a_t, jnp.maximum, -jnp.inf, True))
        b_rows, a_rows, cm_rows = b_t.T, a_t.T, cm_t.T

        for d in range(N_DIRS):
            reverse = d == 1
            last = 0 if reverse else CHUNK - 1
            b = b_rows[d * n_chunks:(d + 1) * n_chunks]
            a = a_rows[d * n_chunks:(d + 1) * n_chunks]
            cm = cm_rows[d * n_chunks:(d + 1) * n_chunks]
            b_last = jnp.broadcast_to(b[:, last:last + 1], b.shape)
            cm_last = jnp.broadcast_to(cm[:, last:last + 1], b.shape)
            m_prev = _chunk_carry(b_last, cm_last, reverse)
            m_row = jnp.maximum(m_prev, cm)
            a_inter = jnp.exp(m_prev - m_row)
            exp_neg = jnp.exp(-(b + m_row))
            m_end = jnp.broadcast_to(m_row[:, last:last + 1], b.shape)
            wk = jnp.exp(a - m_end)
            decay = jnp.broadcast_to(a_inter[:, last:last + 1], b.shape)
            for r, val in ((ROW_A, a * LOG2_E), (ROW_M, m_row * LOG2_E), (ROW_AINTER, a_inter),
                           (ROW_EXPNEG, exp_neg), (ROW_WK, wk), (ROW_DECAY, decay)):
                coef_ref[slot, d, r] = val

    def passes(slot):
        def coef_row(d, r, c):
            return coef_ref[slot, d, r, pl.ds(c, 1), :]

        def chunk_operands(c):
            off = pl.multiple_of(c * CHUNK, CHUNK)
            return off, qt_ref[0, 0, :, pl.ds(off, CHUNK)], k_ref[0, 0, pl.ds(off, CHUNK), :], \
                vt_ref[0, 0, :, pl.ds(off, CHUNK)]

        key_pos = lax.broadcasted_iota(jnp.int32, (CHUNK, CHUNK), 0)
        qry_pos = lax.broadcasted_iota(jnp.int32, (CHUNK, CHUNK), 1)
        visible = (key_pos <= qry_pos, key_pos >= qry_pos)

        def weighted_scores(c):
            _, qt_c, k_c, _ = chunk_operands(c)
            scores = jnp.dot(k_c, qt_c, preferred_element_type=F32)
            for d in range(N_DIRS):
                a_col = jnp.broadcast_to(coef_row(d, ROW_A, c), (CHUNK, CHUNK)).T
                weight = jnp.exp2(jnp.where(visible[d], a_col - coef_row(d, ROW_M, c), -jnp.inf))
                st_ref[d, c] = (scores * weight).astype(BF16)

        def state_step(d, c, state):
            _, _, k_c, vt_c = chunk_operands(c)
            sprev_ref[d, c] = state.astype(BF16)
            vw = vt_c * coef_row(d, ROW_WK, c).astype(BF16)
            return coef_row(d, ROW_DECAY, c) * state + jnp.dot(vw, k_c, preferred_element_type=F32)

        def state_body(i, states):
            s_fwd, s_bwd = states
            for j in range(CHUNKS_PER_STEP):
                c = i * CHUNKS_PER_STEP + j
                weighted_scores(c)
                s_fwd = state_step(0, c, s_fwd)
                s_bwd = state_step(1, n_chunks - 1 - c, s_bwd)
            return s_fwd, s_bwd

        zero_state = jnp.zeros((AUG_ROWS, HEAD_DIM), F32)
        lax.fori_loop(0, n_chunks // CHUNKS_PER_STEP, state_body, (zero_state, zero_state))

        def output_chunk(c):
            off, qt_c, _, vt_c = chunk_operands(c)
            hsum = None
            for d in range(N_DIRS):
                qa = qt_c * coef_row(d, ROW_AINTER, c).astype(BF16)
                lhs = jnp.concatenate([vt_c, sprev_ref[d, c]], axis=1)
                rhs = jnp.concatenate([st_ref[d, c], qa], axis=0)
                numden = jnp.dot(lhs, rhs, preferred_element_type=F32)
                den = numden[HEAD_DIM:HEAD_DIM + 1, :]
                ht = numden[0:HEAD_DIM, :] * (1.0 / jnp.maximum(jnp.abs(den), coef_row(d, ROW_EXPNEG, c)))
                hsum = ht if hsum is None else hsum + ht
            out_ref[0, 0, pl.ds(off, CHUNK), :] = hsum.T.astype(BF16)

        def output_body(i, carry):
            for j in range(CHUNKS_PER_STEP):
                output_chunk(i * CHUNKS_PER_STEP + j)
            return carry

        lax.fori_loop(0, n_chunks // CHUNKS_PER_STEP, output_body, 0)

    @pl.when(step == 0)
    def _():
        prologue(0)

    @pl.when(step == n_units)
    def _():
        passes((n_units - 1) % 2)

    for fill in range(2):
        @pl.when((step > 0) & (step < n_units) & (step % 2 == fill))
        def _():
            passes(1 - fill)
            prologue(fill)


def _mlstm(qt, k, vt, gates, gbias, weights_to_cast):
    batch, heads, hd, seq = qt.shape
    n_chunks = seq // CHUNK
    assert n_chunks % CHUNKS_PER_STEP == 0
    n_units = batch * heads

    def pass_unit(g):
        u = jnp.maximum(g - 1, 0)
        return u // heads, u % heads, 0, 0

    def prologue_unit(g):
        return jnp.minimum(g, n_units - 1)

    spec_t = pl.BlockSpec((1, 1, hd, seq), pass_unit)
    spec_n = pl.BlockSpec((1, 1, seq, hd), pass_unit)
    spec_aug = pl.BlockSpec((1, 1, AUG_ROWS, seq), pass_unit)
    cast_specs = []
    for w in weights_to_cast:
        assert w.shape[0] % (n_units * BF16_SUBLANES) == 0
        cast_specs.append(pl.BlockSpec((w.shape[0] // n_units, w.shape[1]), lambda g: (prologue_unit(g), 0)))
    outs = pl.pallas_call(
        functools.partial(_mlstm_kernel, n_casts=len(weights_to_cast), n_units=n_units),
        grid=(n_units + 1,),
        in_specs=[spec_t, spec_n, spec_aug,
                  pl.BlockSpec((1, 1, N_GATE_ROWS, n_chunks, CHUNK),
                               lambda g: (prologue_unit(g) // heads, prologue_unit(g) % heads, 0, 0, 0)),
                  pl.BlockSpec((1, N_GATE_ROWS, CHUNK), lambda g: (prologue_unit(g) % heads, 0, 0))] + cast_specs,
        out_specs=[spec_n] + cast_specs,
        out_shape=[jax.ShapeDtypeStruct((batch, heads, seq, hd), BF16)]
        + [jax.ShapeDtypeStruct(w.shape, BF16) for w in weights_to_cast],
        scratch_shapes=[pltpu.VMEM((2, N_DIRS, N_COEF, n_chunks, CHUNK), F32),
                        pltpu.VMEM((N_DIRS, n_chunks, AUG_ROWS, hd), BF16),
                        pltpu.VMEM((N_DIRS, n_chunks, CHUNK, CHUNK), BF16)],
        compiler_params=pltpu.CompilerParams(dimension_semantics=("arbitrary",),
                                             vmem_limit_bytes=VMEM_LIMIT_BYTES),
    )(qt, k, vt, gates, gbias, *weights_to_cast)
    return outs[0], outs[1:]


def _outffn_kernel(x_ref, hm_ref, o_ref, u_ref, uprev_ref, unext_ref, mod_ref,
                   ghead_ref, pscale_ref, gffn_ref, gfinal_ref,
                   band_ref, wpool_ref, wout_ref, wff1_ref, wff2_ref, out_ref,
                   ubuf_ref, ycat_ref, *, seq, n_tiles, final_norm):
    tm = x_ref.shape[1]
    dm = N_HEADS * HEAD_DIM
    step = pl.program_id(0)
    tiles_per_seq = seq // tm
    tile = step % tiles_per_seq

    def matmul_stage(drain):
        yb = []
        for gi in range(len(POOL_WINDOWS)):
            cols = slice(gi * HEAD_DIM, (gi + 1) * HEAD_DIM)
            mapped = jnp.dot(ycat_ref[drain, :, dm + gi * HEAD_DIM:dm + (gi + 1) * HEAD_DIM], wpool_ref[gi],
                             preferred_element_type=F32)
            yb.append((mapped * pscale_ref[:, cols]).astype(BF16))
        gate1 = mod_ref[0, 2:3, :]
        yproj = (jnp.dot(ycat_ref[drain, :, 0:dm], wout_ref[0:dm, :], preferred_element_type=F32)
                 + jnp.dot(jnp.concatenate(yb, axis=1), wout_ref[dm:, :], preferred_element_type=F32))
        x1 = x_ref[0] + gate1 * yproj

        shift2 = mod_ref[0, 3:4, :]
        scale2 = mod_ref[0, 4:5, :]
        gate2 = mod_ref[0, 5:6, :]
        h2 = ((_rms_scale(x1) * gffn_ref[...]) * (1.0 + scale2) + shift2).astype(BF16)
        acc = jnp.zeros(x1.shape, F32)
        for j in range(wff1_ref.shape[1] // FF_COLS):
            hid = jnp.dot(h2, wff1_ref[:, j * FF_COLS:(j + 1) * FF_COLS], preferred_element_type=F32)
            hid = jnp.square(jnp.maximum(hid, 0.0)).astype(BF16)
            acc = acc + jnp.dot(hid, wff2_ref[j * FF_COLS:(j + 1) * FF_COLS, :], preferred_element_type=F32)
        x2 = x1 + gate2 * acc
        if final_norm:
            x2 = _rms_scale(x2) * gfinal_ref[...]
        out_ref[0] = x2

    def elementwise_stage(fill):
        for hd in range(N_HEADS):
            cols = slice(hd * HEAD_DIM, (hd + 1) * HEAD_DIM)
            hn = _rms_scale(hm_ref[0, hd].astype(F32)) * ghead_ref[:, cols]
            gate = jax.nn.sigmoid(o_ref[0, :, cols].astype(F32))
            ycat_ref[fill, :, cols] = (gate * hn).astype(BF16)

        ubuf_ref[0:POOL_HALO, :] = jnp.where(tile > 0, uprev_ref[0], jnp.zeros_like(uprev_ref[0]))
        ubuf_ref[POOL_HALO:POOL_HALO + tm, :] = u_ref[0]
        ubuf_ref[POOL_HALO + tm:, :] = jnp.where(tile < tiles_per_seq - 1, unext_ref[0],
                                                 jnp.zeros_like(unext_ref[0]))
        for gi in range(len(POOL_WINDOWS)):
            cols = slice(gi * HEAD_DIM, (gi + 1) * HEAD_DIM)
            for r0 in range(0, tm, POOL_BLOCK):
                ext = ubuf_ref[r0:r0 + POOL_BLOCK + 2 * POOL_HALO, cols]
                mixed = jnp.dot(band_ref[gi], ext, preferred_element_type=F32)
                ycat_ref[fill, r0:r0 + POOL_BLOCK, dm + gi * HEAD_DIM:dm + (gi + 1) * HEAD_DIM] = mixed.astype(BF16)

    def sequence_end_rows(fill):
        def clipped_rows(r0):
            ext = ubuf_ref[r0:r0 + 3 * POOL_HALO, :].astype(F32)
            pos = tile * tm + r0 + lax.broadcasted_iota(jnp.int32, (POOL_HALO, HEAD_DIM), 0)
            for gi, win in enumerate(POOL_WINDOWS):
                half = win // 2
                grp = ext[:, gi * HEAD_DIM:(gi + 1) * HEAD_DIM]
                total = grp[POOL_HALO - half:2 * POOL_HALO - half]
                for j in range(1 - half, half):
                    total = total + grp[POOL_HALO + j:2 * POOL_HALO + j]
                cnt = jnp.minimum(pos + half, seq) - jnp.maximum(pos - half, 0)
                mixed = total / cnt.astype(F32) - grp[POOL_HALO:2 * POOL_HALO]
                ycat_ref[fill, r0:r0 + POOL_HALO, dm + gi * HEAD_DIM:dm + (gi + 1) * HEAD_DIM] = mixed.astype(BF16)

        @pl.when(tile == 0)
        def _():
            clipped_rows(0)

        @pl.when(tile == tiles_per_seq - 1)
        def _():
            clipped_rows(tm - POOL_HALO)

    @pl.when(step == 0)
    def _():
        elementwise_stage(0)
        sequence_end_rows(0)

    @pl.when(step == n_tiles)
    def _():
        matmul_stage((n_tiles - 1) % 2)

    for fill in range(2):
        @pl.when((step > 0) & (step < n_tiles) & (step % 2 == fill))
        def _():
            matmul_stage(1 - fill)
            elementwise_stage(fill)
            sequence_end_rows(fill)


def _pool_bands():
    row = np.arange(POOL_BLOCK)[:, None]
    col = np.arange(POOL_BLOCK + 2 * POOL_HALO)[None, :] - POOL_HALO
    bands = [((col >= row - win // 2) & (col < row + win // 2)) / win - (col == row) for win in POOL_WINDOWS]
    return jnp.asarray(np.stack(bands), dtype=BF16)


def _outffn(x, hm, o, u, mod, g_head, pool_scale, g_ffn, g_final, w_pool, w_out, w_ff1, w_ff2,
            final_norm):
    batch, seq, d = x.shape
    tm = OUTFFN_ROWS
    dm = N_HEADS * HEAD_DIM
    d_pool = u.shape[2]
    halo_per_tile = tm // POOL_HALO
    n_halo_blocks = seq // POOL_HALO
    tiles_per_seq = seq // tm
    n_tiles = batch * tiles_per_seq

    def mm_tile(g):
        t = jnp.maximum(g - 1, 0)
        return t // tiles_per_seq, t % tiles_per_seq

    def ew_tile(g):
        t = jnp.minimum(g, n_tiles - 1)
        return t // tiles_per_seq, t % tiles_per_seq

    def mm_rows(g):
        b, i = mm_tile(g)
        return b, i, 0

    def ew_rows(g):
        b, i = ew_tile(g)
        return b, i, 0

    def ew_heads(g):
        b, i = ew_tile(g)
        return b, 0, i, 0

    def ew_prev(g):
        b, i = ew_tile(g)
        return b, jnp.maximum(i * halo_per_tile - 1, 0), 0

    def ew_next(g):
        b, i = ew_tile(g)
        return b, jnp.minimum((i + 1) * halo_per_tile, n_halo_blocks - 1), 0

    bands = _pool_bands()
    kernel = functools.partial(_outffn_kernel, seq=seq, n_tiles=n_tiles, final_norm=final_norm)
    return pl.pallas_call(
        kernel,
        grid=(n_tiles + 1,),
        in_specs=[pl.BlockSpec((1, tm, d), mm_rows),
                  pl.BlockSpec((1, N_HEADS, tm, HEAD_DIM), ew_heads),
                  pl.BlockSpec((1, tm, dm), ew_rows),
                  pl.BlockSpec((1, tm, d_pool), ew_rows),
                  pl.BlockSpec((1, POOL_HALO, d_pool), ew_prev),
                  pl.BlockSpec((1, POOL_HALO, d_pool), ew_next),
                  pl.BlockSpec((1,) + mod.shape[1:], lambda g: (mm_tile(g)[0], 0, 0)),
                  _const_spec((1, dm)),
                  _const_spec((1, d_pool)),
                  _const_spec((1, d)),
                  _const_spec((1, d)),
                  _const_spec(bands.shape),
                  _const_spec(w_pool.shape),
                  _const_spec(w_out.shape),
                  _const_spec(w_ff1.shape),
                  _const_spec(w_ff2.shape)],
        out_specs=pl.BlockSpec((1, tm, d), mm_rows),
        out_shape=jax.ShapeDtypeStruct((batch, seq, d), F32),
        scratch_shapes=[pltpu.VMEM((tm + 2 * POOL_HALO, d_pool), BF16),
                        pltpu.VMEM((2, tm, dm + d_pool), BF16)],
        compiler_params=pltpu.CompilerParams(dimension_semantics=("arbitrary",),
                                             vmem_limit_bytes=VMEM_LIMIT_BYTES),
    )(x, hm, o, u, u, u, mod, g_head.reshape(1, dm), pool_scale.reshape(1, d_pool),
      g_ffn.reshape(1, d), g_final.reshape(1, d), bands, w_pool, w_out, w_ff1, w_ff2)


def _pack_w_in(w_in):
    dm = N_HEADS * HEAD_DIM
    n_gates = N_DIRS * 2 * N_HEADS
    q, k, v, o = (w_in[:, i * dm:(i + 1) * dm] for i in range(4))
    gates = w_in[:, 4 * dm:4 * dm + n_gates]
    pool = w_in[:, 4 * dm + n_gates:]
    gates_t = gates.reshape(-1, N_DIRS, 2, N_HEADS).transpose(3, 1, 2, 0).reshape(n_gates, -1)
    cols = jnp.concatenate([k, o, pool], axis=1).astype(BF16)
    rows = jnp.concatenate([q.T, v.T, gates_t], axis=0).astype(BF16)
    return cols, rows


def _pack_gate_bias(b_igate, b_fgate):
    rows = jnp.stack([b_igate, b_fgate], axis=1)
    rows = rows.transpose(2, 0, 1).reshape(N_HEADS, N_GATE_ROWS)
    return jnp.broadcast_to(rows[:, :, None], (N_HEADS, N_GATE_ROWS, CHUNK)).astype(F32)


def kernel(x, c, w_ada, b_ada, g_mix, w_in, b_igate, b_fgate, g_head, w_pool, pool_scale, w_out,
           g_ffn, w_ff1, w_ff2, g_final):
    depth = w_ada.shape[0]
    batch, seq, d = x.shape
    d_pool = pool_scale.shape[1]
    assert g_head.shape[1] == N_HEADS * HEAD_DIM and w_pool.shape[1] == len(POOL_WINDOWS)
    assert seq % INPROJ_ROWS == 0 and seq % OUTFFN_ROWS == 0 and w_ff1.shape[2] % FF_COLS == 0
    for l in range(depth):
        mod = _adaln(c, w_ada[l], b_ada[l]).reshape(batch, 6, d)
        gdim = w_pool.shape[2]
        (qt, k, vt, o, u, gates), (w_out_b, w_ff1_b) = _inproj(
            x, mod, g_mix[l], *_pack_w_in(w_in[l]), d_pool, (w_out[l], w_ff1[l]))
        hm, (w_pool_b, w_ff2_b) = _mlstm(
            qt, k, vt, gates, _pack_gate_bias(b_igate[l], b_fgate[l]),
            (w_pool[l].reshape(-1, gdim), w_ff2[l]))
        x = _outffn(x, hm, o, u, mod, g_head[l], pool_scale[l], g_ffn[l], g_final,
                    w_pool_b.reshape(-1, gdim, gdim), w_out_b, w_ff1_b, w_ff2_b, final_norm=(l == depth - 1))
    return x
```

```python
import functools

import jax
import jax.numpy as jnp
import numpy as np
from jax import lax
from jax.experimental import pallas as pl
from jax.experimental.pallas import tpu as pltpu

F32 = jnp.float32
BF16 = jnp.bfloat16

LANES = 128
BF16_SUBLANES = 16
VMEM_LIMIT_BYTES = 56 * 1024 * 1024

N_HEADS = 4
HEAD_DIM = 128
CHUNK = 128
POOL_WINDOWS = (2, 4, 8, 16)
POOL_BLOCK = 128
POOL_HALO = 16
N_DIRS = 2
N_GATE_ROWS = 2 * N_DIRS
EPS = 1e-6
LOG2_E = 1.4426950408889634

ADALN_COLS = 2048
INPROJ_ROWS = 1024
OUTFFN_ROWS = 512
FF_COLS = 1024
AUG_ROWS = HEAD_DIM + BF16_SUBLANES

ROW_A, ROW_M, ROW_AINTER, ROW_EXPNEG, ROW_WK, ROW_DECAY = range(6)
N_COEF = 6
CHUNKS_PER_STEP = 32


def _const_spec(shape):
    zeros = (0,) * len(shape)
    return pl.BlockSpec(shape, lambda *_: zeros, pipeline_mode=pl.Buffered(1))


def _rms_scale(x):
    return x * lax.rsqrt(jnp.mean(x * x, axis=-1, keepdims=True) + EPS)


def _modulated_norm_bf16(x, gain, shift, scale):
    slope = (gain * (1.0 + scale)).astype(BF16)
    return _rms_scale(x).astype(BF16) * slope + shift.astype(BF16)


def _adaln_kernel(c_ref, w_ref, b_ref, o_ref):
    c = c_ref[...]
    act = c * jax.nn.sigmoid(c)
    o_ref[...] = jnp.dot(act.astype(BF16), w_ref[...].astype(BF16),
                         preferred_element_type=F32) + b_ref[...]


def _adaln(c, w_ada, b_ada):
    batch, d = c.shape
    n = w_ada.shape[1]
    cols = ADALN_COLS
    assert n % cols == 0
    return pl.pallas_call(
        _adaln_kernel,
        grid=(n // cols,),
        in_specs=[pl.BlockSpec((batch, d), lambda j: (0, 0)),
                  pl.BlockSpec((d, cols), lambda j: (0, j)),
                  pl.BlockSpec((1, cols), lambda j: (0, j))],
        out_specs=pl.BlockSpec((batch, cols), lambda j: (0, j)),
        out_shape=jax.ShapeDtypeStruct((batch, n), F32),
        compiler_params=pltpu.CompilerParams(dimension_semantics=("arbitrary",),
                                             vmem_limit_bytes=VMEM_LIMIT_BYTES),
    )(c, w_ada, b_ada.reshape(1, n))


def _inproj_kernel(x_ref, mod_ref, g_ref, w_ref, wt_ref, *refs, n_casts):
    cast_in, cast_out = refs[:n_casts], refs[n_casts + 6:]
    qt_ref, k_ref, vt_ref, o_ref, u_ref, gates_ref = refs[n_casts:n_casts + 6]
    for src, dst in zip(cast_in, cast_out):
        dst[...] = src[...].astype(BF16)

    dm = N_HEADS * HEAD_DIM
    x = x_ref[0]
    shift = mod_ref[0, 0:1, :]
    scale = mod_ref[0, 1:2, :]
    hb = _modulated_norm_bf16(x, g_ref[...], shift, scale)

    by_token = jnp.dot(hb, w_ref[...], preferred_element_type=F32)
    for hd in range(N_HEADS):
        k_ref[0, hd] = by_token[:, hd * HEAD_DIM:(hd + 1) * HEAD_DIM].astype(BF16)
    o_ref[0] = by_token[:, dm:2 * dm].astype(BF16)
    u_ref[0] = by_token[:, 2 * dm:].astype(BF16)
    by_feature = lax.dot_general(wt_ref[...], hb, (((1,), (1,)), ((), ())), preferred_element_type=F32)
    for hd in range(N_HEADS):
        rows = slice(hd * HEAD_DIM, (hd + 1) * HEAD_DIM)
        qt_ref[0, hd] = (by_feature[rows, :] * (HEAD_DIM ** -0.5)).astype(BF16)
        vt_ref[0, hd, 0:HEAD_DIM, :] = by_feature[dm + hd * HEAD_DIM:dm + (hd + 1) * HEAD_DIM, :].astype(BF16)
        vt_ref[0, hd, HEAD_DIM:AUG_ROWS, :] = jnp.ones((AUG_ROWS - HEAD_DIM, x.shape[0]), BF16)
    gt = by_feature[2 * dm:, :]
    n_chunks = x.shape[0] // CHUNK
    for hd in range(N_HEADS):
        for r in range(N_GATE_ROWS):
            row = hd * N_GATE_ROWS + r
            for j in range(n_chunks):
                gates_ref[0, hd, r, j:j + 1, :] = gt[row:row + 1, j * CHUNK:(j + 1) * CHUNK]


def _inproj(x, mod, g_mix, w_cat, w_gates_t, d_pool, weights_to_cast):
    batch, seq, d = x.shape
    tm = INPROJ_ROWS
    dm = N_HEADS * HEAD_DIM
    grid = (batch, seq // tm)
    tiles_per_seq = seq // tm
    n_steps = batch * tiles_per_seq
    cast_specs = []
    for w in weights_to_cast:
        assert w.shape[0] % (n_steps * BF16_SUBLANES) == 0
        cast_specs.append(pl.BlockSpec((w.shape[0] // n_steps, w.shape[1]),
                                       lambda b, i: (b * tiles_per_seq + i, 0)))
    head_t = jax.ShapeDtypeStruct((batch, N_HEADS, HEAD_DIM, seq), BF16)
    head_n = jax.ShapeDtypeStruct((batch, N_HEADS, seq, HEAD_DIM), BF16)
    head_aug = jax.ShapeDtypeStruct((batch, N_HEADS, AUG_ROWS, seq), BF16)
    out_shape = (head_t, head_n, head_aug,
                 jax.ShapeDtypeStruct((batch, seq, dm), BF16),
                 jax.ShapeDtypeStruct((batch, seq, d_pool), BF16),
                 jax.ShapeDtypeStruct((batch, N_HEADS, N_GATE_ROWS, seq // CHUNK, CHUNK), F32))
    spec_t = pl.BlockSpec((1, N_HEADS, HEAD_DIM, tm), lambda b, i: (b, 0, 0, i))
    spec_n = pl.BlockSpec((1, N_HEADS, tm, HEAD_DIM), lambda b, i: (b, 0, i, 0))
    spec_aug = pl.BlockSpec((1, N_HEADS, AUG_ROWS, tm), lambda b, i: (b, 0, 0, i))
    out_specs = (spec_t, spec_n, spec_aug,
                 pl.BlockSpec((1, tm, dm), lambda b, i: (b, i, 0)),
                 pl.BlockSpec((1, tm, d_pool), lambda b, i: (b, i, 0)),
                 pl.BlockSpec((1, N_HEADS, N_GATE_ROWS, tm // CHUNK, CHUNK), lambda b, i: (b, 0, 0, i, 0)))
    outs = pl.pallas_call(
        functools.partial(_inproj_kernel, n_casts=len(weights_to_cast)),
        grid=grid,
        in_specs=[pl.BlockSpec((1, tm, d), lambda b, i: (b, i, 0)),
                  pl.BlockSpec((1,) + mod.shape[1:], lambda b, i: (b, 0, 0)),
                  _const_spec((1, d)),
                  _const_spec(w_cat.shape),
                  _const_spec(w_gates_t.shape)] + cast_specs,
        out_specs=list(out_specs) + cast_specs,
        out_shape=list(out_shape) + [jax.ShapeDtypeStruct(w.shape, BF16) for w in weights_to_cast],
        compiler_params=pltpu.CompilerParams(dimension_semantics=("arbitrary", "arbitrary"),
                                             vmem_limit_bytes=VMEM_LIMIT_BYTES),
    )(x, mod, g_mix.reshape(1, d), w_cat, w_gates_t, *weights_to_cast)
    return outs[:6], outs[6:]


def _time_scan(x, op, fill, reverse):
    row = lax.broadcasted_iota(jnp.int32, x.shape, 0)
    step = 1
    while step < CHUNK:
        if reverse:
            shifted = jnp.where(row < CHUNK - step, pltpu.roll(x, CHUNK - step, axis=0), fill)
        else:
            shifted = jnp.where(row >= step, pltpu.roll(x, step, axis=0), fill)
        x = op(x, shifted)
        step *= 2
    return x


def _chunk_carry(decay_log, peak, reverse):
    n = decay_log.shape[0]
    row = lax.broadcasted_iota(jnp.int32, decay_log.shape, 0)

    def earlier(x, dist, fill):
        if reverse:
            return jnp.where(row < n - dist, pltpu.roll(x, n - dist, axis=0), fill)
        return jnp.where(row >= dist, pltpu.roll(x, dist, axis=0), fill)

    p, q = decay_log, decay_log + peak
    dist = 1
    while dist < n:
        q = jnp.maximum(earlier(q, dist, -jnp.inf) + p, q)
        p = earlier(p, dist, 0.0) + p
        dist *= 2
    return earlier(jnp.maximum(p, q), 1, 0.0)


def _mlstm_kernel(qt_ref, k_ref, vt_ref, gates_ref, gbias_ref, *refs, n_casts, n_units):
    cast_in, out_ref, cast_out = refs[:n_casts], refs[n_casts], refs[n_casts + 1:2 * n_casts + 1]
    coef_ref, sprev_ref, st_ref = refs[2 * n_casts + 1:]
    for src, dst in zip(cast_in, cast_out):
        dst[...] = src[...].astype(BF16)

    n_chunks = gates_ref.shape[3]
    assert 2 * N_DIRS * n_chunks == LANES
    step = pl.program_id(0)

    def prologue(slot):
        log_i, log_f = [], []
        for d in range(N_DIRS):
            log_i.append(gates_ref[0, 0, 2 * d] + gbias_ref[0, 2 * d:2 * d + 1, :])
            z = gates_ref[0, 0, 2 * d + 1] + gbias_ref[0, 2 * d + 1:2 * d + 2, :]
            log_f.append(-(jnp.maximum(-z, 0.0) + jnp.log1p(jnp.exp(-jnp.abs(z)))))
        by_time = jnp.concatenate(log_f + log_i, axis=0).T
        lane = lax.broadcasted_iota(jnp.int32, by_time.shape, 1)
        fwd_lane = (lane % (N_DIRS * n_chunks)) < n_chunks
        b_t = jnp.where(fwd_lane, _time_scan(by_time, jnp.add, 0.0, False),
                        _time_scan(by_time, jnp.add, 0.0, True))
        a_t = pltpu.roll(by_time, N_DIRS * n_chunks, axis=1) - b_t
        cm_t = jnp.where(fwd_lane, _time_scan(a_t, jnp.maximum, -jnp.inf, False),
                         _time_scan(a_t, jnp.maximum, -jnp.inf, True))
        b_rows, a_rows, cm_rows = b_t.T, a_t.T, cm_t.T

        for d in range(N_DIRS):
            reverse = d == 1
            last = 0 if reverse else CHUNK - 1
            b = b_rows[d * n_chunks:(d + 1) * n_chunks]
            a = a_rows[d * n_chunks:(d + 1) * n_chunks]
            cm = cm_rows[d * n_chunks:(d + 1) * n_chunks]
            b_last = jnp.broadcast_to(b[:, last:last + 1], b.shape)
            cm_last = jnp.broadcast_to(cm[:, last:last + 1], b.shape)
            m_prev = _chunk_carry(b_last, cm_last, reverse)
            m_row = jnp.maximum(m_prev, cm)
            a_inter = jnp.exp(m_prev - m_row)
            exp_neg = jnp.exp(-(b + m_row))
            m_end = jnp.broadcast_to(m_row[:, last:last + 1], b.shape)
            wk = jnp.exp(a - m_end)
            decay = jnp.broadcast_to(a_inter[:, last:last + 1], b.shape)
            for r, val in ((ROW_A, a * LOG2_E), (ROW_M, m_row * LOG2_E), (ROW_AINTER, a_inter),
                           (ROW_EXPNEG, exp_neg), (ROW_WK, wk), (ROW_DECAY, decay)):
                coef_ref[slot, d, r] = val

    def passes(slot):
        def coef_row(d, r, c):
            return coef_ref[slot, d, r, pl.ds(c, 1), :]

        def chunk_operands(c):
            off = pl.multiple_of(c * CHUNK, CHUNK)
            return off, qt_ref[0, 0, :, pl.ds(off, CHUNK)], k_ref[0, 0, pl.ds(off, CHUNK), :], \
                vt_ref[0, 0, :, pl.ds(off, CHUNK)]

        key_pos = lax.broadcasted_iota(jnp.int32, (CHUNK, CHUNK), 0)
        qry_pos = lax.broadcasted_iota(jnp.int32, (CHUNK, CHUNK), 1)
        visible = (key_pos <= qry_pos, key_pos >= qry_pos)

        def weighted_scores(c):
            _, qt_c, k_c, _ = chunk_operands(c)
            scores = jnp.dot(k_c, qt_c, preferred_element_type=F32)
            for d in range(N_DIRS):
                a_col = jnp.broadcast_to(coef_row(d, ROW_A, c), (CHUNK, CHUNK)).T
                weight = jnp.exp2(jnp.where(visible[d], a_col - coef_row(d, ROW_M, c), -jnp.inf))
                st_ref[d, c] = (scores * weight).astype(BF16)

        def state_step(d, c, state):
            _, _, k_c, vt_c = chunk_operands(c)
            sprev_ref[d, c] = state.astype(BF16)
            vw = vt_c * coef_row(d, ROW_WK, c).astype(BF16)
            return coef_row(d, ROW_DECAY, c) * state + jnp.dot(vw, k_c, preferred_element_type=F32)

        def state_body(i, states):
            s_fwd, s_bwd = states
            for j in range(CHUNKS_PER_STEP):
                c = i * CHUNKS_PER_STEP + j
                weighted_scores(c)
                s_fwd = state_step(0, c, s_fwd)
                s_bwd = state_step(1, n_chunks - 1 - c, s_bwd)
            return s_fwd, s_bwd

        zero_state = jnp.zeros((AUG_ROWS, HEAD_DIM), F32)
        lax.fori_loop(0, n_chunks // CHUNKS_PER_STEP, state_body, (zero_state, zero_state))

        def output_chunk(c):
            off, qt_c, _, vt_c = chunk_operands(c)
            hsum = None
            for d in range(N_DIRS):
                qa = qt_c * coef_row(d, ROW_AINTER, c).astype(BF16)
                lhs = jnp.concatenate([vt_c, sprev_ref[d, c]], axis=1)
                rhs = jnp.concatenate([st_ref[d, c], qa], axis=0)
                numden = jnp.dot(lhs, rhs, preferred_element_type=F32)
                den = numden[HEAD_DIM:HEAD_DIM + 1, :]
                ht = numden[0:HEAD_DIM, :] * (1.0 / jnp.maximum(jnp.abs(den), coef_row(d, ROW_EXPNEG, c)))
                hsum = ht if hsum is None else hsum + ht
            out_ref[0, 0, pl.ds(off, CHUNK), :] = hsum.T.astype(BF16)

        def output_body(i, carry):
            for j in range(CHUNKS_PER_STEP):
                output_chunk(i * CHUNKS_PER_STEP + j)
            return carry

        lax.fori_loop(0, n_chunks // CHUNKS_PER_STEP, output_body, 0)

    @pl.when(step == 0)
    def _():
        prologue(0)

    @pl.when(step == n_units)
    def _():
        passes((n_units - 1) % 2)

    for fill in range(2):
        @pl.when((step > 0) & (step < n_units) & (step % 2 == fill))
        def _():
            passes(1 - fill)
            prologue(fill)


def _mlstm(qt, k, vt, gates, gbias, weights_to_cast):
    batch, heads, hd, seq = qt.shape
    n_chunks = seq // CHUNK
    assert n_chunks % CHUNKS_PER_STEP == 0
    n_units = batch * heads

    def pass_unit(g):
        u = jnp.maximum(g - 1, 0)
        return u // heads, u % heads, 0, 0

    def prologue_unit(g):
        return jnp.minimum(g, n_units - 1)

    spec_t = pl.BlockSpec((1, 1, hd, seq), pass_unit)
    spec_n = pl.BlockSpec((1, 1, seq, hd), pass_unit)
    spec_aug = pl.BlockSpec((1, 1, AUG_ROWS, seq), pass_unit)
    cast_specs = []
    for w in weights_to_cast:
        assert w.shape[0] % (n_units * BF16_SUBLANES) == 0
        cast_specs.append(pl.BlockSpec((w.shape[0] // n_units, w.shape[1]), lambda g: (prologue_unit(g), 0)))
    outs = pl.pallas_call(
        functools.partial(_mlstm_kernel, n_casts=len(weights_to_cast), n_units=n_units),
        grid=(n_units + 1,),
        in_specs=[spec_t, spec_n, spec_aug,
                  pl.BlockSpec((1, 1, N_GATE_ROWS, n_chunks, CHUNK),
                               lambda g: (prologue_unit(g) // heads, prologue_unit(g) % heads, 0, 0, 0)),
                  pl.BlockSpec((1, N_GATE_ROWS, CHUNK), lambda g: (prologue_unit(g) % heads, 0, 0))] + cast_specs,
        out_specs=[spec_n] + cast_specs,
        out_shape=[jax.ShapeDtypeStruct((batch, heads, seq, hd), BF16)]
        + [jax.ShapeDtypeStruct(w.shape, BF16) for w in weights_to_cast],
        scratch_shapes=[pltpu.VMEM((2, N_DIRS, N_COEF, n_chunks, CHUNK), F32),
                        pltpu.VMEM((N_DIRS, n_chunks, AUG_ROWS, hd), BF16),
                        pltpu.VMEM((N_DIRS, n_chunks, CHUNK, CHUNK), BF16)],
        compiler_params=pltpu.CompilerParams(dimension_semantics=("arbitrary",),
                                             vmem_limit_bytes=VMEM_LIMIT_BYTES),
    )(qt, k, vt, gates, gbias, *weights_to_cast)
    return outs[0], outs[1:]


def _outffn_kernel(x_ref, hm_ref, o_ref, u_ref, uprev_ref, unext_ref, mod_ref,
                   ghead_ref, pscale_ref, gffn_ref, gfinal_ref,
                   band_ref, wpool_ref, wout_ref, wff1_ref, wff2_ref, out_ref,
                   ubuf_ref, ycat_ref, *, seq, n_tiles, final_norm):
    tm = x_ref.shape[1]
    dm = N_HEADS * HEAD_DIM
    step = pl.program_id(0)
    tiles_per_seq = seq // tm
    tile = step % tiles_per_seq

    def matmul_stage(drain):
        yb = []
        for gi in range(len(POOL_WINDOWS)):
            cols = slice(gi * HEAD_DIM, (gi + 1) * HEAD_DIM)
            mapped = jnp.dot(ycat_ref[drain, :, dm + gi * HEAD_DIM:dm + (gi + 1) * HEAD_DIM], wpool_ref[gi],
                             preferred_element_type=F32)
            yb.append((mapped * pscale_ref[:, cols]).astype(BF16))
        gate1 = mod_ref[0, 2:3, :]
        yproj = (jnp.dot(ycat_ref[drain, :, 0:dm], wout_ref[0:dm, :], preferred_element_type=F32)
                 + jnp.dot(jnp.concatenate(yb, axis=1), wout_ref[dm:, :], preferred_element_type=F32))
        x1 = x_ref[0] + gate1 * yproj

        shift2 = mod_ref[0, 3:4, :]
        scale2 = mod_ref[0, 4:5, :]
        gate2 = mod_ref[0, 5:6, :]
        h2 = ((_rms_scale(x1) * gffn_ref[...]) * (1.0 + scale2) + shift2).astype(BF16)
        acc = jnp.zeros(x1.shape, F32)
        for j in range(wff1_ref.shape[1] // FF_COLS):
            hid = jnp.dot(h2, wff1_ref[:, j * FF_COLS:(j + 1) * FF_COLS], preferred_element_type=F32)
            hid = jnp.square(jnp.maximum(hid, 0.0)).astype(BF16)
            acc = acc + jnp.dot(hid, wff2_ref[j * FF_COLS:(j + 1) * FF_COLS, :], preferred_element_type=F32)
        x2 = x1 + gate2 * acc
        if final_norm:
            x2 = _rms_scale(x2) * gfinal_ref[...]
        out_ref[0] = x2

    def elementwise_stage(fill):
        for hd in range(N_HEADS):
            cols = slice(hd * HEAD_DIM, (hd + 1) * HEAD_DIM)
            hn = _rms_scale(hm_ref[0, hd].astype(F32)) * ghead_ref[:, cols]
            gate = jax.nn.sigmoid(o_ref[0, :, cols].astype(F32))
            ycat_ref[fill, :, cols] = (gate * hn).astype(BF16)

        ubuf_ref[0:POOL_HALO, :] = jnp.where(tile > 0, uprev_ref[0], jnp.zeros_like(uprev_ref[0]))
        ubuf_ref[POOL_HALO:POOL_HALO + tm, :] = u_ref[0]
        ubuf_ref[POOL_HALO + tm:, :] = jnp.where(tile < tiles_per_seq - 1, unext_ref[0],
                                                 jnp.zeros_like(unext_ref[0]))
        for gi in range(len(POOL_WINDOWS)):
            cols = slice(gi * HEAD_DIM, (gi + 1) * HEAD_DIM)
            for r0 in range(0, tm, POOL_BLOCK):
                ext = ubuf_ref[r0:r0 + POOL_BLOCK + 2 * POOL_HALO, cols]
                mixed = jnp.dot(band_ref[gi], ext, preferred_element_type=F32)
                ycat_ref[fill, r0:r0 + POOL_BLOCK, dm + gi * HEAD_DIM:dm + (gi + 1) * HEAD_DIM] = mixed.astype(BF16)

    def sequence_end_rows(fill):
        def clipped_rows(r0):
            ext = ubuf_ref[r0:r0 + 3 * POOL_HALO, :].astype(F32)
            pos = tile * tm + r0 + lax.broadcasted_iota(jnp.int32, (POOL_HALO, HEAD_DIM), 0)
            for gi, win in enumerate(POOL_WINDOWS):
                half = win // 2
                grp = ext[:, gi * HEAD_DIM:(gi + 1) * HEAD_DIM]
                total = grp[POOL_HALO - half:2 * POOL_HALO - half]
                for j in range(1 - half, half):
                    total = total + grp[POOL_HALO + j:2 * POOL_HALO + j]
                cnt = jnp.minimum(pos + half, seq) - jnp.maximum(pos - half, 0)
                mixed = total / cnt.astype(F32) - grp[POOL_HALO:2 * POOL_HALO]
                ycat_ref[fill, r0:r0 + POOL_HALO, dm + gi * HEAD_DIM:dm + (gi + 1) * HEAD_DIM] = mixed.astype(BF16)

        @pl.when(tile == 0)
        def _():
            clipped_rows(0)

        @pl.when(tile == tiles_per_seq - 1)
        def _():
            clipped_rows(tm - POOL_HALO)

    @pl.when(step == 0)
    def _():
        elementwise_stage(0)
        sequence_end_rows(0)

    @pl.when(step == n_tiles)
    def _():
        matmul_stage((n_tiles - 1) % 2)

    for fill in range(2):
        @pl.when((step > 0) & (step < n_tiles) & (step % 2 == fill))
        def _():
            matmul_stage(1 - fill)
            elementwise_stage(fill)
            sequence_end_rows(fill)


def _pool_bands():
    row = np.arange(POOL_BLOCK)[:, None]
    col = np.arange(POOL_BLOCK + 2 * POOL_HALO)[None, :] - POOL_HALO
    bands = [((col >= row - win // 2) & (col < row + win // 2)) / win - (col == row) for win in POOL_WINDOWS]
    return jnp.asarray(np.stack(bands), dtype=BF16)


def _outffn(x, hm, o, u, mod, g_head, pool_scale, g_ffn, g_final, w_pool, w_out, w_ff1, w_ff2,
            final_norm):
    batch, seq, d = x.shape
    tm = OUTFFN_ROWS
    dm = N_HEADS * HEAD_DIM
    d_pool = u.shape[2]
    halo_per_tile = tm // POOL_HALO
    n_halo_blocks = seq // POOL_HALO
    tiles_per_seq = seq // tm
    n_tiles = batch * tiles_per_seq

    def mm_tile(g):
        t = jnp.maximum(g - 1, 0)
        return t // tiles_per_seq, t % tiles_per_seq

    def ew_tile(g):
        t = jnp.minimum(g, n_tiles - 1)
        return t // tiles_per_seq, t % tiles_per_seq

    def mm_rows(g):
        b, i = mm_tile(g)
        return b, i, 0

    def ew_rows(g):
        b, i = ew_tile(g)
        return b, i, 0

    def ew_heads(g):
        b, i = ew_tile(g)
        return b, 0, i, 0

    def ew_prev(g):
        b, i = ew_tile(g)
        return b, jnp.maximum(i * halo_per_tile - 1, 0), 0

    def ew_next(g):
        b, i = ew_tile(g)
        return b, jnp.minimum((i + 1) * halo_per_tile, n_halo_blocks - 1), 0

    bands = _pool_bands()
    kernel = functools.partial(_outffn_kernel, seq=seq, n_tiles=n_tiles, final_norm=final_norm)
    return pl.pallas_call(
        kernel,
        grid=(n_tiles + 1,),
        in_specs=[pl.BlockSpec((1, tm, d), mm_rows),
                  pl.BlockSpec((1, N_HEADS, tm, HEAD_DIM), ew_heads),
                  pl.BlockSpec((1, tm, dm), ew_rows),
                  pl.BlockSpec((1, tm, d_pool), ew_rows),
                  pl.BlockSpec((1, POOL_HALO, d_pool), ew_prev),
                  pl.BlockSpec((1, POOL_HALO, d_pool), ew_next),
                  pl.BlockSpec((1,) + mod.shape[1:], lambda g: (mm_tile(g)[0], 0, 0)),
                  _const_spec((1, dm)),
                  _const_spec((1, d_pool)),
                  _const_spec((1, d)),
                  _const_spec((1, d)),
                  _const_spec(bands.shape),
                  _const_spec(w_pool.shape),
                  _const_spec(w_out.shape),
                  _const_spec(w_ff1.shape),
                  _const_spec(w_ff2.shape)],
        out_specs=pl.BlockSpec((1, tm, d), mm_rows),
        out_shape=jax.ShapeDtypeStruct((batch, seq, d), F32),
        scratch_shapes=[pltpu.VMEM((tm + 2 * POOL_HALO, d_pool), BF16),
                        pltpu.VMEM((2, tm, dm + d_pool), BF16)],
        compiler_params=pltpu.CompilerParams(dimension_semantics=("arbitrary",),
                                             vmem_limit_bytes=VMEM_LIMIT_BYTES),
    )(x, hm, o, u, u, u, mod, g_head.reshape(1, dm), pool_scale.reshape(1, d_pool),
      g_ffn.reshape(1, d), g_final.reshape(1, d), bands, w_pool, w_out, w_ff1, w_ff2)


def _pack_w_in(w_in):
    dm = N_HEADS * HEAD_DIM
    n_gates = N_DIRS * 2 * N_HEADS
    q, k, v, o = (w_in[:, i * dm:(i + 1) * dm] for i in range(4))
    gates = w_in[:, 4 * dm:4 * dm + n_gates]
    pool = w_in[:, 4 * dm + n_gates:]
    gates_t = gates.reshape(-1, N_DIRS, 2, N_HEADS).transpose(3, 1, 2, 0).reshape(n_gates, -1)
    cols = jnp.concatenate([k, o, pool], axis=1).astype(BF16)
    rows = jnp.concatenate([q.T, v.T, gates_t], axis=0).astype(BF16)
    return cols, rows


def _pack_gate_bias(b_igate, b_fgate):
    rows = jnp.stack([b_igate, b_fgate], axis=1)
    rows = rows.transpose(2, 0, 1).reshape(N_HEADS, N_GATE_ROWS)
    return jnp.broadcast_to(rows[:, :, None], (N_HEADS, N_GATE_ROWS, CHUNK)).astype(F32)


def kernel(x, c, w_ada, b_ada, g_mix, w_in, b_igate, b_fgate, g_head, w_pool, pool_scale, w_out,
           g_ffn, w_ff1, w_ff2, g_final):
    depth = w_ada.shape[0]
    batch, seq, d = x.shape
    d_pool = pool_scale.shape[1]
    assert g_head.shape[1] == N_HEADS * HEAD_DIM and w_pool.shape[1] == len(POOL_WINDOWS)
    assert seq % INPROJ_ROWS == 0 and seq % OUTFFN_ROWS == 0 and w_ff1.shape[2] % FF_COLS == 0
    for l in range(depth):
        mod = _adaln(c, w_ada[l], b_ada[l]).reshape(batch, 6, d)
        gdim = w_pool.shape[2]
        (qt, k, vt, o, u, gates), (w_out_b, w_ff1_b) = _inproj(
            x, mod, g_mix[l], *_pack_w_in(w_in[l]), d_pool, (w_out[l], w_ff1[l]))
        hm, (w_pool_b, w_ff2_b) = _mlstm(
            qt, k, vt, gates, _pack_gate_bias(b_igate[l], b_fgate[l]),
            (w_pool[l].reshape(-1, gdim), w_ff2[l]))
        x = _outffn(x, hm, o, u, mod, g_head[l], pool_scale[l], g_ffn[l], g_final,
                    w_pool_b.reshape(-1, gdim, gdim), w_out_b, w_ff1_b, w_ff2_b, final_norm=(l == depth - 1))
    return x
```

```python
import functools

import jax
import jax.numpy as jnp
import numpy as np
from jax import lax
from jax.experimental import pallas as pl
from jax.experimental.pallas import tpu as pltpu

F32 = jnp.float32
BF16 = jnp.bfloat16

LANES = 128
BF16_SUBLANES = 16
VMEM_LIMIT_BYTES = 56 * 1024 * 1024

N_HEADS = 4
HEAD_DIM = 128
CHUNK = 128
POOL_WINDOWS = (2, 4, 8, 16)
POOL_BLOCK = 128
POOL_HALO = 16
N_DIRS = 2
N_GATE_ROWS = 2 * N_DIRS
EPS = 1e-6
LOG2_E = 1.4426950408889634

ADALN_ROWS = 128
INPROJ_ROWS = 1024
OUTFFN_ROWS = 512
FF_COLS = 1024
AUG_ROWS = HEAD_DIM + BF16_SUBLANES

ROW_A, ROW_M, ROW_AINTER, ROW_EXPNEG, ROW_WK, ROW_DECAY = range(6)
N_COEF = 6
CHUNKS_PER_STEP = 32


def _const_spec(shape):
    zeros = (0,) * len(shape)
    return pl.BlockSpec(shape, lambda *_: zeros, pipeline_mode=pl.Buffered(1))


def _rms_scale(x):
    return x * lax.rsqrt(jnp.mean(x * x, axis=-1, keepdims=True) + EPS)


def _modulated_norm_bf16(x, gain, shift, scale):
    slope = (gain * (1.0 + scale)).astype(BF16)
    return _rms_scale(x).astype(BF16) * slope + shift.astype(BF16)


def _adaln_kernel(c_ref, w_ref, b_ref, o_ref):
    @pl.when(pl.program_id(0) == 0)
    def _():
        o_ref[...] = jnp.broadcast_to(b_ref[...], o_ref.shape)

    c = c_ref[...]
    act = c * jax.nn.sigmoid(c)
    o_ref[...] += jnp.dot(act.astype(BF16), w_ref[...].astype(BF16), preferred_element_type=F32)


def _adaln(c, w_ada, b_ada):
    batch, d = c.shape
    n = w_ada.shape[1]
    rows = ADALN_ROWS
    assert d % rows == 0
    return pl.pallas_call(
        _adaln_kernel,
        grid=(d // rows,),
        in_specs=[pl.BlockSpec((batch, rows), lambda j: (0, j)),
                  pl.BlockSpec((rows, n), lambda j: (j, 0)),
                  pl.BlockSpec((1, n), lambda j: (0, 0))],
        out_specs=pl.BlockSpec((batch, n), lambda j: (0, 0)),
        out_shape=jax.ShapeDtypeStruct((batch, n), F32),
        compiler_params=pltpu.CompilerParams(dimension_semantics=("arbitrary",),
                                             vmem_limit_bytes=VMEM_LIMIT_BYTES),
    )(c, w_ada, b_ada.reshape(1, n))


def _inproj_kernel(x_ref, mod_ref, g_ref, w_ref, wt_ref, *refs, n_casts):
    cast_in, cast_out = refs[:n_casts], refs[n_casts + 6:]
    qt_ref, k_ref, vt_ref, o_ref, u_ref, gates_ref = refs[n_casts:n_casts + 6]
    for src, dst in zip(cast_in, cast_out):
        dst[...] = src[...].astype(BF16)

    dm = N_HEADS * HEAD_DIM
    x = x_ref[0]
    shift = mod_ref[0, 0:1, :]
    scale = mod_ref[0, 1:2, :]
    hb = _modulated_norm_bf16(x, g_ref[...], shift, scale)

    by_token = jnp.dot(hb, w_ref[...], preferred_element_type=F32)
    for hd in range(N_HEADS):
        k_ref[0, hd] = by_token[:, hd * HEAD_DIM:(hd + 1) * HEAD_DIM].astype(BF16)
    o_ref[0] = by_token[:, dm:2 * dm].astype(BF16)
    u_ref[0] = by_token[:, 2 * dm:].astype(BF16)
    by_feature = lax.dot_general(wt_ref[...], hb, (((1,), (1,)), ((), ())), preferred_element_type=F32)
    for hd in range(N_HEADS):
        rows = slice(hd * HEAD_DIM, (hd + 1) * HEAD_DIM)
        qt_ref[0, hd] = (by_feature[rows, :] * (HEAD_DIM ** -0.5)).astype(BF16)
        vt_ref[0, hd, 0:HEAD_DIM, :] = by_feature[dm + hd * HEAD_DIM:dm + (hd + 1) * HEAD_DIM, :].astype(BF16)
        vt_ref[0, hd, HEAD_DIM:AUG_ROWS, :] = jnp.ones((AUG_ROWS - HEAD_DIM, x.shape[0]), BF16)
    gt = by_feature[2 * dm:, :]
    n_chunks = x.shape[0] // CHUNK
    for hd in range(N_HEADS):
        for r in range(N_GATE_ROWS):
            row = hd * N_GATE_ROWS + r
            for j in range(n_chunks):
                gates_ref[0, hd, r, j:j + 1, :] = gt[row:row + 1, j * CHUNK:(j + 1) * CHUNK]


def _inproj(x, mod, g_mix, w_cat, w_gates_t, d_pool, weights_to_cast):
    batch, seq, d = x.shape
    tm = INPROJ_ROWS
    dm = N_HEADS * HEAD_DIM
    grid = (batch, seq // tm)
    tiles_per_seq = seq // tm
    n_steps = batch * tiles_per_seq
    cast_specs = []
    for w in weights_to_cast:
        assert w.shape[0] % (n_steps * BF16_SUBLANES) == 0
        cast_specs.append(pl.BlockSpec((w.shape[0] // n_steps, w.shape[1]),
                                       lambda b, i: (b * tiles_per_seq + i, 0)))
    head_t = jax.ShapeDtypeStruct((batch, N_HEADS, HEAD_DIM, seq), BF16)
    head_n = jax.ShapeDtypeStruct((batch, N_HEADS, seq, HEAD_DIM), BF16)
    head_aug = jax.ShapeDtypeStruct((batch, N_HEADS, AUG_ROWS, seq), BF16)
    out_shape = (head_t, head_n, head_aug,
                 jax.ShapeDtypeStruct((batch, seq, dm), BF16),
                 jax.ShapeDtypeStruct((batch, seq, d_pool), BF16),
                 jax.ShapeDtypeStruct((batch, N_HEADS, N_GATE_ROWS, seq // CHUNK, CHUNK), F32))
    spec_t = pl.BlockSpec((1, N_HEADS, HEAD_DIM, tm), lambda b, i: (b, 0, 0, i))
    spec_n = pl.BlockSpec((1, N_HEADS, tm, HEAD_DIM), lambda b, i: (b, 0, i, 0))
    spec_aug = pl.BlockSpec((1, N_HEADS, AUG_ROWS, tm), lambda b, i: (b, 0, 0, i))
    out_specs = (spec_t, spec_n, spec_aug,
                 pl.BlockSpec((1, tm, dm), lambda b, i: (b, i, 0)),
                 pl.BlockSpec((1, tm, d_pool), lambda b, i: (b, i, 0)),
                 pl.BlockSpec((1, N_HEADS, N_GATE_ROWS, tm // CHUNK, CHUNK), lambda b, i: (b, 0, 0, i, 0)))
    outs = pl.pallas_call(
        functools.partial(_inproj_kernel, n_casts=len(weights_to_cast)),
        grid=grid,
        in_specs=[pl.BlockSpec((1, tm, d), lambda b, i: (b, i, 0)),
                  pl.BlockSpec((1,) + mod.shape[1:], lambda b, i: (b, 0, 0)),
                  _const_spec((1, d)),
                  _const_spec(w_cat.shape),
                  _const_spec(w_gates_t.shape)] + cast_specs,
        out_specs=list(out_specs) + cast_specs,
        out_shape=list(out_shape) + [jax.ShapeDtypeStruct(w.shape, BF16) for w in weights_to_cast],
        compiler_params=pltpu.CompilerParams(dimension_semantics=("arbitrary", "arbitrary"),
                                             vmem_limit_bytes=VMEM_LIMIT_BYTES),
    )(x, mod, g_mix.reshape(1, d), w_cat, w_gates_t, *weights_to_cast)
    return outs[:6], outs[6:]


def _time_scan(x, op, fill, reverse):
    row = lax.broadcasted_iota(jnp.int32, x.shape, 0)
    step = 1
    while step < CHUNK:
        if reverse:
            shifted = jnp.where(row < CHUNK - step, pltpu.roll(x, CHUNK - step, axis=0), fill)
        else:
            shifted = jnp.where(row >= step, pltpu.roll(x, step, axis=0), fill)
        x = op(x, shifted)
        step *= 2
    return x


def _chunk_carry(decay_log, peak, reverse):
    n = decay_log.shape[0]
    row = lax.broadcasted_iota(jnp.int32, decay_log.shape, 0)

    def earlier(x, dist, fill):
        if reverse:
            return jnp.where(row < n - dist, pltpu.roll(x, n - dist, axis=0), fill)
        return jnp.where(row >= dist, pltpu.roll(x, dist, axis=0), fill)

    p, q = decay_log, decay_log + peak
    dist = 1
    while dist < n:
        q = jnp.maximum(earlier(q, dist, -jnp.inf) + p, q)
        p = earlier(p, dist, 0.0) + p
        dist *= 2
    return earlier(jnp.maximum(p, q), 1, 0.0)


def _mlstm_kernel(qt_ref, k_ref, vt_ref, gates_ref, gbias_ref, *refs, n_casts, n_units):
    cast_in, out_ref, cast_out = refs[:n_casts], refs[n_casts], refs[n_casts + 1:2 * n_casts + 1]
    coef_ref, sprev_ref, st_ref = refs[2 * n_casts + 1:]
    for src, dst in zip(cast_in, cast_out):
        dst[...] = src[...].astype(BF16)

    n_chunks = gates_ref.shape[3]
    assert 2 * N_DIRS * n_chunks == LANES
    step = pl.program_id(0)

    def prologue(slot):
        log_i, log_f = [], []
        for d in range(N_DIRS):
            log_i.append(gates_ref[0, 0, 2 * d] + gbias_ref[0, 2 * d:2 * d + 1, :])
            z = gates_ref[0, 0, 2 * d + 1] + gbias_ref[0, 2 * d + 1:2 * d + 2, :]
            log_f.append(-(jnp.maximum(-z, 0.0) + jnp.log1p(jnp.exp(-jnp.abs(z)))))
        by_time = jnp.concatenate(log_f + log_i, axis=0).T
        lane = lax.broadcasted_iota(jnp.int32, by_time.shape, 1)
        fwd_lane = (lane % (N_DIRS * n_chunks)) < n_chunks
        b_t = jnp.where(fwd_lane, _time_scan(by_time, jnp.add, 0.0, False),
                        _time_scan(by_time, jnp.add, 0.0, True))
        a_t = pltpu.roll(by_time, N_DIRS * n_chunks, axis=1) - b_t
        cm_t = jnp.where(fwd_lane, _time_scan(a_t, jnp.maximum, -jnp.inf, False),
                         _time_scan(a_t, jnp.maximum, -jnp.inf, True))
        b_rows, a_rows, cm_rows = b_t.T, a_t.T, cm_t.T

        for d in range(N_DIRS):
            reverse = d == 1
            last = 0 if reverse else CHUNK - 1
            b = b_rows[d * n_chunks:(d + 1) * n_chunks]
            a = a_rows[d * n_chunks:(d + 1) * n_chunks]
            cm = cm_rows[d * n_chunks:(d + 1) * n_chunks]
            b_last = jnp.broadcast_to(b[:, last:last + 1], b.shape)
            cm_last = jnp.broadcast_to(cm[:, last:last + 1], b.shape)
            m_prev = _chunk_carry(b_last, cm_last, reverse)
            m_row = jnp.maximum(m_prev, cm)
            a_inter = jnp.exp(m_prev - m_row)
            exp_neg = jnp.exp(-(b + m_row))
            m_end = jnp.broadcast_to(m_row[:, last:last + 1], b.shape)
            wk = jnp.exp(a - m_end)
            decay = jnp.broadcast_to(a_inter[:, last:last + 1], b.shape)
            for r, val in ((ROW_A, a * LOG2_E), (ROW_M, m_row * LOG2_E), (ROW_AINTER, a_inter),
                           (ROW_EXPNEG, exp_neg), (ROW_WK, wk), (ROW_DECAY, decay)):
                coef_ref[slot, d, r] = val

    def passes(slot):
        def coef_row(d, r, c):
            return coef_ref[slot, d, r, pl.ds(c, 1), :]

        def chunk_operands(c):
            off = pl.multiple_of(c * CHUNK, CHUNK)
            return off, qt_ref[0, 0, :, pl.ds(off, CHUNK)], k_ref[0, 0, pl.ds(off, CHUNK), :], \
                vt_ref[0, 0, :, pl.ds(off, CHUNK)]

        key_pos = lax.broadcasted_iota(jnp.int32, (CHUNK, CHUNK), 0)
        qry_pos = lax.broadcasted_iota(jnp.int32, (CHUNK, CHUNK), 1)
        visible = (key_pos <= qry_pos, key_pos >= qry_pos)

        def weighted_scores(c):
            _, qt_c, k_c, _ = chunk_operands(c)
            scores = jnp.dot(k_c, qt_c, preferred_element_type=F32)
            for d in range(N_DIRS):
                a_col = jnp.broadcast_to(coef_row(d, ROW_A, c), (CHUNK, CHUNK)).T
                weight = jnp.exp2(jnp.where(visible[d], a_col - coef_row(d, ROW_M, c), -jnp.inf))
                st_ref[d, c] = (scores * weight).astype(BF16)

        def state_step(d, c, state):
            _, _, k_c, vt_c = chunk_operands(c)
            sprev_ref[d, c] = state.astype(BF16)
            vw = vt_c * coef_row(d, ROW_WK, c).astype(BF16)
            return coef_row(d, ROW_DECAY, c) * state + jnp.dot(vw, k_c, preferred_element_type=F32)

        def state_body(i, states):
            s_fwd, s_bwd = states
            for j in range(CHUNKS_PER_STEP):
                c = i * CHUNKS_PER_STEP + j
                weighted_scores(c)
                s_fwd = state_step(0, c, s_fwd)
                s_bwd = state_step(1, n_chunks - 1 - c, s_bwd)
            return s_fwd, s_bwd

        zero_state = jnp.zeros((AUG_ROWS, HEAD_DIM), F32)
        lax.fori_loop(0, n_chunks // CHUNKS_PER_STEP, state_body, (zero_state, zero_state))

        def output_chunk(c):
            off, qt_c, _, vt_c = chunk_operands(c)
            hsum = None
            for d in range(N_DIRS):
                qa = qt_c * coef_row(d, ROW_AINTER, c).astype(BF16)
                lhs = jnp.concatenate([vt_c, sprev_ref[d, c]], axis=1)
                rhs = jnp.concatenate([st_ref[d, c], qa], axis=0)
                numden = jnp.dot(lhs, rhs, preferred_element_type=F32)
                den = numden[HEAD_DIM:HEAD_DIM + 1, :]
                ht = numden[0:HEAD_DIM, :] * (1.0 / jnp.maximum(jnp.abs(den), coef_row(d, ROW_EXPNEG, c)))
                hsum = ht if hsum is None else hsum + ht
            out_ref[0, 0, pl.ds(off, CHUNK), :] = hsum.T.astype(BF16)

        def output_body(i, carry):
            for j in range(CHUNKS_PER_STEP):
                output_chunk(i * CHUNKS_PER_STEP + j)
            return carry

        lax.fori_loop(0, n_chunks // CHUNKS_PER_STEP, output_body, 0)

    @pl.when(step == 0)
    def _():
        prologue(0)

    @pl.when(step == n_units)
    def _():
        passes((n_units - 1) % 2)

    for fill in range(2):
        @pl.when((step > 0) & (step < n_units) & (step % 2 == fill))
        def _():
            passes(1 - fill)
            prologue(fill)


def _mlstm(qt, k, vt, gates, gbias, weights_to_cast):
    batch, heads, hd, seq = qt.shape
    n_chunks = seq // CHUNK
    assert n_chunks % CHUNKS_PER_STEP == 0
    n_units = batch * heads

    def pass_unit(g):
        u = jnp.maximum(g - 1, 0)
        return u // heads, u % heads, 0, 0

    def prologue_unit(g):
        return jnp.minimum(g, n_units - 1)

    spec_t = pl.BlockSpec((1, 1, hd, seq), pass_unit)
    spec_n = pl.BlockSpec((1, 1, seq, hd), pass_unit)
    spec_aug = pl.BlockSpec((1, 1, AUG_ROWS, seq), pass_unit)
    cast_specs = []
    for w in weights_to_cast:
        assert w.shape[0] % (n_units * BF16_SUBLANES) == 0
        cast_specs.append(pl.BlockSpec((w.shape[0] // n_units, w.shape[1]), lambda g: (prologue_unit(g), 0)))
    outs = pl.pallas_call(
        functools.partial(_mlstm_kernel, n_casts=len(weights_to_cast), n_units=n_units),
        grid=(n_units + 1,),
        in_specs=[spec_t, spec_n, spec_aug,
                  pl.BlockSpec((1, 1, N_GATE_ROWS, n_chunks, CHUNK),
                               lambda g: (prologue_unit(g) // heads, prologue_unit(g) % heads, 0, 0, 0)),
                  pl.BlockSpec((1, N_GATE_ROWS, CHUNK), lambda g: (prologue_unit(g) % heads, 0, 0))] + cast_specs,
        out_specs=[spec_n] + cast_specs,
        out_shape=[jax.ShapeDtypeStruct((batch, heads, seq, hd), BF16)]
        + [jax.ShapeDtypeStruct(w.shape, BF16) for w in weights_to_cast],
        scratch_shapes=[pltpu.VMEM((2, N_DIRS, N_COEF, n_chunks, CHUNK), F32),
                        pltpu.VMEM((N_DIRS, n_chunks, AUG_ROWS, hd), BF16),
                        pltpu.VMEM((N_DIRS, n_chunks, CHUNK, CHUNK), BF16)],
        compiler_params=pltpu.CompilerParams(dimension_semantics=("arbitrary",),
                                             vmem_limit_bytes=VMEM_LIMIT_BYTES),
    )(qt, k, vt, gates, gbias, *weights_to_cast)
    return outs[0], outs[1:]


def _outffn_kernel(x_ref, hm_ref, o_ref, u_ref, uprev_ref, unext_ref, mod_ref,
                   ghead_ref, pscale_ref, gffn_ref, gfinal_ref,
                   band_ref, wpool_ref, wout_ref, wff1_ref, wff2_ref, out_ref,
                   ubuf_ref, ycat_ref, *, seq, n_tiles, final_norm):
    tm = x_ref.shape[1]
    dm = N_HEADS * HEAD_DIM
    step = pl.program_id(0)
    tiles_per_seq = seq // tm
    tile = step % tiles_per_seq

    def matmul_stage(drain):
        yb = []
        for gi in range(len(POOL_WINDOWS)):
            cols = slice(gi * HEAD_DIM, (gi + 1) * HEAD_DIM)
            mapped = jnp.dot(ycat_ref[drain, :, dm + gi * HEAD_DIM:dm + (gi + 1) * HEAD_DIM], wpool_ref[gi],
                             preferred_element_type=F32)
            yb.append((mapped * pscale_ref[:, cols]).astype(BF16))
        gate1 = mod_ref[0, 2:3, :]
        yproj = (jnp.dot(ycat_ref[drain, :, 0:dm], wout_ref[0:dm, :], preferred_element_type=F32)
                 + jnp.dot(jnp.concatenate(yb, axis=1), wout_ref[dm:, :], preferred_element_type=F32))
        x1 = x_ref[0] + gate1 * yproj

        shift2 = mod_ref[0, 3:4, :]
        scale2 = mod_ref[0, 4:5, :]
        gate2 = mod_ref[0, 5:6, :]
        h2 = ((_rms_scale(x1) * gffn_ref[...]) * (1.0 + scale2) + shift2).astype(BF16)
        acc = jnp.zeros(x1.shape, F32)
        for j in range(wff1_ref.shape[1] // FF_COLS):
            hid = jnp.dot(h2, wff1_ref[:, j * FF_COLS:(j + 1) * FF_COLS], preferred_element_type=F32)
            hid = jnp.square(jnp.maximum(hid, 0.0)).astype(BF16)
            acc = acc + jnp.dot(hid, wff2_ref[j * FF_COLS:(j + 1) * FF_COLS, :], preferred_element_type=F32)
        x2 = x1 + gate2 * acc
        if final_norm:
            x2 = _rms_scale(x2) * gfinal_ref[...]
        out_ref[0] = x2

    def elementwise_stage(fill):
        for hd in range(N_HEADS):
            cols = slice(hd * HEAD_DIM, (hd + 1) * HEAD_DIM)
            hn = _rms_scale(hm_ref[0, hd].astype(F32)) * ghead_ref[:, cols]
            gate = jax.nn.sigmoid(o_ref[0, :, cols].astype(F32))
            ycat_ref[fill, :, cols] = (gate * hn).astype(BF16)

        ubuf_ref[0:POOL_HALO, :] = jnp.where(tile > 0, uprev_ref[0], jnp.zeros_like(uprev_ref[0]))
        ubuf_ref[POOL_HALO:POOL_HALO + tm, :] = u_ref[0]
        ubuf_ref[POOL_HALO + tm:, :] = jnp.where(tile < tiles_per_seq - 1, unext_ref[0],
                                                 jnp.zeros_like(unext_ref[0]))
        for gi in range(len(POOL_WINDOWS)):
            cols = slice(gi * HEAD_DIM, (gi + 1) * HEAD_DIM)
            for r0 in range(0, tm, POOL_BLOCK):
                ext = ubuf_ref[r0:r0 + POOL_BLOCK + 2 * POOL_HALO, cols]
                mixed = jnp.dot(band_ref[gi], ext, preferred_element_type=F32)
                ycat_ref[fill, r0:r0 + POOL_BLOCK, dm + gi * HEAD_DIM:dm + (gi + 1) * HEAD_DIM] = mixed.astype(BF16)

    def sequence_end_rows(fill):
        def clipped_rows(r0):
            ext = ubuf_ref[r0:r0 + 3 * POOL_HALO, :].astype(F32)
            pos = tile * tm + r0 + lax.broadcasted_iota(jnp.int32, (POOL_HALO, HEAD_DIM), 0)
            for gi, win in enumerate(POOL_WINDOWS):
                half = win // 2
                grp = ext[:, gi * HEAD_DIM:(gi + 1) * HEAD_DIM]
                total = grp[POOL_HALO - half:2 * POOL_HALO - half]
                for j in range(1 - half, half):
                    total = total + grp[POOL_HALO + j:2 * POOL_HALO + j]
                cnt = jnp.minimum(pos + half, seq) - jnp.maximum(pos - half, 0)
                mixed = total / cnt.astype(F32) - grp[POOL_HALO:2 * POOL_HALO]
                ycat_ref[fill, r0:r0 + POOL_HALO, dm + gi * HEAD_DIM:dm + (gi + 1) * HEAD_DIM] = mixed.astype(BF16)

        @pl.when(tile == 0)
        def _():
            clipped_rows(0)

        @pl.when(tile == tiles_per_seq - 1)
        def _():
            clipped_rows(tm - POOL_HALO)

    @pl.when(step == 0)
    def _():
        elementwise_stage(0)
        sequence_end_rows(0)

    @pl.when(step == n_tiles)
    def _():
        matmul_stage((n_tiles - 1) % 2)

    for fill in range(2):
        @pl.when((step > 0) & (step < n_tiles) & (step % 2 == fill))
        def _():
            matmul_stage(1 - fill)
            elementwise_stage(fill)
            sequence_end_rows(fill)


def _pool_bands():
    row = np.arange(POOL_BLOCK)[:, None]
    col = np.arange(POOL_BLOCK + 2 * POOL_HALO)[None, :] - POOL_HALO
    bands = [((col >= row - win // 2) & (col < row + win // 2)) / win - (col == row) for win in POOL_WINDOWS]
    return jnp.asarray(np.stack(bands), dtype=BF16)


def _outffn(x, hm, o, u, mod, g_head, pool_scale, g_ffn, g_final, w_pool, w_out, w_ff1, w_ff2,
            final_norm):
    batch, seq, d = x.shape
    tm = OUTFFN_ROWS
    dm = N_HEADS * HEAD_DIM
    d_pool = u.shape[2]
    halo_per_tile = tm // POOL_HALO
    n_halo_blocks = seq // POOL_HALO
    tiles_per_seq = seq // tm
    n_tiles = batch * tiles_per_seq

    def mm_tile(g):
        t = jnp.maximum(g - 1, 0)
        return t // tiles_per_seq, t % tiles_per_seq

    def ew_tile(g):
        t = jnp.minimum(g, n_tiles - 1)
        return t // tiles_per_seq, t % tiles_per_seq

    def mm_rows(g):
        b, i = mm_tile(g)
        return b, i, 0

    def ew_rows(g):
        b, i = ew_tile(g)
        return b, i, 0

    def ew_heads(g):
        b, i = ew_tile(g)
        return b, 0, i, 0

    def ew_prev(g):
        b, i = ew_tile(g)
        return b, jnp.maximum(i * halo_per_tile - 1, 0), 0

    def ew_next(g):
        b, i = ew_tile(g)
        return b, jnp.minimum((i + 1) * halo_per_tile, n_halo_blocks - 1), 0

    bands = _pool_bands()
    kernel = functools.partial(_outffn_kernel, seq=seq, n_tiles=n_tiles, final_norm=final_norm)
    return pl.pallas_call(
        kernel,
        grid=(n_tiles + 1,),
        in_specs=[pl.BlockSpec((1, tm, d), mm_rows),
                  pl.BlockSpec((1, N_HEADS, tm, HEAD_DIM), ew_heads),
                  pl.BlockSpec((1, tm, dm), ew_rows),
                  pl.BlockSpec((1, tm, d_pool), ew_rows),
                  pl.BlockSpec((1, POOL_HALO, d_pool), ew_prev),
                  pl.BlockSpec((1, POOL_HALO, d_pool), ew_next),
                  pl.BlockSpec((1,) + mod.shape[1:], lambda g: (mm_tile(g)[0], 0, 0)),
                  _const_spec((1, dm)),
                  _const_spec((1, d_pool)),
                  _const_spec((1, d)),
                  _const_spec((1, d)),
                  _const_spec(bands.shape),
                  _const_spec(w_pool.shape),
                  _const_spec(w_out.shape),
                  _const_spec(w_ff1.shape),
                  _const_spec(w_ff2.shape)],
        out_specs=pl.BlockSpec((1, tm, d), mm_rows),
        out_shape=jax.ShapeDtypeStruct((batch, seq, d), F32),
        scratch_shapes=[pltpu.VMEM((tm + 2 * POOL_HALO, d_pool), BF16),
                        pltpu.VMEM((2, tm, dm + d_pool), BF16)],
        compiler_params=pltpu.CompilerParams(dimension_semantics=("arbitrary",),
                                             vmem_limit_bytes=VMEM_LIMIT_BYTES),
    )(x, hm, o, u, u, u, mod, g_head.reshape(1, dm), pool_scale.reshape(1, d_pool),
      g_ffn.reshape(1, d), g_final.reshape(1, d), bands, w_pool, w_out, w_ff1, w_ff2)


def _pack_w_in(w_in):
    dm = N_HEADS * HEAD_DIM
    n_gates = N_DIRS * 2 * N_HEADS
    q, k, v, o = (w_in[:, i * dm:(i + 1) * dm] for i in range(4))
    gates = w_in[:, 4 * dm:4 * dm + n_gates]
    pool = w_in[:, 4 * dm + n_gates:]
    gates_t = gates.reshape(-1, N_DIRS, 2, N_HEADS).transpose(3, 1, 2, 0).reshape(n_gates, -1)
    cols = jnp.concatenate([k, o, pool], axis=1).astype(BF16)
    rows = jnp.concatenate([q.T, v.T, gates_t], axis=0).astype(BF16)
    return cols, rows


def _pack_gate_bias(b_igate, b_fgate):
    rows = jnp.stack([b_igate, b_fgate], axis=1)
    rows = rows.transpose(2, 0, 1).reshape(N_HEADS, N_GATE_ROWS)
    return jnp.broadcast_to(rows[:, :, None], (N_HEADS, N_GATE_ROWS, CHUNK)).astype(F32)


def kernel(x, c, w_ada, b_ada, g_mix, w_in, b_igate, b_fgate, g_head, w_pool, pool_scale, w_out,
           g_ffn, w_ff1, w_ff2, g_final):
    depth = w_ada.shape[0]
    batch, seq, d = x.shape
    d_pool = pool_scale.shape[1]
    assert g_head.shape[1] == N_HEADS * HEAD_DIM and w_pool.shape[1] == len(POOL_WINDOWS)
    assert seq % INPROJ_ROWS == 0 and seq % OUTFFN_ROWS == 0 and w_ff1.shape[2] % FF_COLS == 0
    for l in range(depth):
        mod = _adaln(c, w_ada[l], b_ada[l]).reshape(batch, 6, d)
        gdim = w_pool.shape[2]
        (qt, k, vt, o, u, gates), (w_out_b, w_ff1_b) = _inproj(
            x, mod, g_mix[l], *_pack_w_in(w_in[l]), d_pool, (w_out[l], w_ff1[l]))
        hm, (w_pool_b, w_ff2_b) = _mlstm(
            qt, k, vt, gates, _pack_gate_bias(b_igate[l], b_fgate[l]),
            (w_pool[l].reshape(-1, gdim), w_ff2[l]))
        x = _outffn(x, hm, o, u, mod, g_head[l], pool_scale[l], g_ffn[l], g_final,
                    w_pool_b.reshape(-1, gdim, gdim), w_out_b, w_ff1_b, w_ff2_b, final_norm=(l == depth - 1))
    return x
```

```python
import functools

import jax
import jax.numpy as jnp
import numpy as np
from jax import lax
from jax.experimental import pallas as pl
from jax.experimental.pallas import tpu as pltpu

F32 = jnp.float32
BF16 = jnp.bfloat16

LANES = 128
BF16_SUBLANES = 16
VMEM_LIMIT_BYTES = 56 * 1024 * 1024

N_HEADS = 4
HEAD_DIM = 128
CHUNK = 128
POOL_WINDOWS = (2, 4, 8, 16)
POOL_BLOCK = 128
POOL_HALO = 16
N_DIRS = 2
N_GATE_ROWS = 2 * N_DIRS
EPS = 1e-6
LOG2_E = 1.4426950408889634

ADALN_ROWS = 128
INPROJ_ROWS = 1024
OUTFFN_ROWS = 512
FF_COLS = 1024
AUG_ROWS = HEAD_DIM + BF16_SUBLANES

ROW_A, ROW_M, ROW_AINTER, ROW_EXPNEG, ROW_WK, ROW_DECAY = range(6)
N_COEF = 6
CHUNKS_PER_STEP = 32


def _const_spec(shape):
    zeros = (0,) * len(shape)
    return pl.BlockSpec(shape, lambda *_: zeros, pipeline_mode=pl.Buffered(1))


def _rms_scale(x):
    return x * lax.rsqrt(jnp.mean(x * x, axis=-1, keepdims=True) + EPS)


def _modulated_norm_bf16(x, gain, shift, scale):
    slope = (gain * (1.0 + scale)).astype(BF16)
    return _rms_scale(x).astype(BF16) * slope + shift.astype(BF16)


def _adaln_kernel(c_ref, w_ref, b_ref, o_ref):
    @pl.when(pl.program_id(0) == 0)
    def _():
        o_ref[...] = jnp.broadcast_to(b_ref[...], o_ref.shape)

    c = c_ref[...]
    act = c * jax.nn.sigmoid(c)
    o_ref[...] += jnp.dot(act.astype(BF16), w_ref[...].astype(BF16), preferred_element_type=F32)


def _adaln(c, w_ada, b_ada):
    batch, d = c.shape
    n = w_ada.shape[1]
    rows = ADALN_ROWS
    assert d % rows == 0
    return pl.pallas_call(
        _adaln_kernel,
        grid=(d // rows,),
        in_specs=[pl.BlockSpec((batch, rows), lambda j: (0, j)),
                  pl.BlockSpec((rows, n), lambda j: (j, 0)),
                  pl.BlockSpec((1, n), lambda j: (0, 0))],
        out_specs=pl.BlockSpec((batch, n), lambda j: (0, 0)),
        out_shape=jax.ShapeDtypeStruct((batch, n), F32),
        compiler_params=pltpu.CompilerParams(dimension_semantics=("arbitrary",),
                                             vmem_limit_bytes=VMEM_LIMIT_BYTES),
    )(c, w_ada, b_ada.reshape(1, n))


def _inproj_kernel(x_ref, mod_ref, g_ref, w_ref, wt_ref, *refs, n_casts):
    cast_in, cast_out = refs[:n_casts], refs[n_casts + 6:]
    qt_ref, k_ref, vt_ref, o_ref, u_ref, gates_ref = refs[n_casts:n_casts + 6]
    for src, dst in zip(cast_in, cast_out):
        dst[...] = src[...].astype(BF16)

    dm = N_HEADS * HEAD_DIM
    x = x_ref[0]
    shift = mod_ref[0, 0:1, :]
    scale = mod_ref[0, 1:2, :]
    hb = _modulated_norm_bf16(x, g_ref[...], shift, scale)

    by_token = jnp.dot(hb, w_ref[...], preferred_element_type=F32)
    for hd in range(N_HEADS):
        k_ref[0, hd] = by_token[:, hd * HEAD_DIM:(hd + 1) * HEAD_DIM].astype(BF16)
    o_ref[0] = by_token[:, dm:2 * dm].astype(BF16)
    u_ref[0] = by_token[:, 2 * dm:].astype(BF16)
    by_feature = lax.dot_general(wt_ref[...], hb, (((1,), (1,)), ((), ())), preferred_element_type=F32)
    for hd in range(N_HEADS):
        rows = slice(hd * HEAD_DIM, (hd + 1) * HEAD_DIM)
        qt_ref[0, hd] = (by_feature[rows, :] * (HEAD_DIM ** -0.5)).astype(BF16)
        vt_ref[0, hd, 0:HEAD_DIM, :] = by_feature[dm + hd * HEAD_DIM:dm + (hd + 1) * HEAD_DIM, :].astype(BF16)
        vt_ref[0, hd, HEAD_DIM:AUG_ROWS, :] = jnp.ones((AUG_ROWS - HEAD_DIM, x.shape[0]), BF16)
    gt = by_feature[2 * dm:, :]
    n_chunks = x.shape[0] // CHUNK
    for hd in range(N_HEADS):
        for r in range(N_GATE_ROWS):
            row = hd * N_GATE_ROWS + r
            for j in range(n_chunks):
                gates_ref[0, hd, r, j:j + 1, :] = gt[row:row + 1, j * CHUNK:(j + 1) * CHUNK]


def _inproj(x, mod, g_mix, w_cat, w_gates_t, d_pool, weights_to_cast):
    batch, seq, d = x.shape
    tm = INPROJ_ROWS
    dm = N_HEADS * HEAD_DIM
    grid = (batch, seq // tm)
    tiles_per_seq = seq // tm
    n_steps = batch * tiles_per_seq
    cast_specs = []
    for w in weights_to_cast:
        assert w.shape[0] % (n_steps * BF16_SUBLANES) == 0
        cast_specs.append(pl.BlockSpec((w.shape[0] // n_steps, w.shape[1]),
                                       lambda b, i: (b * tiles_per_seq + i, 0)))
    head_t = jax.ShapeDtypeStruct((batch, N_HEADS, HEAD_DIM, seq), BF16)
    head_n = jax.ShapeDtypeStruct((batch, N_HEADS, seq, HEAD_DIM), BF16)
    head_aug = jax.ShapeDtypeStruct((batch, N_HEADS, AUG_ROWS, seq), BF16)
    out_shape = (head_t, head_n, head_aug,
                 jax.ShapeDtypeStruct((batch, seq, dm), BF16),
                 jax.ShapeDtypeStruct((batch, seq, d_pool), BF16),
                 jax.ShapeDtypeStruct((batch, N_HEADS, N_GATE_ROWS, seq // CHUNK, CHUNK), F32))
    spec_t = pl.BlockSpec((1, N_HEADS, HEAD_DIM, tm), lambda b, i: (b, 0, 0, i))
    spec_n = pl.BlockSpec((1, N_HEADS, tm, HEAD_DIM), lambda b, i: (b, 0, i, 0))
    spec_aug = pl.BlockSpec((1, N_HEADS, AUG_ROWS, tm), lambda b, i: (b, 0, 0, i))
    out_specs = (spec_t, spec_n, spec_aug,
                 pl.BlockSpec((1, tm, dm), lambda b, i: (b, i, 0)),
                 pl.BlockSpec((1, tm, d_pool), lambda b, i: (b, i, 0)),
                 pl.BlockSpec((1, N_HEADS, N_GATE_ROWS, tm // CHUNK, CHUNK), lambda b, i: (b, 0, 0, i, 0)))
    outs = pl.pallas_call(
        functools.partial(_inproj_kernel, n_casts=len(weights_to_cast)),
        grid=grid,
        in_specs=[pl.BlockSpec((1, tm, d), lambda b, i: (b, i, 0)),
                  pl.BlockSpec((1,) + mod.shape[1:], lambda b, i: (b, 0, 0)),
                  _const_spec((1, d)),
                  _const_spec(w_cat.shape),
                  _const_spec(w_gates_t.shape)] + cast_specs,
        out_specs=list(out_specs) + cast_specs,
        out_shape=list(out_shape) + [jax.ShapeDtypeStruct(w.shape, BF16) for w in weights_to_cast],
        compiler_params=pltpu.CompilerParams(dimension_semantics=("arbitrary", "arbitrary"),
                                             vmem_limit_bytes=VMEM_LIMIT_BYTES),
    )(x, mod, g_mix.reshape(1, d), w_cat, w_gates_t, *weights_to_cast)
    return outs[:6], outs[6:]


def _time_scan(x, op, fill, reverse):
    row = lax.broadcasted_iota(jnp.int32, x.shape, 0)
    step = 1
    while step < CHUNK:
        if reverse:
            shifted = jnp.where(row < CHUNK - step, pltpu.roll(x, CHUNK - step, axis=0), fill)
        else:
            shifted = jnp.where(row >= step, pltpu.roll(x, step, axis=0), fill)
        x = op(x, shifted)
        step *= 2
    return x


def _chunk_carry(decay_log, peak, reverse):
    n = decay_log.shape[0]
    row = lax.broadcasted_iota(jnp.int32, decay_log.shape, 0)

    def earlier(x, dist, fill):
        if reverse:
            return jnp.where(row < n - dist, pltpu.roll(x, n - dist, axis=0), fill)
        return jnp.where(row >= dist, pltpu.roll(x, dist, axis=0), fill)

    p, q = decay_log, decay_log + peak
    dist = 1
    while dist < n:
        q = jnp.maximum(earlier(q, dist, -jnp.inf) + p, q)
        p = earlier(p, dist, 0.0) + p
        dist *= 2
    return earlier(jnp.maximum(p, q), 1, 0.0)


def _mlstm_kernel(qt_ref, k_ref, vt_ref, gates_ref, gbias_ref, wpool_ref, pscale_ref, woutp_ref, *refs,
                  n_casts, n_units):
    cast_in, out_ref, fold_ref = refs[:n_casts], refs[n_casts], refs[n_casts + 1]
    cast_out = refs[n_casts + 2:2 * n_casts + 2]
    coef_ref, sprev_ref, st_ref = refs[2 * n_casts + 2:]

    def weight_slabs():
        for src, dst in zip(cast_in, cast_out):
            dst[...] = src[...].astype(BF16)
        lhs = wpool_ref[...] * pscale_ref[...]
        rhs = woutp_ref[...]
        lhs_hi, rhs_hi = lhs.astype(BF16), rhs.astype(BF16)
        lhs_lo = (lhs - lhs_hi.astype(F32)).astype(BF16)
        rhs_lo = (rhs - rhs_hi.astype(F32)).astype(BF16)
        fold_ref[...] = (jnp.dot(lhs_hi, rhs_hi, preferred_element_type=F32)
                         + jnp.dot(lhs_hi, rhs_lo, preferred_element_type=F32)
                         + jnp.dot(lhs_lo, rhs_hi, preferred_element_type=F32)).astype(BF16)

    n_chunks = gates_ref.shape[3]
    assert 2 * N_DIRS * n_chunks == LANES
    step = pl.program_id(0)

    def prologue(slot):
        log_i, log_f = [], []
        for d in range(N_DIRS):
            log_i.append(gates_ref[0, 0, 2 * d] + gbias_ref[0, 2 * d:2 * d + 1, :])
            z = gates_ref[0, 0, 2 * d + 1] + gbias_ref[0, 2 * d + 1:2 * d + 2, :]
            log_f.append(-(jnp.maximum(-z, 0.0) + jnp.log1p(jnp.exp(-jnp.abs(z)))))
        by_time = jnp.concatenate(log_f + log_i, axis=0).T
        lane = lax.broadcasted_iota(jnp.int32, by_time.shape, 1)
        fwd_lane = (lane % (N_DIRS * n_chunks)) < n_chunks
        b_t = jnp.where(fwd_lane, _time_scan(by_time, jnp.add, 0.0, False),
                        _time_scan(by_time, jnp.add, 0.0, True))
        a_t = pltpu.roll(by_time, N_DIRS * n_chunks, axis=1) - b_t
        cm_t = jnp.where(fwd_lane, _time_scan(a_t, jnp.maximum, -jnp.inf, False),
                         _time_scan(a_t, jnp.maximum, -jnp.inf, True))
        b_rows, a_rows, cm_rows = b_t.T, a_t.T, cm_t.T

        for d in range(N_DIRS):
            reverse = d == 1
            last = 0 if reverse else CHUNK - 1
            b = b_rows[d * n_chunks:(d + 1) * n_chunks]
            a = a_rows[d * n_chunks:(d + 1) * n_chunks]
            cm = cm_rows[d * n_chunks:(d + 1) * n_chunks]
            b_last = jnp.broadcast_to(b[:, last:last + 1], b.shape)
            cm_last = jnp.broadcast_to(cm[:, last:last + 1], b.shape)
            m_prev = _chunk_carry(b_last, cm_last, reverse)
            m_row = jnp.maximum(m_prev, cm)
            a_inter = jnp.exp(m_prev - m_row)
            exp_neg = jnp.exp(-(b + m_row))
            m_end = jnp.broadcast_to(m_row[:, last:last + 1], b.shape)
            wk = jnp.exp(a - m_end)
            decay = jnp.broadcast_to(a_inter[:, last:last + 1], b.shape)
            for r, val in ((ROW_A, a * LOG2_E), (ROW_M, m_row * LOG2_E), (ROW_AINTER, a_inter),
                           (ROW_EXPNEG, exp_neg), (ROW_WK, wk), (ROW_DECAY, decay)):
                coef_ref[slot, d, r] = val

    def passes(slot):
        def coef_row(d, r, c):
            return coef_ref[slot, d, r, pl.ds(c, 1), :]

        def chunk_operands(c):
            off = pl.multiple_of(c * CHUNK, CHUNK)
            return off, qt_ref[0, 0, :, pl.ds(off, CHUNK)], k_ref[0, 0, pl.ds(off, CHUNK), :], \
                vt_ref[0, 0, :, pl.ds(off, CHUNK)]

        key_pos = lax.broadcasted_iota(jnp.int32, (CHUNK, CHUNK), 0)
        qry_pos = lax.broadcasted_iota(jnp.int32, (CHUNK, CHUNK), 1)
        visible = (key_pos <= qry_pos, key_pos >= qry_pos)

        def weighted_scores(c):
            _, qt_c, k_c, _ = chunk_operands(c)
            scores = jnp.dot(k_c, qt_c, preferred_element_type=F32)
            for d in range(N_DIRS):
                a_col = jnp.broadcast_to(coef_row(d, ROW_A, c), (CHUNK, CHUNK)).T
                weight = jnp.exp2(jnp.where(visible[d], a_col - coef_row(d, ROW_M, c), -jnp.inf))
                st_ref[d, c] = (scores * weight).astype(BF16)

        def state_step(d, c, state):
            _, _, k_c, vt_c = chunk_operands(c)
            sprev_ref[d, c] = state.astype(BF16)
            vw = vt_c * coef_row(d, ROW_WK, c).astype(BF16)
            return coef_row(d, ROW_DECAY, c) * state + jnp.dot(vw, k_c, preferred_element_type=F32)

        def state_body(i, states):
            s_fwd, s_bwd = states
            for j in range(CHUNKS_PER_STEP):
                c = i * CHUNKS_PER_STEP + j
                weighted_scores(c)
                s_fwd = state_step(0, c, s_fwd)
                s_bwd = state_step(1, n_chunks - 1 - c, s_bwd)
            return s_fwd, s_bwd

        zero_state = jnp.zeros((AUG_ROWS, HEAD_DIM), F32)
        lax.fori_loop(0, n_chunks // CHUNKS_PER_STEP, state_body, (zero_state, zero_state))

        def output_chunk(c):
            off, qt_c, _, vt_c = chunk_operands(c)
            hsum = None
            for d in range(N_DIRS):
                qa = qt_c * coef_row(d, ROW_AINTER, c).astype(BF16)
                lhs = jnp.concatenate([vt_c, sprev_ref[d, c]], axis=1)
                rhs = jnp.concatenate([st_ref[d, c], qa], axis=0)
                numden = jnp.dot(lhs, rhs, preferred_element_type=F32)
                den = numden[HEAD_DIM:HEAD_DIM + 1, :]
                ht = numden[0:HEAD_DIM, :] * (1.0 / jnp.maximum(jnp.abs(den), coef_row(d, ROW_EXPNEG, c)))
                hsum = ht if hsum is None else hsum + ht
            out_ref[0, 0, pl.ds(off, CHUNK), :] = hsum.T.astype(BF16)

        def output_body(i, carry):
            for j in range(CHUNKS_PER_STEP):
                output_chunk(i * CHUNKS_PER_STEP + j)
            return carry

        lax.fori_loop(0, n_chunks // CHUNKS_PER_STEP, output_body, 0)

    @pl.when(step == 0)
    def _():
        prologue(0)
        weight_slabs()

    @pl.when(step == n_units)
    def _():
        passes((n_units - 1) % 2)
        weight_slabs()

    for fill in range(2):
        @pl.when((step > 0) & (step < n_units) & (step % 2 == fill))
        def _():
            passes(1 - fill)
            prologue(fill)
            weight_slabs()


def _mlstm(qt, k, vt, gates, gbias, w_pool, pool_scale, w_out, weights_to_cast):
    batch, heads, hd, seq = qt.shape
    n_chunks = seq // CHUNK
    assert n_chunks % CHUNKS_PER_STEP == 0
    n_units = batch * heads
    groups, gdim, _ = w_pool.shape
    d = w_out.shape[1]
    first_pool_block = (w_out.shape[0] - groups * gdim) // gdim
    units_per_group = n_units // groups
    fold_rows = gdim // units_per_group
    assert n_units % groups == 0 and fold_rows % BF16_SUBLANES == 0

    def pass_unit(g):
        u = jnp.maximum(g - 1, 0)
        return u // heads, u % heads, 0, 0

    def prologue_unit(g):
        return jnp.minimum(g, n_units - 1)

    spec_t = pl.BlockSpec((1, 1, hd, seq), pass_unit)
    spec_n = pl.BlockSpec((1, 1, seq, hd), pass_unit)
    spec_aug = pl.BlockSpec((1, 1, AUG_ROWS, seq), pass_unit)
    cast_specs = []
    for w in weights_to_cast:
        assert w.shape[0] % (n_units * BF16_SUBLANES) == 0
        cast_specs.append(pl.BlockSpec((w.shape[0] // n_units, w.shape[1]), lambda g: (prologue_unit(g), 0)))
    outs = pl.pallas_call(
        functools.partial(_mlstm_kernel, n_casts=len(weights_to_cast), n_units=n_units),
        grid=(n_units + 1,),
        in_specs=[spec_t, spec_n, spec_aug,
                  pl.BlockSpec((1, 1, N_GATE_ROWS, n_chunks, CHUNK),
                               lambda g: (prologue_unit(g) // heads, prologue_unit(g) % heads, 0, 0, 0)),
                  pl.BlockSpec((1, N_GATE_ROWS, CHUNK), lambda g: (prologue_unit(g) % heads, 0, 0)),
                  pl.BlockSpec((fold_rows, gdim), lambda g: (prologue_unit(g), 0)),
                  pl.BlockSpec((1, gdim), lambda g: (0, prologue_unit(g) // units_per_group)),
                  pl.BlockSpec((gdim, d), lambda g: (first_pool_block + prologue_unit(g) // units_per_group, 0))]
        + cast_specs,
        out_specs=[spec_n, pl.BlockSpec((fold_rows, d), lambda g: (prologue_unit(g), 0))] + cast_specs,
        out_shape=[jax.ShapeDtypeStruct((batch, heads, seq, hd), BF16),
                   jax.ShapeDtypeStruct((groups * gdim, d), BF16)]
        + [jax.ShapeDtypeStruct(w.shape, BF16) for w in weights_to_cast],
        scratch_shapes=[pltpu.VMEM((2, N_DIRS, N_COEF, n_chunks, CHUNK), F32),
                        pltpu.VMEM((N_DIRS, n_chunks, AUG_ROWS, hd), BF16),
                        pltpu.VMEM((N_DIRS, n_chunks, CHUNK, CHUNK), BF16)],
        compiler_params=pltpu.CompilerParams(dimension_semantics=("arbitrary",),
                                             vmem_limit_bytes=VMEM_LIMIT_BYTES),
    )(qt, k, vt, gates, gbias, w_pool.reshape(groups * gdim, gdim), pool_scale.reshape(1, groups * gdim), w_out,
      *weights_to_cast)
    return outs[0], outs[1], outs[2:]


def _outffn_kernel(x_ref, hm_ref, o_ref, u_ref, uprev_ref, unext_ref, mod_ref,
                   ghead_ref, gffn_ref, gfinal_ref,
                   band_ref, wout_ref, wfold_ref, wff1_ref, wff2_ref, out_ref,
                   ubuf_ref, ycat_ref, *, seq, n_tiles, final_norm):
    tm = x_ref.shape[1]
    dm = N_HEADS * HEAD_DIM
    step = pl.program_id(0)
    tiles_per_seq = seq // tm
    tile = step % tiles_per_seq

    def matmul_stage(drain):
        gate1 = mod_ref[0, 2:3, :]
        yproj = (jnp.dot(ycat_ref[drain, :, 0:dm], wout_ref[0:dm, :], preferred_element_type=F32)
                 + jnp.dot(ycat_ref[drain, :, dm:], wfold_ref[...], preferred_element_type=F32))
        x1 = x_ref[0] + gate1 * yproj

        shift2 = mod_ref[0, 3:4, :]
        scale2 = mod_ref[0, 4:5, :]
        gate2 = mod_ref[0, 5:6, :]
        h2 = ((_rms_scale(x1) * gffn_ref[...]) * (1.0 + scale2) + shift2).astype(BF16)
        acc = jnp.zeros(x1.shape, F32)
        for j in range(wff1_ref.shape[1] // FF_COLS):
            hid = jnp.dot(h2, wff1_ref[:, j * FF_COLS:(j + 1) * FF_COLS], preferred_element_type=F32)
            hid = jnp.square(jnp.maximum(hid, 0.0)).astype(BF16)
            acc = acc + jnp.dot(hid, wff2_ref[j * FF_COLS:(j + 1) * FF_COLS, :], preferred_element_type=F32)
        x2 = x1 + gate2 * acc
        if final_norm:
            x2 = _rms_scale(x2) * gfinal_ref[...]
        out_ref[0] = x2

    def elementwise_stage(fill):
        for hd in range(N_HEADS):
            cols = slice(hd * HEAD_DIM, (hd + 1) * HEAD_DIM)
            hn = _rms_scale(hm_ref[0, hd].astype(F32)) * ghead_ref[:, cols]
            gate = jax.nn.sigmoid(o_ref[0, :, cols].astype(F32))
            ycat_ref[fill, :, cols] = (gate * hn).astype(BF16)

        ubuf_ref[0:POOL_HALO, :] = jnp.where(tile > 0, uprev_ref[0], jnp.zeros_like(uprev_ref[0]))
        ubuf_ref[POOL_HALO:POOL_HALO + tm, :] = u_ref[0]
        ubuf_ref[POOL_HALO + tm:, :] = jnp.where(tile < tiles_per_seq - 1, unext_ref[0],
                                                 jnp.zeros_like(unext_ref[0]))
        for gi in range(len(POOL_WINDOWS)):
            cols = slice(gi * HEAD_DIM, (gi + 1) * HEAD_DIM)
            for r0 in range(0, tm, POOL_BLOCK):
                ext = ubuf_ref[r0:r0 + POOL_BLOCK + 2 * POOL_HALO, cols]
                mixed = jnp.dot(band_ref[gi], ext, preferred_element_type=F32)
                ycat_ref[fill, r0:r0 + POOL_BLOCK, dm + gi * HEAD_DIM:dm + (gi + 1) * HEAD_DIM] = mixed.astype(BF16)

    def sequence_end_rows(fill):
        def clipped_rows(r0):
            ext = ubuf_ref[r0:r0 + 3 * POOL_HALO, :].astype(F32)
            pos = tile * tm + r0 + lax.broadcasted_iota(jnp.int32, (POOL_HALO, HEAD_DIM), 0)
            for gi, win in enumerate(POOL_WINDOWS):
                half = win // 2
                grp = ext[:, gi * HEAD_DIM:(gi + 1) * HEAD_DIM]
                total = grp[POOL_HALO - half:2 * POOL_HALO - half]
                for j in range(1 - half, half):
                    total = total + grp[POOL_HALO + j:2 * POOL_HALO + j]
                cnt = jnp.minimum(pos + half, seq) - jnp.maximum(pos - half, 0)
                mixed = total / cnt.astype(F32) - grp[POOL_HALO:2 * POOL_HALO]
                ycat_ref[fill, r0:r0 + POOL_HALO, dm + gi * HEAD_DIM:dm + (gi + 1) * HEAD_DIM] = mixed.astype(BF16)

        @pl.when(tile == 0)
        def _():
            clipped_rows(0)

        @pl.when(tile == tiles_per_seq - 1)
        def _():
            clipped_rows(tm - POOL_HALO)

    @pl.when(step == 0)
    def _():
        elementwise_stage(0)
        sequence_end_rows(0)

    @pl.when(step == n_tiles)
    def _():
        matmul_stage((n_tiles - 1) % 2)

    for fill in range(2):
        @pl.when((step > 0) & (step < n_tiles) & (step % 2 == fill))
        def _():
            matmul_stage(1 - fill)
            elementwise_stage(fill)
            sequence_end_rows(fill)


def _pool_bands():
    row = np.arange(POOL_BLOCK)[:, None]
    col = np.arange(POOL_BLOCK + 2 * POOL_HALO)[None, :] - POOL_HALO
    bands = [((col >= row - win // 2) & (col < row + win // 2)) / win - (col == row) for win in POOL_WINDOWS]
    return jnp.asarray(np.stack(bands), dtype=BF16)


def _outffn(x, hm, o, u, mod, g_head, g_ffn, g_final, w_out, w_fold, w_ff1, w_ff2, final_norm):
    batch, seq, d = x.shape
    tm = OUTFFN_ROWS
    dm = N_HEADS * HEAD_DIM
    d_pool = u.shape[2]
    halo_per_tile = tm // POOL_HALO
    n_halo_blocks = seq // POOL_HALO
    tiles_per_seq = seq // tm
    n_tiles = batch * tiles_per_seq

    def mm_tile(g):
        t = jnp.maximum(g - 1, 0)
        return t // tiles_per_seq, t % tiles_per_seq

    def ew_tile(g):
        t = jnp.minimum(g, n_tiles - 1)
        return t // tiles_per_seq, t % tiles_per_seq

    def mm_rows(g):
        b, i = mm_tile(g)
        return b, i, 0

    def ew_rows(g):
        b, i = ew_tile(g)
        return b, i, 0

    def ew_heads(g):
        b, i = ew_tile(g)
        return b, 0, i, 0

    def ew_prev(g):
        b, i = ew_tile(g)
        return b, jnp.maximum(i * halo_per_tile - 1, 0), 0

    def ew_next(g):
        b, i = ew_tile(g)
        return b, jnp.minimum((i + 1) * halo_per_tile, n_halo_blocks - 1), 0

    bands = _pool_bands()
    kernel = functools.partial(_outffn_kernel, seq=seq, n_tiles=n_tiles, final_norm=final_norm)
    return pl.pallas_call(
        kernel,
        grid=(n_tiles + 1,),
        in_specs=[pl.BlockSpec((1, tm, d), mm_rows),
                  pl.BlockSpec((1, N_HEADS, tm, HEAD_DIM), ew_heads),
                  pl.BlockSpec((1, tm, dm), ew_rows),
                  pl.BlockSpec((1, tm, d_pool), ew_rows),
                  pl.BlockSpec((1, POOL_HALO, d_pool), ew_prev),
                  pl.BlockSpec((1, POOL_HALO, d_pool), ew_next),
                  pl.BlockSpec((1,) + mod.shape[1:], lambda g: (mm_tile(g)[0], 0, 0)),
                  _const_spec((1, dm)),
                  _const_spec((1, d)),
                  _const_spec((1, d)),
                  _const_spec(bands.shape),
                  _const_spec(w_out.shape),
                  _const_spec(w_fold.shape),
                  _const_spec(w_ff1.shape),
                  _const_spec(w_ff2.shape)],
        out_specs=pl.BlockSpec((1, tm, d), mm_rows),
        out_shape=jax.ShapeDtypeStruct((batch, seq, d), F32),
        scratch_shapes=[pltpu.VMEM((tm + 2 * POOL_HALO, d_pool), BF16),
                        pltpu.VMEM((2, tm, dm + d_pool), BF16)],
        compiler_params=pltpu.CompilerParams(dimension_semantics=("arbitrary",),
                                             vmem_limit_bytes=VMEM_LIMIT_BYTES),
    )(x, hm, o, u, u, u, mod, g_head.reshape(1, dm),
      g_ffn.reshape(1, d), g_final.reshape(1, d), bands, w_out, w_fold, w_ff1, w_ff2)


def _pack_w_in(w_in):
    dm = N_HEADS * HEAD_DIM
    n_gates = N_DIRS * 2 * N_HEADS
    q, k, v, o = (w_in[:, i * dm:(i + 1) * dm] for i in range(4))
    gates = w_in[:, 4 * dm:4 * dm + n_gates]
    pool = w_in[:, 4 * dm + n_gates:]
    gates_t = gates.reshape(-1, N_DIRS, 2, N_HEADS).transpose(3, 1, 2, 0).reshape(n_gates, -1)
    cols = jnp.concatenate([k, o, pool], axis=1).astype(BF16)
    rows = jnp.concatenate([q.T, v.T, gates_t], axis=0).astype(BF16)
    return cols, rows


def _pack_gate_bias(b_igate, b_fgate):
    rows = jnp.stack([b_igate, b_fgate], axis=1)
    rows = rows.transpose(2, 0, 1).reshape(N_HEADS, N_GATE_ROWS)
    return jnp.broadcast_to(rows[:, :, None], (N_HEADS, N_GATE_ROWS, CHUNK)).astype(F32)


def kernel(x, c, w_ada, b_ada, g_mix, w_in, b_igate, b_fgate, g_head, w_pool, pool_scale, w_out,
           g_ffn, w_ff1, w_ff2, g_final):
    depth = w_ada.shape[0]
    batch, seq, d = x.shape
    d_pool = pool_scale.shape[1]
    assert g_head.shape[1] == N_HEADS * HEAD_DIM and w_pool.shape[1] == len(POOL_WINDOWS)
    assert seq % INPROJ_ROWS == 0 and seq % OUTFFN_ROWS == 0 and w_ff1.shape[2] % FF_COLS == 0
    for l in range(depth):
        mod = _adaln(c, w_ada[l], b_ada[l]).reshape(batch, 6, d)
        (qt, k, vt, o, u, gates), (w_out_b, w_ff1_b) = _inproj(
            x, mod, g_mix[l], *_pack_w_in(w_in[l]), d_pool, (w_out[l], w_ff1[l]))
        hm, w_fold, (w_ff2_b,) = _mlstm(
            qt, k, vt, gates, _pack_gate_bias(b_igate[l], b_fgate[l]), w_pool[l], pool_scale[l], w_out[l],
            (w_ff2[l],))
        x = _outffn(x, hm, o, u, mod, g_head[l], g_ffn[l], g_final,
                    w_out_b, w_fold, w_ff1_b, w_ff2_b, final_norm=(l == depth - 1))
    return x
```

```python
import functools

import jax
import jax.numpy as jnp
import numpy as np
from jax import lax
from jax.experimental import pallas as pl
from jax.experimental.pallas import tpu as pltpu

F32 = jnp.float32
BF16 = jnp.bfloat16

LANES = 128
BF16_SUBLANES = 16
VMEM_LIMIT_BYTES = 56 * 1024 * 1024

N_HEADS = 4
HEAD_DIM = 128
CHUNK = 128
POOL_WINDOWS = (2, 4, 8, 16)
POOL_BLOCK = 128
POOL_HALO = 16
N_DIRS = 2
N_GATE_ROWS = 2 * N_DIRS
EPS = 1e-6
LOG2_E = 1.4426950408889634

ADALN_ROWS = 128
INPROJ_ROWS = 1024
OUTFFN_ROWS = 512
FF_COLS = 1024
AUG_ROWS = HEAD_DIM + BF16_SUBLANES

ROW_A, ROW_M, ROW_AINTER, ROW_EXPNEG, ROW_WK, ROW_DECAY = range(6)
N_COEF = 6
CHUNKS_PER_STEP = 32


def _const_spec(shape):
    zeros = (0,) * len(shape)
    return pl.BlockSpec(shape, lambda *_: zeros, pipeline_mode=pl.Buffered(1))


def _rms_scale(x):
    return x * lax.rsqrt(jnp.mean(x * x, axis=-1, keepdims=True) + EPS)


def _modulated_norm_bf16(x, gain, shift, scale):
    slope = (gain * (1.0 + scale)).astype(BF16)
    return _rms_scale(x).astype(BF16) * slope + shift.astype(BF16)


def _adaln_kernel(c_ref, w_ref, b_ref, o_ref):
    @pl.when(pl.program_id(0) == 0)
    def _():
        o_ref[...] = jnp.broadcast_to(b_ref[...], o_ref.shape)

    c = c_ref[...]
    act = c * jax.nn.sigmoid(c)
    o_ref[...] += jnp.dot(act.astype(BF16), w_ref[...].astype(BF16), preferred_element_type=F32)


def _adaln(c, w_ada, b_ada):
    batch, d = c.shape
    n = w_ada.shape[1]
    rows = ADALN_ROWS
    assert d % rows == 0
    return pl.pallas_call(
        _adaln_kernel,
        grid=(d // rows,),
        in_specs=[pl.BlockSpec((batch, rows), lambda j: (0, j)),
                  pl.BlockSpec((rows, n), lambda j: (j, 0)),
                  pl.BlockSpec((1, n), lambda j: (0, 0))],
        out_specs=pl.BlockSpec((batch, n), lambda j: (0, 0)),
        out_shape=jax.ShapeDtypeStruct((batch, n), F32),
        compiler_params=pltpu.CompilerParams(dimension_semantics=("arbitrary",),
                                             vmem_limit_bytes=VMEM_LIMIT_BYTES),
    )(c, w_ada, b_ada.reshape(1, n))


def _inproj_kernel(x_ref, mod_ref, g_ref, w_ref, wt_ref, *refs, n_casts):
    cast_in, cast_out = refs[:n_casts], refs[n_casts + 6:]
    qt_ref, k_ref, vt_ref, o_ref, u_ref, gates_ref = refs[n_casts:n_casts + 6]
    for src, dst in zip(cast_in, cast_out):
        dst[...] = src[...].astype(BF16)

    dm = N_HEADS * HEAD_DIM
    x = x_ref[0]
    shift = mod_ref[0, 0:1, :]
    scale = mod_ref[0, 1:2, :]
    hb = _modulated_norm_bf16(x, g_ref[...], shift, scale)

    by_token = jnp.dot(hb, w_ref[...], preferred_element_type=F32)
    for hd in range(N_HEADS):
        k_ref[0, hd] = by_token[:, hd * HEAD_DIM:(hd + 1) * HEAD_DIM].astype(BF16)
    o_ref[0] = by_token[:, dm:2 * dm].astype(BF16)
    u_ref[0] = by_token[:, 2 * dm:].astype(BF16)
    by_feature = lax.dot_general(wt_ref[...], hb, (((1,), (1,)), ((), ())), preferred_element_type=F32)
    for hd in range(N_HEADS):
        rows = slice(hd * HEAD_DIM, (hd + 1) * HEAD_DIM)
        qt_ref[0, hd] = (by_feature[rows, :] * (HEAD_DIM ** -0.5)).astype(BF16)
        vt_ref[0, hd, 0:HEAD_DIM, :] = by_feature[dm + hd * HEAD_DIM:dm + (hd + 1) * HEAD_DIM, :].astype(BF16)
        vt_ref[0, hd, HEAD_DIM:AUG_ROWS, :] = jnp.ones((AUG_ROWS - HEAD_DIM, x.shape[0]), BF16)
    gt = by_feature[2 * dm:, :]
    n_chunks = x.shape[0] // CHUNK
    for hd in range(N_HEADS):
        for r in range(N_GATE_ROWS):
            row = hd * N_GATE_ROWS + r
            for j in range(n_chunks):
                gates_ref[0, hd, r, j:j + 1, :] = gt[row:row + 1, j * CHUNK:(j + 1) * CHUNK]


def _inproj(x, mod, g_mix, w_cat, w_gates_t, d_pool, weights_to_cast):
    batch, seq, d = x.shape
    tm = INPROJ_ROWS
    dm = N_HEADS * HEAD_DIM
    grid = (batch, seq // tm)
    tiles_per_seq = seq // tm
    n_steps = batch * tiles_per_seq
    cast_specs = []
    for w in weights_to_cast:
        assert w.shape[0] % (n_steps * BF16_SUBLANES) == 0
        cast_specs.append(pl.BlockSpec((w.shape[0] // n_steps, w.shape[1]),
                                       lambda b, i: (b * tiles_per_seq + i, 0)))
    head_t = jax.ShapeDtypeStruct((batch, N_HEADS, HEAD_DIM, seq), BF16)
    head_n = jax.ShapeDtypeStruct((batch, N_HEADS, seq, HEAD_DIM), BF16)
    head_aug = jax.ShapeDtypeStruct((batch, N_HEADS, AUG_ROWS, seq), BF16)
    out_shape = (head_t, head_n, head_aug,
                 jax.ShapeDtypeStruct((batch, seq, dm), BF16),
                 jax.ShapeDtypeStruct((batch, seq, d_pool), BF16),
                 jax.ShapeDtypeStruct((batch, N_HEADS, N_GATE_ROWS, seq // CHUNK, CHUNK), F32))
    spec_t = pl.BlockSpec((1, N_HEADS, HEAD_DIM, tm), lambda b, i: (b, 0, 0, i))
    spec_n = pl.BlockSpec((1, N_HEADS, tm, HEAD_DIM), lambda b, i: (b, 0, i, 0))
    spec_aug = pl.BlockSpec((1, N_HEADS, AUG_ROWS, tm), lambda b, i: (b, 0, 0, i))
    out_specs = (spec_t, spec_n, spec_aug,
                 pl.BlockSpec((1, tm, dm), lambda b, i: (b, i, 0)),
                 pl.BlockSpec((1, tm, d_pool), lambda b, i: (b, i, 0)),
                 pl.BlockSpec((1, N_HEADS, N_GATE_ROWS, tm // CHUNK, CHUNK), lambda b, i: (b, 0, 0, i, 0)))
    outs = pl.pallas_call(
        functools.partial(_inproj_kernel, n_casts=len(weights_to_cast)),
        grid=grid,
        in_specs=[pl.BlockSpec((1, tm, d), lambda b, i: (b, i, 0)),
                  pl.BlockSpec((1,) + mod.shape[1:], lambda b, i: (b, 0, 0)),
                  _const_spec((1, d)),
                  _const_spec(w_cat.shape),
                  _const_spec(w_gates_t.shape)] + cast_specs,
        out_specs=list(out_specs) + cast_specs,
        out_shape=list(out_shape) + [jax.ShapeDtypeStruct(w.shape, BF16) for w in weights_to_cast],
        compiler_params=pltpu.CompilerParams(dimension_semantics=("arbitrary", "arbitrary"),
                                             vmem_limit_bytes=VMEM_LIMIT_BYTES),
    )(x, mod, g_mix.reshape(1, d), w_cat, w_gates_t, *weights_to_cast)
    return outs[:6], outs[6:]


def _time_scan(x, op, fill, reverse):
    row = lax.broadcasted_iota(jnp.int32, x.shape, 0)
    step = 1
    while step < CHUNK:
        if reverse:
            shifted = jnp.where(row < CHUNK - step, pltpu.roll(x, CHUNK - step, axis=0), fill)
        else:
            shifted = jnp.where(row >= step, pltpu.roll(x, step, axis=0), fill)
        x = op(x, shifted)
        step *= 2
    return x


def _chunk_carry(decay_log, peak, reverse):
    n = decay_log.shape[0]
    row = lax.broadcasted_iota(jnp.int32, decay_log.shape, 0)

    def earlier(x, dist, fill):
        if reverse:
            return jnp.where(row < n - dist, pltpu.roll(x, n - dist, axis=0), fill)
        return jnp.where(row >= dist, pltpu.roll(x, dist, axis=0), fill)

    p, q = decay_log, decay_log + peak
    dist = 1
    while dist < n:
        q = jnp.maximum(earlier(q, dist, -jnp.inf) + p, q)
        p = earlier(p, dist, 0.0) + p
        dist *= 2
    return earlier(jnp.maximum(p, q), 1, 0.0)


def _mlstm_kernel(qt_ref, k_ref, vt_ref, gates_ref, gbias_ref, wpool_ref, pscale_ref, woutp_ref, *refs,
                  n_casts, n_units):
    cast_in, out_ref, fold_ref = refs[:n_casts], refs[n_casts], refs[n_casts + 1]
    cast_out = refs[n_casts + 2:2 * n_casts + 2]
    coef_ref, sprev_ref, st_ref = refs[2 * n_casts + 2:]

    def weight_slabs():
        for src, dst in zip(cast_in, cast_out):
            dst[...] = src[...].astype(BF16)
        lhs = wpool_ref[...] * pscale_ref[...]
        rhs = woutp_ref[...]
        lhs_hi, rhs_hi = lhs.astype(BF16), rhs.astype(BF16)
        lhs_lo = (lhs - lhs_hi.astype(F32)).astype(BF16)
        rhs_lo = (rhs - rhs_hi.astype(F32)).astype(BF16)
        fold_ref[...] = (jnp.dot(lhs_hi, rhs_hi, preferred_element_type=F32)
                         + jnp.dot(lhs_hi, rhs_lo, preferred_element_type=F32)
                         + jnp.dot(lhs_lo, rhs_hi, preferred_element_type=F32)).astype(BF16)

    n_chunks = gates_ref.shape[3]
    assert 2 * N_DIRS * n_chunks == LANES
    step = pl.program_id(0)

    def prologue(slot):
        log_i, log_f = [], []
        for d in range(N_DIRS):
            log_i.append(gates_ref[0, 0, 2 * d] + gbias_ref[0, 2 * d:2 * d + 1, :])
            z = gates_ref[0, 0, 2 * d + 1] + gbias_ref[0, 2 * d + 1:2 * d + 2, :]
            log_f.append(-(jnp.maximum(-z, 0.0) + jnp.log1p(jnp.exp(-jnp.abs(z)))))
        by_time = jnp.concatenate(log_f + log_i, axis=0).T
        lane = lax.broadcasted_iota(jnp.int32, by_time.shape, 1)
        fwd_lane = (lane % (N_DIRS * n_chunks)) < n_chunks
        b_t = jnp.where(fwd_lane, _time_scan(by_time, jnp.add, 0.0, False),
                        _time_scan(by_time, jnp.add, 0.0, True))
        a_t = pltpu.roll(by_time, N_DIRS * n_chunks, axis=1) - b_t
        cm_t = jnp.where(fwd_lane, _time_scan(a_t, jnp.maximum, -jnp.inf, False),
                         _time_scan(a_t, jnp.maximum, -jnp.inf, True))
        b_rows, a_rows, cm_rows = b_t.T, a_t.T, cm_t.T

        for d in range(N_DIRS):
            reverse = d == 1
            last = 0 if reverse else CHUNK - 1
            b = b_rows[d * n_chunks:(d + 1) * n_chunks]
            a = a_rows[d * n_chunks:(d + 1) * n_chunks]
            cm = cm_rows[d * n_chunks:(d + 1) * n_chunks]
            b_last = jnp.broadcast_to(b[:, last:last + 1], b.shape)
            cm_last = jnp.broadcast_to(cm[:, last:last + 1], b.shape)
            m_prev = _chunk_carry(b_last, cm_last, reverse)
            m_row = jnp.maximum(m_prev, cm)
            a_inter = jnp.exp(m_prev - m_row)
            exp_neg = jnp.exp(-(b + m_row))
            m_end = jnp.broadcast_to(m_row[:, last:last + 1], b.shape)
            wk = jnp.exp(a - m_end)
            decay = jnp.broadcast_to(a_inter[:, last:last + 1], b.shape)
            for r, val in ((ROW_A, a * LOG2_E), (ROW_M, m_row * LOG2_E), (ROW_AINTER, a_inter),
                           (ROW_EXPNEG, exp_neg), (ROW_WK, wk), (ROW_DECAY, decay)):
                coef_ref[slot, d, r] = val

    def passes(slot):
        def coef_row(d, r, c):
            return coef_ref[slot, d, r, pl.ds(c, 1), :]

        def chunk_operands(c):
            off = pl.multiple_of(c * CHUNK, CHUNK)
            return off, qt_ref[0, 0, :, pl.ds(off, CHUNK)], k_ref[0, 0, pl.ds(off, CHUNK), :], \
                vt_ref[0, 0, :, pl.ds(off, CHUNK)]

        key_pos = lax.broadcasted_iota(jnp.int32, (CHUNK, CHUNK), 0)
        qry_pos = lax.broadcasted_iota(jnp.int32, (CHUNK, CHUNK), 1)
        visible = (key_pos <= qry_pos, key_pos >= qry_pos)

        def weighted_scores(c):
            _, qt_c, k_c, _ = chunk_operands(c)
            scores = jnp.dot(k_c, qt_c, preferred_element_type=F32)
            for d in range(N_DIRS):
                a_col = jnp.broadcast_to(coef_row(d, ROW_A, c), (CHUNK, CHUNK)).T
                weight = jnp.exp2(jnp.where(visible[d], a_col - coef_row(d, ROW_M, c), -jnp.inf))
                st_ref[d, c] = (scores * weight).astype(BF16)

        def state_step(d, c, state):
            _, _, k_c, vt_c = chunk_operands(c)
            sprev_ref[d, c] = state.astype(BF16)
            vw = vt_c * coef_row(d, ROW_WK, c).astype(BF16)
            return coef_row(d, ROW_DECAY, c) * state + jnp.dot(vw, k_c, preferred_element_type=F32)

        def state_body(i, states):
            s_fwd, s_bwd = states
            for j in range(CHUNKS_PER_STEP):
                c = i * CHUNKS_PER_STEP + j
                weighted_scores(c)
                s_fwd = state_step(0, c, s_fwd)
                s_bwd = state_step(1, n_chunks - 1 - c, s_bwd)
            return s_fwd, s_bwd

        zero_state = jnp.zeros((AUG_ROWS, HEAD_DIM), F32)
        lax.fori_loop(0, n_chunks // CHUNKS_PER_STEP, state_body, (zero_state, zero_state))

        def output_chunk(c):
            off, qt_c, _, vt_c = chunk_operands(c)
            hsum = None
            for d in range(N_DIRS):
                qa = qt_c * coef_row(d, ROW_AINTER, c).astype(BF16)
                lhs = jnp.concatenate([vt_c, sprev_ref[d, c]], axis=1)
                rhs = jnp.concatenate([st_ref[d, c], qa], axis=0)
                numden = jnp.dot(lhs, rhs, preferred_element_type=F32)
                den = numden[HEAD_DIM:HEAD_DIM + 1, :]
                ht = numden[0:HEAD_DIM, :] * (1.0 / jnp.maximum(jnp.abs(den), coef_row(d, ROW_EXPNEG, c)))
                hsum = ht if hsum is None else hsum + ht
            out_ref[0, 0, pl.ds(off, CHUNK), :] = hsum.T.astype(BF16)

        def output_body(i, carry):
            for j in range(CHUNKS_PER_STEP):
                output_chunk(i * CHUNKS_PER_STEP + j)
            return carry

        lax.fori_loop(0, n_chunks // CHUNKS_PER_STEP, output_body, 0)

    @pl.when(step == 0)
    def _():
        prologue(0)
        weight_slabs()

    @pl.when(step == n_units)
    def _():
        passes((n_units - 1) % 2)
        weight_slabs()

    for fill in range(2):
        @pl.when((step > 0) & (step < n_units) & (step % 2 == fill))
        def _():
            passes(1 - fill)
            prologue(fill)
            weight_slabs()


def _mlstm(qt, k, vt, gates, gbias, w_pool, pool_scale, w_out, weights_to_cast):
    batch, heads, hd, seq = qt.shape
    n_chunks = seq // CHUNK
    assert n_chunks % CHUNKS_PER_STEP == 0
    n_units = batch * heads
    groups, gdim, _ = w_pool.shape
    d = w_out.shape[1]
    first_pool_block = (w_out.shape[0] - groups * gdim) // gdim
    units_per_group = n_units // groups
    fold_rows = gdim // units_per_group
    assert n_units % groups == 0 and fold_rows % BF16_SUBLANES == 0

    def pass_unit(g):
        u = jnp.maximum(g - 1, 0)
        return u // heads, u % heads, 0, 0

    def prologue_unit(g):
        return jnp.minimum(g, n_units - 1)

    spec_t = pl.BlockSpec((1, 1, hd, seq), pass_unit)
    spec_n = pl.BlockSpec((1, 1, seq, hd), pass_unit)
    spec_aug = pl.BlockSpec((1, 1, AUG_ROWS, seq), pass_unit)
    cast_specs = []
    for w in weights_to_cast:
        assert w.shape[0] % (n_units * BF16_SUBLANES) == 0
        cast_specs.append(pl.BlockSpec((w.shape[0] // n_units, w.shape[1]), lambda g: (prologue_unit(g), 0)))
    outs = pl.pallas_call(
        functools.partial(_mlstm_kernel, n_casts=len(weights_to_cast), n_units=n_units),
        grid=(n_units + 1,),
        in_specs=[spec_t, spec_n, spec_aug,
                  pl.BlockSpec((1, 1, N_GATE_ROWS, n_chunks, CHUNK),
                               lambda g: (prologue_unit(g) // heads, prologue_unit(g) % heads, 0, 0, 0)),
                  pl.BlockSpec((1, N_GATE_ROWS, CHUNK), lambda g: (prologue_unit(g) % heads, 0, 0)),
                  pl.BlockSpec((fold_rows, gdim), lambda g: (prologue_unit(g), 0)),
                  pl.BlockSpec((1, gdim), lambda g: (0, prologue_unit(g) // units_per_group)),
                  pl.BlockSpec((gdim, d), lambda g: (first_pool_block + prologue_unit(g) // units_per_group, 0))]
        + cast_specs,
        out_specs=[spec_n, pl.BlockSpec((fold_rows, d), lambda g: (prologue_unit(g), 0))] + cast_specs,
        out_shape=[jax.ShapeDtypeStruct((batch, heads, seq, hd), BF16),
                   jax.ShapeDtypeStruct((groups * gdim, d), BF16)]
        + [jax.ShapeDtypeStruct(w.shape, BF16) for w in weights_to_cast],
        scratch_shapes=[pltpu.VMEM((2, N_DIRS, N_COEF, n_chunks, CHUNK), F32),
                        pltpu.VMEM((N_DIRS, n_chunks, AUG_ROWS, hd), BF16),
                        pltpu.VMEM((N_DIRS, n_chunks, CHUNK, CHUNK), BF16)],
        compiler_params=pltpu.CompilerParams(dimension_semantics=("arbitrary",),
                                             vmem_limit_bytes=VMEM_LIMIT_BYTES),
    )(qt, k, vt, gates, gbias, w_pool.reshape(groups * gdim, gdim), pool_scale.reshape(1, groups * gdim), w_out,
      *weights_to_cast)
    return outs[0], outs[1], outs[2:]


def _outffn_kernel(x_ref, hm_ref, o_ref, u_ref, uprev_ref, unext_ref, mod_ref,
                   ghead_ref, gffn_ref, gfinal_ref,
                   band_ref, wout_ref, wfold_ref, wff1_ref, wff2_ref, out_ref,
                   ubuf_ref, ycat_ref, *, seq, n_tiles, final_norm):
    tm = x_ref.shape[1]
    dm = N_HEADS * HEAD_DIM
    step = pl.program_id(0)
    tiles_per_seq = seq // tm
    tile = step % tiles_per_seq

    def matmul_stage(drain):
        gate1 = mod_ref[0, 2:3, :]
        yproj = (jnp.dot(ycat_ref[drain, :, 0:dm], wout_ref[0:dm, :], preferred_element_type=F32)
                 + jnp.dot(ycat_ref[drain, :, dm:], wfold_ref[...], preferred_element_type=F32))
        x1 = x_ref[0] + gate1 * yproj

        shift2 = mod_ref[0, 3:4, :]
        scale2 = mod_ref[0, 4:5, :]
        gate2 = mod_ref[0, 5:6, :]
        h2 = ((_rms_scale(x1) * gffn_ref[...]) * (1.0 + scale2) + shift2).astype(BF16)
        acc = jnp.zeros(x1.shape, F32)
        for j in range(wff1_ref.shape[1] // FF_COLS):
            hid = jnp.dot(h2, wff1_ref[:, j * FF_COLS:(j + 1) * FF_COLS], preferred_element_type=F32)
            hid = jnp.square(jnp.maximum(hid, 0.0)).astype(BF16)
            acc = acc + jnp.dot(hid, wff2_ref[j * FF_COLS:(j + 1) * FF_COLS, :], preferred_element_type=F32)
        x2 = x1 + gate2 * acc
        if final_norm:
            x2 = _rms_scale(x2) * gfinal_ref[...]
        out_ref[0] = x2

    def elementwise_stage(fill):
        for hd in range(N_HEADS):
            cols = slice(hd * HEAD_DIM, (hd + 1) * HEAD_DIM)
            hn = _rms_scale(hm_ref[0, hd].astype(F32)) * ghead_ref[:, cols]
            gate = jax.nn.sigmoid(o_ref[0, :, cols].astype(F32))
            ycat_ref[fill, :, cols] = (gate * hn).astype(BF16)

        ubuf_ref[0:POOL_HALO, :] = jnp.where(tile > 0, uprev_ref[0], jnp.zeros_like(uprev_ref[0]))
        ubuf_ref[POOL_HALO:POOL_HALO + tm, :] = u_ref[0]
        ubuf_ref[POOL_HALO + tm:, :] = jnp.where(tile < tiles_per_seq - 1, unext_ref[0],
                                                 jnp.zeros_like(unext_ref[0]))
        for gi in range(len(POOL_WINDOWS)):
            cols = slice(gi * HEAD_DIM, (gi + 1) * HEAD_DIM)
            ext = jnp.concatenate([ubuf_ref[r0:r0 + POOL_BLOCK + 2 * POOL_HALO, cols]
                                   for r0 in range(0, tm, POOL_BLOCK)], axis=1)
            mixed = jnp.dot(band_ref[gi], ext, preferred_element_type=F32).astype(BF16)
            for blk, r0 in enumerate(range(0, tm, POOL_BLOCK)):
                ycat_ref[fill, r0:r0 + POOL_BLOCK, dm + gi * HEAD_DIM:dm + (gi + 1) * HEAD_DIM] = (
                    mixed[:, blk * HEAD_DIM:(blk + 1) * HEAD_DIM])

    def sequence_end_rows(fill):
        def clipped_rows(r0):
            ext = ubuf_ref[r0:r0 + 3 * POOL_HALO, :].astype(F32)
            pos = tile * tm + r0 + lax.broadcasted_iota(jnp.int32, (POOL_HALO, HEAD_DIM), 0)
            for gi, win in enumerate(POOL_WINDOWS):
                half = win // 2
                grp = ext[:, gi * HEAD_DIM:(gi + 1) * HEAD_DIM]
                total = grp[POOL_HALO - half:2 * POOL_HALO - half]
                for j in range(1 - half, half):
                    total = total + grp[POOL_HALO + j:2 * POOL_HALO + j]
                cnt = jnp.minimum(pos + half, seq) - jnp.maximum(pos - half, 0)
                mixed = total / cnt.astype(F32) - grp[POOL_HALO:2 * POOL_HALO]
                ycat_ref[fill, r0:r0 + POOL_HALO, dm + gi * HEAD_DIM:dm + (gi + 1) * HEAD_DIM] = mixed.astype(BF16)

        @pl.when(tile == 0)
        def _():
            clipped_rows(0)

        @pl.when(tile == tiles_per_seq - 1)
        def _():
            clipped_rows(tm - POOL_HALO)

    @pl.when(step == 0)
    def _():
        elementwise_stage(0)
        sequence_end_rows(0)

    @pl.when(step == n_tiles)
    def _():
        matmul_stage((n_tiles - 1) % 2)

    for fill in range(2):
        @pl.when((step > 0) & (step < n_tiles) & (step % 2 == fill))
        def _():
            matmul_stage(1 - fill)
            elementwise_stage(fill)
            sequence_end_rows(fill)


def _pool_bands():
    row = np.arange(POOL_BLOCK)[:, None]
    col = np.arange(POOL_BLOCK + 2 * POOL_HALO)[None, :] - POOL_HALO
    bands = [((col >= row - win // 2) & (col < row + win // 2)) / win - (col == row) for win in POOL_WINDOWS]
    return jnp.asarray(np.stack(bands), dtype=BF16)


def _outffn(x, hm, o, u, mod, g_head, g_ffn, g_final, w_out, w_fold, w_ff1, w_ff2, final_norm):
    batch, seq, d = x.shape
    tm = OUTFFN_ROWS
    dm = N_HEADS * HEAD_DIM
    d_pool = u.shape[2]
    halo_per_tile = tm // POOL_HALO
    n_halo_blocks = seq // POOL_HALO
    tiles_per_seq = seq // tm
    n_tiles = batch * tiles_per_seq

    def mm_tile(g):
        t = jnp.maximum(g - 1, 0)
        return t // tiles_per_seq, t % tiles_per_seq

    def ew_tile(g):
        t = jnp.minimum(g, n_tiles - 1)
        return t // tiles_per_seq, t % tiles_per_seq

    def mm_rows(g):
        b, i = mm_tile(g)
        return b, i, 0

    def ew_rows(g):
        b, i = ew_tile(g)
        return b, i, 0

    def ew_heads(g):
        b, i = ew_tile(g)
        return b, 0, i, 0

    def ew_prev(g):
        b, i = ew_tile(g)
        return b, jnp.maximum(i * halo_per_tile - 1, 0), 0

    def ew_next(g):
        b, i = ew_tile(g)
        return b, jnp.minimum((i + 1) * halo_per_tile, n_halo_blocks - 1), 0

    bands = _pool_bands()
    kernel = functools.partial(_outffn_kernel, seq=seq, n_tiles=n_tiles, final_norm=final_norm)
    return pl.pallas_call(
        kernel,
        grid=(n_tiles + 1,),
        in_specs=[pl.BlockSpec((1, tm, d), mm_rows),
                  pl.BlockSpec((1, N_HEADS, tm, HEAD_DIM), ew_heads),
                  pl.BlockSpec((1, tm, dm), ew_rows),
                  pl.BlockSpec((1, tm, d_pool), ew_rows),
                  pl.BlockSpec((1, POOL_HALO, d_pool), ew_prev),
                  pl.BlockSpec((1, POOL_HALO, d_pool), ew_next),
                  pl.BlockSpec((1,) + mod.shape[1:], lambda g: (mm_tile(g)[0], 0, 0)),
                  _const_spec((1, dm)),
                  _const_spec((1, d)),
                  _const_spec((1, d)),
                  _const_spec(bands.shape),
                  _const_spec(w_out.shape),
                  _const_spec(w_fold.shape),
                  _const_spec(w_ff1.shape),
                  _const_spec(w_ff2.shape)],
        out_specs=pl.BlockSpec((1, tm, d), mm_rows),
        out_shape=jax.ShapeDtypeStruct((batch, seq, d), F32),
        scratch_shapes=[pltpu.VMEM((tm + 2 * POOL_HALO, d_pool), BF16),
                        pltpu.VMEM((2, tm, dm + d_pool), BF16)],
        compiler_params=pltpu.CompilerParams(dimension_semantics=("arbitrary",),
                                             vmem_limit_bytes=VMEM_LIMIT_BYTES),
    )(x, hm, o, u, u, u, mod, g_head.reshape(1, dm),
      g_ffn.reshape(1, d), g_final.reshape(1, d), bands, w_out, w_fold, w_ff1, w_ff2)


def _pack_w_in(w_in):
    dm = N_HEADS * HEAD_DIM
    n_gates = N_DIRS * 2 * N_HEADS
    q, k, v, o = (w_in[:, i * dm:(i + 1) * dm] for i in range(4))
    gates = w_in[:, 4 * dm:4 * dm + n_gates]
    pool = w_in[:, 4 * dm + n_gates:]
    gates_t = gates.reshape(-1, N_DIRS, 2, N_HEADS).transpose(3, 1, 2, 0).reshape(n_gates, -1)
    cols = jnp.concatenate([k, o, pool], axis=1).astype(BF16)
    rows = jnp.concatenate([q.T, v.T, gates_t], axis=0).astype(BF16)
    return cols, rows


def _pack_gate_bias(b_igate, b_fgate):
    rows = jnp.stack([b_igate, b_fgate], axis=1)
    rows = rows.transpose(2, 0, 1).reshape(N_HEADS, N_GATE_ROWS)
    return jnp.broadcast_to(rows[:, :, None], (N_HEADS, N_GATE_ROWS, CHUNK)).astype(F32)


def kernel(x, c, w_ada, b_ada, g_mix, w_in, b_igate, b_fgate, g_head, w_pool, pool_scale, w_out,
           g_ffn, w_ff1, w_ff2, g_final):
    depth = w_ada.shape[0]
    batch, seq, d = x.shape
    d_pool = pool_scale.shape[1]
    assert g_head.shape[1] == N_HEADS * HEAD_DIM and w_pool.shape[1] == len(POOL_WINDOWS)
    assert seq % INPROJ_ROWS == 0 and seq % OUTFFN_ROWS == 0 and w_ff1.shape[2] % FF_COLS == 0
    for l in range(depth):
        mod = _adaln(c, w_ada[l], b_ada[l]).reshape(batch, 6, d)
        (qt, k, vt, o, u, gates), (w_out_b, w_ff1_b) = _inproj(
            x, mod, g_mix[l], *_pack_w_in(w_in[l]), d_pool, (w_out[l], w_ff1[l]))
        hm, w_fold, (w_ff2_b,) = _mlstm(
            qt, k, vt, gates, _pack_gate_bias(b_igate[l], b_fgate[l]), w_pool[l], pool_scale[l], w_out[l],
            (w_ff2[l],))
        x = _outffn(x, hm, o, u, mod, g_head[l], g_ffn[l], g_final,
                    w_out_b, w_fold, w_ff1_b, w_ff2_b, final_norm=(l == depth - 1))
    return x
```

```python
import functools

import jax
import jax.numpy as jnp
import numpy as np
from jax import lax
from jax.experimental import pallas as pl
from jax.experimental.pallas import tpu as pltpu

F32 = jnp.float32
BF16 = jnp.bfloat16

LANES = 128
BF16_SUBLANES = 16
VMEM_LIMIT_BYTES = 56 * 1024 * 1024

N_HEADS = 4
HEAD_DIM = 128
CHUNK = 128
POOL_WINDOWS = (2, 4, 8, 16)
POOL_BLOCK = 128
POOL_HALO = 16
N_DIRS = 2
N_GATE_ROWS = 2 * N_DIRS
EPS = 1e-6
LOG2_E = 1.4426950408889634

ADALN_ROWS = 128
INPROJ_ROWS = 1024
OUTFFN_ROWS = 512
FF_COLS = 1024
AUG_ROWS = HEAD_DIM + BF16_SUBLANES

ROW_A, ROW_M, ROW_AINTER, ROW_EXPNEG, ROW_WK, ROW_DECAY = range(6)
N_COEF = 6
CHUNKS_PER_STEP = 32


def _const_spec(shape):
    zeros = (0,) * len(shape)
    return pl.BlockSpec(shape, lambda *_: zeros, pipeline_mode=pl.Buffered(1))


def _rms_scale(x):
    return x * lax.rsqrt(jnp.mean(x * x, axis=-1, keepdims=True) + EPS)


def _modulated_norm_bf16(x, gain, shift, scale):
    slope = (gain * (1.0 + scale)).astype(BF16)
    return _rms_scale(x).astype(BF16) * slope + shift.astype(BF16)


def _adaln_kernel(c_ref, w_ref, b_ref, o_ref):
    @pl.when(pl.program_id(0) == 0)
    def _():
        o_ref[...] = jnp.broadcast_to(b_ref[...], o_ref.shape)

    c = c_ref[...]
    act = c * jax.nn.sigmoid(c)
    o_ref[...] += jnp.dot(act.astype(BF16), w_ref[...].astype(BF16), preferred_element_type=F32)


def _adaln(c, w_ada, b_ada):
    batch, d = c.shape
    n = w_ada.shape[1]
    rows = ADALN_ROWS
    assert d % rows == 0
    return pl.pallas_call(
        _adaln_kernel,
        grid=(d // rows,),
        in_specs=[pl.BlockSpec((batch, rows), lambda j: (0, j)),
                  pl.BlockSpec((rows, n), lambda j: (j, 0)),
                  pl.BlockSpec((1, n), lambda j: (0, 0))],
        out_specs=pl.BlockSpec((batch, n), lambda j: (0, 0)),
        out_shape=jax.ShapeDtypeStruct((batch, n), F32),
        compiler_params=pltpu.CompilerParams(dimension_semantics=("arbitrary",),
                                             vmem_limit_bytes=VMEM_LIMIT_BYTES),
    )(c, w_ada, b_ada.reshape(1, n))


def _inproj_kernel(x_ref, mod_ref, g_ref, w_ref, wt_ref, *refs, n_casts):
    cast_in, cast_out = refs[:n_casts], refs[n_casts + 6:]
    qt_ref, k_ref, vt_ref, o_ref, u_ref, gates_ref = refs[n_casts:n_casts + 6]
    for src, dst in zip(cast_in, cast_out):
        dst[...] = src[...].astype(BF16)

    dm = N_HEADS * HEAD_DIM
    x = x_ref[0]
    shift = mod_ref[0, 0:1, :]
    scale = mod_ref[0, 1:2, :]
    hb = _modulated_norm_bf16(x, g_ref[...], shift, scale)

    by_token = jnp.dot(hb, w_ref[...], preferred_element_type=F32)
    for hd in range(N_HEADS):
        k_ref[0, hd] = by_token[:, hd * HEAD_DIM:(hd + 1) * HEAD_DIM].astype(BF16)
    o_ref[0] = by_token[:, dm:2 * dm].astype(BF16)
    u_ref[0] = by_token[:, 2 * dm:].astype(BF16)
    by_feature = lax.dot_general(wt_ref[...], hb, (((1,), (1,)), ((), ())), preferred_element_type=F32)
    for hd in range(N_HEADS):
        rows = slice(hd * HEAD_DIM, (hd + 1) * HEAD_DIM)
        qt_ref[0, hd] = (by_feature[rows, :] * (HEAD_DIM ** -0.5)).astype(BF16)
        vt_ref[0, hd, 0:HEAD_DIM, :] = by_feature[dm + hd * HEAD_DIM:dm + (hd + 1) * HEAD_DIM, :].astype(BF16)
        vt_ref[0, hd, HEAD_DIM:AUG_ROWS, :] = jnp.ones((AUG_ROWS - HEAD_DIM, x.shape[0]), BF16)
    gt = by_feature[2 * dm:, :]
    n_chunks = x.shape[0] // CHUNK
    for hd in range(N_HEADS):
        for r in range(N_GATE_ROWS):
            row = hd * N_GATE_ROWS + r
            for j in range(n_chunks):
                gates_ref[0, hd, r, j:j + 1, :] = gt[row:row + 1, j * CHUNK:(j + 1) * CHUNK]


def _inproj(x, mod, g_mix, w_cat, w_gates_t, d_pool, weights_to_cast):
    batch, seq, d = x.shape
    tm = INPROJ_ROWS
    dm = N_HEADS * HEAD_DIM
    grid = (batch, seq // tm)
    tiles_per_seq = seq // tm
    n_steps = batch * tiles_per_seq
    cast_specs = []
    for w in weights_to_cast:
        assert w.shape[0] % (n_steps * BF16_SUBLANES) == 0
        cast_specs.append(pl.BlockSpec((w.shape[0] // n_steps, w.shape[1]),
                                       lambda b, i: (b * tiles_per_seq + i, 0)))
    head_t = jax.ShapeDtypeStruct((batch, N_HEADS, HEAD_DIM, seq), BF16)
    head_n = jax.ShapeDtypeStruct((batch, N_HEADS, seq, HEAD_DIM), BF16)
    head_aug = jax.ShapeDtypeStruct((batch, N_HEADS, AUG_ROWS, seq), BF16)
    out_shape = (head_t, head_n, head_aug,
                 jax.ShapeDtypeStruct((batch, seq, dm), BF16),
                 jax.ShapeDtypeStruct((batch, seq, d_pool), BF16),
                 jax.ShapeDtypeStruct((batch, N_HEADS, N_GATE_ROWS, seq // CHUNK, CHUNK), F32))
    spec_t = pl.BlockSpec((1, N_HEADS, HEAD_DIM, tm), lambda b, i: (b, 0, 0, i))
    spec_n = pl.BlockSpec((1, N_HEADS, tm, HEAD_DIM), lambda b, i: (b, 0, i, 0))
    spec_aug = pl.BlockSpec((1, N_HEADS, AUG_ROWS, tm), lambda b, i: (b, 0, 0, i))
    out_specs = (spec_t, spec_n, spec_aug,
                 pl.BlockSpec((1, tm, dm), lambda b, i: (b, i, 0)),
                 pl.BlockSpec((1, tm, d_pool), lambda b, i: (b, i, 0)),
                 pl.BlockSpec((1, N_HEADS, N_GATE_ROWS, tm // CHUNK, CHUNK), lambda b, i: (b, 0, 0, i, 0)))
    outs = pl.pallas_call(
        functools.partial(_inproj_kernel, n_casts=len(weights_to_cast)),
        grid=grid,
        in_specs=[pl.BlockSpec((1, tm, d), lambda b, i: (b, i, 0)),
                  pl.BlockSpec((1,) + mod.shape[1:], lambda b, i: (b, 0, 0)),
                  _const_spec((1, d)),
                  _const_spec(w_cat.shape),
                  _const_spec(w_gates_t.shape)] + cast_specs,
        out_specs=list(out_specs) + cast_specs,
        out_shape=list(out_shape) + [jax.ShapeDtypeStruct(w.shape, BF16) for w in weights_to_cast],
        compiler_params=pltpu.CompilerParams(dimension_semantics=("arbitrary", "arbitrary"),
                                             vmem_limit_bytes=VMEM_LIMIT_BYTES),
    )(x, mod, g_mix.reshape(1, d), w_cat, w_gates_t, *weights_to_cast)
    return outs[:6], outs[6:]


def _time_scan(x, op, fill, reverse):
    row = lax.broadcasted_iota(jnp.int32, x.shape, 0)
    step = 1
    while step < CHUNK:
        if reverse:
            shifted = jnp.where(row < CHUNK - step, pltpu.roll(x, CHUNK - step, axis=0), fill)
        else:
            shifted = jnp.where(row >= step, pltpu.roll(x, step, axis=0), fill)
        x = op(x, shifted)
        step *= 2
    return x


def _chunk_carry(decay_log, peak, reverse):
    n = decay_log.shape[0]
    row = lax.broadcasted_iota(jnp.int32, decay_log.shape, 0)

    def earlier(x, dist, fill):
        if reverse:
            return jnp.where(row < n - dist, pltpu.roll(x, n - dist, axis=0), fill)
        return jnp.where(row >= dist, pltpu.roll(x, dist, axis=0), fill)

    p, q = decay_log, decay_log + peak
    dist = 1
    while dist < n:
        q = jnp.maximum(earlier(q, dist, -jnp.inf) + p, q)
        p = earlier(p, dist, 0.0) + p
        dist *= 2
    return earlier(jnp.maximum(p, q), 1, 0.0)


def _mlstm_kernel(qt_ref, k_ref, vt_ref, gates_ref, gbias_ref, wpool_ref, pscale_ref, woutp_ref, *refs,
                  n_casts, n_units):
    cast_in, out_ref, fold_ref = refs[:n_casts], refs[n_casts], refs[n_casts + 1]
    cast_out = refs[n_casts + 2:2 * n_casts + 2]
    coef_ref, sprev_ref, st_ref = refs[2 * n_casts + 2:]

    def weight_slabs():
        for src, dst in zip(cast_in, cast_out):
            dst[...] = src[...].astype(BF16)
        lhs = wpool_ref[...] * pscale_ref[...]
        rhs = woutp_ref[...]
        lhs_hi, rhs_hi = lhs.astype(BF16), rhs.astype(BF16)
        lhs_lo = (lhs - lhs_hi.astype(F32)).astype(BF16)
        rhs_lo = (rhs - rhs_hi.astype(F32)).astype(BF16)
        fold_ref[...] = (jnp.dot(lhs_hi, rhs_hi, preferred_element_type=F32)
                         + jnp.dot(lhs_hi, rhs_lo, preferred_element_type=F32)
                         + jnp.dot(lhs_lo, rhs_hi, preferred_element_type=F32)).astype(BF16)

    n_chunks = gates_ref.shape[3]
    assert 2 * N_DIRS * n_chunks == LANES
    step = pl.program_id(0)

    def prologue(slot):
        log_i, log_f = [], []
        for d in range(N_DIRS):
            log_i.append(gates_ref[0, 0, 2 * d] + gbias_ref[0, 2 * d:2 * d + 1, :])
            z = gates_ref[0, 0, 2 * d + 1] + gbias_ref[0, 2 * d + 1:2 * d + 2, :]
            log_f.append(-(jnp.maximum(-z, 0.0) + jnp.log1p(jnp.exp(-jnp.abs(z)))))
        by_time = jnp.concatenate(log_f + log_i, axis=0).T
        lane = lax.broadcasted_iota(jnp.int32, by_time.shape, 1)
        fwd_lane = (lane % (N_DIRS * n_chunks)) < n_chunks
        b_t = jnp.where(fwd_lane, _time_scan(by_time, jnp.add, 0.0, False),
                        _time_scan(by_time, jnp.add, 0.0, True))
        a_t = pltpu.roll(by_time, N_DIRS * n_chunks, axis=1) - b_t
        cm_t = jnp.where(fwd_lane, _time_scan(a_t, jnp.maximum, -jnp.inf, False),
                         _time_scan(a_t, jnp.maximum, -jnp.inf, True))
        b_rows, a_rows, cm_rows = b_t.T, a_t.T, cm_t.T

        for d in range(N_DIRS):
            reverse = d == 1
            last = 0 if reverse else CHUNK - 1
            b = b_rows[d * n_chunks:(d + 1) * n_chunks]
            a = a_rows[d * n_chunks:(d + 1) * n_chunks]
            cm = cm_rows[d * n_chunks:(d + 1) * n_chunks]
            b_last = jnp.broadcast_to(b[:, last:last + 1], b.shape)
            cm_last = jnp.broadcast_to(cm[:, last:last + 1], b.shape)
            m_prev = _chunk_carry(b_last, cm_last, reverse)
            m_row = jnp.maximum(m_prev, cm)
            a_inter = jnp.exp(m_prev - m_row)
            exp_neg = jnp.exp(-(b + m_row))
            m_end = jnp.broadcast_to(m_row[:, last:last + 1], b.shape)
            wk = jnp.exp(a - m_end)
            decay = jnp.broadcast_to(a_inter[:, last:last + 1], b.shape)
            for r, val in ((ROW_A, a * LOG2_E), (ROW_M, m_row * LOG2_E), (ROW_AINTER, a_inter),
                           (ROW_EXPNEG, exp_neg), (ROW_WK, wk), (ROW_DECAY, decay)):
                coef_ref[slot, d, r] = val

    def passes(slot):
        def coef_row(d, r, c):
            return coef_ref[slot, d, r, pl.ds(c, 1), :]

        def chunk_operands(c):
            off = pl.multiple_of(c * CHUNK, CHUNK)
            return off, qt_ref[0, 0, :, pl.ds(off, CHUNK)], k_ref[0, 0, pl.ds(off, CHUNK), :], \
                vt_ref[0, 0, :, pl.ds(off, CHUNK)]

        key_pos = lax.broadcasted_iota(jnp.int32, (CHUNK, CHUNK), 0)
        qry_pos = lax.broadcasted_iota(jnp.int32, (CHUNK, CHUNK), 1)
        visible = (key_pos <= qry_pos, key_pos >= qry_pos)

        zero_tile = jnp.zeros((CHUNK, HEAD_DIM), BF16)

        def paired_dot(lhs_a, rhs_a, lhs_b, rhs_b):
            rhs = jnp.concatenate([jnp.concatenate([rhs_a, zero_tile], axis=1),
                                   jnp.concatenate([zero_tile, rhs_b], axis=1)], axis=0)
            both = jnp.dot(jnp.concatenate([lhs_a, lhs_b], axis=1), rhs, preferred_element_type=F32)
            return both[:, 0:HEAD_DIM], both[:, HEAD_DIM:]

        def weighted_scores(c0):
            _, qt_a, k_a, _ = chunk_operands(c0)
            _, qt_b, k_b, _ = chunk_operands(c0 + 1)
            for c, scores in zip((c0, c0 + 1), paired_dot(k_a, qt_a, k_b, qt_b)):
                for d in range(N_DIRS):
                    a_col = jnp.broadcast_to(coef_row(d, ROW_A, c), (CHUNK, CHUNK)).T
                    weight = jnp.exp2(jnp.where(visible[d], a_col - coef_row(d, ROW_M, c), -jnp.inf))
                    st_ref[d, c] = (scores * weight).astype(BF16)

        def state_steps(c_fwd, c_bwd, s_fwd, s_bwd):
            _, _, k_f, vt_f = chunk_operands(c_fwd)
            _, _, k_b, vt_b = chunk_operands(c_bwd)
            sprev_ref[0, c_fwd] = s_fwd.astype(BF16)
            sprev_ref[1, c_bwd] = s_bwd.astype(BF16)
            vw_f = vt_f * coef_row(0, ROW_WK, c_fwd).astype(BF16)
            vw_b = vt_b * coef_row(1, ROW_WK, c_bwd).astype(BF16)
            upd_f, upd_b = paired_dot(vw_f, k_f, vw_b, k_b)
            return (coef_row(0, ROW_DECAY, c_fwd) * s_fwd + upd_f,
                    coef_row(1, ROW_DECAY, c_bwd) * s_bwd + upd_b)

        def state_body(i, states):
            s_fwd, s_bwd = states
            for j in range(CHUNKS_PER_STEP):
                c = i * CHUNKS_PER_STEP + j
                if j % 2 == 0:
                    weighted_scores(c)
                s_fwd, s_bwd = state_steps(c, n_chunks - 1 - c, s_fwd, s_bwd)
            return s_fwd, s_bwd

        zero_state = jnp.zeros((AUG_ROWS, HEAD_DIM), F32)
        lax.fori_loop(0, n_chunks // CHUNKS_PER_STEP, state_body, (zero_state, zero_state))

        def output_chunk(c):
            off, qt_c, _, vt_c = chunk_operands(c)
            hsum = None
            for d in range(N_DIRS):
                qa = qt_c * coef_row(d, ROW_AINTER, c).astype(BF16)
                lhs = jnp.concatenate([vt_c, sprev_ref[d, c]], axis=1)
                rhs = jnp.concatenate([st_ref[d, c], qa], axis=0)
                numden = jnp.dot(lhs, rhs, preferred_element_type=F32)
                den = numden[HEAD_DIM:HEAD_DIM + 1, :]
                ht = numden[0:HEAD_DIM, :] * (1.0 / jnp.maximum(jnp.abs(den), coef_row(d, ROW_EXPNEG, c)))
                hsum = ht if hsum is None else hsum + ht
            out_ref[0, 0, pl.ds(off, CHUNK), :] = hsum.T.astype(BF16)

        def output_body(i, carry):
            for j in range(CHUNKS_PER_STEP):
                output_chunk(i * CHUNKS_PER_STEP + j)
            return carry

        lax.fori_loop(0, n_chunks // CHUNKS_PER_STEP, output_body, 0)

    @pl.when(step == 0)
    def _():
        prologue(0)
        weight_slabs()

    @pl.when(step == n_units)
    def _():
        passes((n_units - 1) % 2)
        weight_slabs()

    for fill in range(2):
        @pl.when((step > 0) & (step < n_units) & (step % 2 == fill))
        def _():
            passes(1 - fill)
            prologue(fill)
            weight_slabs()


def _mlstm(qt, k, vt, gates, gbias, w_pool, pool_scale, w_out, weights_to_cast):
    batch, heads, hd, seq = qt.shape
    n_chunks = seq // CHUNK
    assert n_chunks % CHUNKS_PER_STEP == 0
    n_units = batch * heads
    groups, gdim, _ = w_pool.shape
    d = w_out.shape[1]
    first_pool_block = (w_out.shape[0] - groups * gdim) // gdim
    units_per_group = n_units // groups
    fold_rows = gdim // units_per_group
    assert n_units % groups == 0 and fold_rows % BF16_SUBLANES == 0

    def pass_unit(g):
        u = jnp.maximum(g - 1, 0)
        return u // heads, u % heads, 0, 0

    def prologue_unit(g):
        return jnp.minimum(g, n_units - 1)

    spec_t = pl.BlockSpec((1, 1, hd, seq), pass_unit)
    spec_n = pl.BlockSpec((1, 1, seq, hd), pass_unit)
    spec_aug = pl.BlockSpec((1, 1, AUG_ROWS, seq), pass_unit)
    cast_specs = []
    for w in weights_to_cast:
        assert w.shape[0] % (n_units * BF16_SUBLANES) == 0
        cast_specs.append(pl.BlockSpec((w.shape[0] // n_units, w.shape[1]), lambda g: (prologue_unit(g), 0)))
    outs = pl.pallas_call(
        functools.partial(_mlstm_kernel, n_casts=len(weights_to_cast), n_units=n_units),
        grid=(n_units + 1,),
        in_specs=[spec_t, spec_n, spec_aug,
                  pl.BlockSpec((1, 1, N_GATE_ROWS, n_chunks, CHUNK),
                               lambda g: (prologue_unit(g) // heads, prologue_unit(g) % heads, 0, 0, 0)),
                  pl.BlockSpec((1, N_GATE_ROWS, CHUNK), lambda g: (prologue_unit(g) % heads, 0, 0)),
                  pl.BlockSpec((fold_rows, gdim), lambda g: (prologue_unit(g), 0)),
                  pl.BlockSpec((1, gdim), lambda g: (0, prologue_unit(g) // units_per_group)),
                  pl.BlockSpec((gdim, d), lambda g: (first_pool_block + prologue_unit(g) // units_per_group, 0))]
        + cast_specs,
        out_specs=[spec_n, pl.BlockSpec((fold_rows, d), lambda g: (prologue_unit(g), 0))] + cast_specs,
        out_shape=[jax.ShapeDtypeStruct((batch, heads, seq, hd), BF16),
                   jax.ShapeDtypeStruct((groups * gdim, d), BF16)]
        + [jax.ShapeDtypeStruct(w.shape, BF16) for w in weights_to_cast],
        scratch_shapes=[pltpu.VMEM((2, N_DIRS, N_COEF, n_chunks, CHUNK), F32),
                        pltpu.VMEM((N_DIRS, n_chunks, AUG_ROWS, hd), BF16),
                        pltpu.VMEM((N_DIRS, n_chunks, CHUNK, CHUNK), BF16)],
        compiler_params=pltpu.CompilerParams(dimension_semantics=("arbitrary",),
                                             vmem_limit_bytes=VMEM_LIMIT_BYTES),
    )(qt, k, vt, gates, gbias, w_pool.reshape(groups * gdim, gdim), pool_scale.reshape(1, groups * gdim), w_out,
      *weights_to_cast)
    return outs[0], outs[1], outs[2:]


def _outffn_kernel(x_ref, hm_ref, o_ref, u_ref, uprev_ref, unext_ref, mod_ref,
                   ghead_ref, gffn_ref, gfinal_ref,
                   band_ref, wout_ref, wfold_ref, wff1_ref, wff2_ref, out_ref,
                   ubuf_ref, ycat_ref, *, seq, n_tiles, final_norm):
    tm = x_ref.shape[1]
    dm = N_HEADS * HEAD_DIM
    step = pl.program_id(0)
    tiles_per_seq = seq // tm
    tile = step % tiles_per_seq

    def matmul_stage(drain):
        gate1 = mod_ref[0, 2:3, :]
        yproj = (jnp.dot(ycat_ref[drain, :, 0:dm], wout_ref[0:dm, :], preferred_element_type=F32)
                 + jnp.dot(ycat_ref[drain, :, dm:], wfold_ref[...], preferred_element_type=F32))
        x1 = x_ref[0] + gate1 * yproj

        shift2 = mod_ref[0, 3:4, :]
        scale2 = mod_ref[0, 4:5, :]
        gate2 = mod_ref[0, 5:6, :]
        h2 = ((_rms_scale(x1) * gffn_ref[...]) * (1.0 + scale2) + shift2).astype(BF16)
        acc = jnp.zeros(x1.shape, F32)
        for j in range(wff1_ref.shape[1] // FF_COLS):
            hid = jnp.dot(h2, wff1_ref[:, j * FF_COLS:(j + 1) * FF_COLS], preferred_element_type=F32)
            hid = jnp.square(jnp.maximum(hid, 0.0)).astype(BF16)
            acc = acc + jnp.dot(hid, wff2_ref[j * FF_COLS:(j + 1) * FF_COLS, :], preferred_element_type=F32)
        x2 = x1 + gate2 * acc
        if final_norm:
            x2 = _rms_scale(x2) * gfinal_ref[...]
        out_ref[0] = x2

    def elementwise_stage(fill):
        for hd in range(N_HEADS):
            cols = slice(hd * HEAD_DIM, (hd + 1) * HEAD_DIM)
            hn = _rms_scale(hm_ref[0, hd].astype(F32)) * ghead_ref[:, cols]
            gate = jax.nn.sigmoid(o_ref[0, :, cols].astype(F32))
            ycat_ref[fill, :, cols] = (gate * hn).astype(BF16)

        ubuf_ref[0:POOL_HALO, :] = jnp.where(tile > 0, uprev_ref[0], jnp.zeros_like(uprev_ref[0]))
        ubuf_ref[POOL_HALO:POOL_HALO + tm, :] = u_ref[0]
        ubuf_ref[POOL_HALO + tm:, :] = jnp.where(tile < tiles_per_seq - 1, unext_ref[0],
                                                 jnp.zeros_like(unext_ref[0]))
        for gi in range(len(POOL_WINDOWS)):
            cols = slice(gi * HEAD_DIM, (gi + 1) * HEAD_DIM)
            ext = jnp.concatenate([ubuf_ref[r0:r0 + POOL_BLOCK + 2 * POOL_HALO, cols]
                                   for r0 in range(0, tm, POOL_BLOCK)], axis=1)
            mixed = jnp.dot(band_ref[gi], ext, preferred_element_type=F32).astype(BF16)
            for blk, r0 in enumerate(range(0, tm, POOL_BLOCK)):
                ycat_ref[fill, r0:r0 + POOL_BLOCK, dm + gi * HEAD_DIM:dm + (gi + 1) * HEAD_DIM] = (
                    mixed[:, blk * HEAD_DIM:(blk + 1) * HEAD_DIM])

    def sequence_end_rows(fill):
        def clipped_rows(r0):
            ext = ubuf_ref[r0:r0 + 3 * POOL_HALO, :].astype(F32)
            pos = tile * tm + r0 + lax.broadcasted_iota(jnp.int32, (POOL_HALO, HEAD_DIM), 0)
            for gi, win in enumerate(POOL_WINDOWS):
                half = win // 2
                grp = ext[:, gi * HEAD_DIM:(gi + 1) * HEAD_DIM]
                total = grp[POOL_HALO - half:2 * POOL_HALO - half]
                for j in range(1 - half, half):
                    total = total + grp[POOL_HALO + j:2 * POOL_HALO + j]
                cnt = jnp.minimum(pos + half, seq) - jnp.maximum(pos - half, 0)
                mixed = total / cnt.astype(F32) - grp[POOL_HALO:2 * POOL_HALO]
                ycat_ref[fill, r0:r0 + POOL_HALO, dm + gi * HEAD_DIM:dm + (gi + 1) * HEAD_DIM] = mixed.astype(BF16)

        @pl.when(tile == 0)
        def _():
            clipped_rows(0)

        @pl.when(tile == tiles_per_seq - 1)
        def _():
            clipped_rows(tm - POOL_HALO)

    @pl.when(step == 0)
    def _():
        elementwise_stage(0)
        sequence_end_rows(0)

    @pl.when(step == n_tiles)
    def _():
        matmul_stage((n_tiles - 1) % 2)

    for fill in range(2):
        @pl.when((step > 0) & (step < n_tiles) & (step % 2 == fill))
        def _():
            matmul_stage(1 - fill)
            elementwise_stage(fill)
            sequence_end_rows(fill)


def _pool_bands():
    row = np.arange(POOL_BLOCK)[:, None]
    col = np.arange(POOL_BLOCK + 2 * POOL_HALO)[None, :] - POOL_HALO
    bands = [((col >= row - win // 2) & (col < row + win // 2)) / win - (col == row) for win in POOL_WINDOWS]
    return jnp.asarray(np.stack(bands), dtype=BF16)


def _outffn(x, hm, o, u, mod, g_head, g_ffn, g_final, w_out, w_fold, w_ff1, w_ff2, final_norm):
    batch, seq, d = x.shape
    tm = OUTFFN_ROWS
    dm = N_HEADS * HEAD_DIM
    d_pool = u.shape[2]
    halo_per_tile = tm // POOL_HALO
    n_halo_blocks = seq // POOL_HALO
    tiles_per_seq = seq // tm
    n_tiles = batch * tiles_per_seq

    def mm_tile(g):
        t = jnp.maximum(g - 1, 0)
        return t // tiles_per_seq, t % tiles_per_seq

    def ew_tile(g):
        t = jnp.minimum(g, n_tiles - 1)
        return t // tiles_per_seq, t % tiles_per_seq

    def mm_rows(g):
        b, i = mm_tile(g)
        return b, i, 0

    def ew_rows(g):
        b, i = ew_tile(g)
        return b, i, 0

    def ew_heads(g):
        b, i = ew_tile(g)
        return b, 0, i, 0

    def ew_prev(g):
        b, i = ew_tile(g)
        return b, jnp.maximum(i * halo_per_tile - 1, 0), 0

    def ew_next(g):
        b, i = ew_tile(g)
        return b, jnp.minimum((i + 1) * halo_per_tile, n_halo_blocks - 1), 0

    bands = _pool_bands()
    kernel = functools.partial(_outffn_kernel, seq=seq, n_tiles=n_tiles, final_norm=final_norm)
    return pl.pallas_call(
        kernel,
        grid=(n_tiles + 1,),
        in_specs=[pl.BlockSpec((1, tm, d), mm_rows),
                  pl.BlockSpec((1, N_HEADS, tm, HEAD_DIM), ew_heads),
                  pl.BlockSpec((1, tm, dm), ew_rows),
                  pl.BlockSpec((1, tm, d_pool), ew_rows),
                  pl.BlockSpec((1, POOL_HALO, d_pool), ew_prev),
                  pl.BlockSpec((1, POOL_HALO, d_pool), ew_next),
                  pl.BlockSpec((1,) + mod.shape[1:], lambda g: (mm_tile(g)[0], 0, 0)),
                  _const_spec((1, dm)),
                  _const_spec((1, d)),
                  _const_spec((1, d)),
                  _const_spec(bands.shape),
                  _const_spec(w_out.shape),
                  _const_spec(w_fold.shape),
                  _const_spec(w_ff1.shape),
                  _const_spec(w_ff2.shape)],
        out_specs=pl.BlockSpec((1, tm, d), mm_rows),
        out_shape=jax.ShapeDtypeStruct((batch, seq, d), F32),
        scratch_shapes=[pltpu.VMEM((tm + 2 * POOL_HALO, d_pool), BF16),
                        pltpu.VMEM((2, tm, dm + d_pool), BF16)],
        compiler_params=pltpu.CompilerParams(dimension_semantics=("arbitrary",),
                                             vmem_limit_bytes=VMEM_LIMIT_BYTES),
    )(x, hm, o, u, u, u, mod, g_head.reshape(1, dm),
      g_ffn.reshape(1, d), g_final.reshape(1, d), bands, w_out, w_fold, w_ff1, w_ff2)


def _pack_w_in(w_in):
    dm = N_HEADS * HEAD_DIM
    n_gates = N_DIRS * 2 * N_HEADS
    q, k, v, o = (w_in[:, i * dm:(i + 1) * dm] for i in range(4))
    gates = w_in[:, 4 * dm:4 * dm + n_gates]
    pool = w_in[:, 4 * dm + n_gates:]
    gates_t = gates.reshape(-1, N_DIRS, 2, N_HEADS).transpose(3, 1, 2, 0).reshape(n_gates, -1)
    cols = jnp.concatenate([k, o, pool], axis=1).astype(BF16)
    rows = jnp.concatenate([q.T, v.T, gates_t], axis=0).astype(BF16)
    return cols, rows


def _pack_gate_bias(b_igate, b_fgate):
    rows = jnp.stack([b_igate, b_fgate], axis=1)
    rows = rows.transpose(2, 0, 1).reshape(N_HEADS, N_GATE_ROWS)
    return jnp.broadcast_to(rows[:, :, None], (N_HEADS, N_GATE_ROWS, CHUNK)).astype(F32)


def kernel(x, c, w_ada, b_ada, g_mix, w_in, b_igate, b_fgate, g_head, w_pool, pool_scale, w_out,
           g_ffn, w_ff1, w_ff2, g_final):
    depth = w_ada.shape[0]
    batch, seq, d = x.shape
    d_pool = pool_scale.shape[1]
    assert g_head.shape[1] == N_HEADS * HEAD_DIM and w_pool.shape[1] == len(POOL_WINDOWS)
    assert seq % INPROJ_ROWS == 0 and seq % OUTFFN_ROWS == 0 and w_ff1.shape[2] % FF_COLS == 0
    for l in range(depth):
        mod = _adaln(c, w_ada[l], b_ada[l]).reshape(batch, 6, d)
        (qt, k, vt, o, u, gates), (w_out_b, w_ff1_b) = _inproj(
            x, mod, g_mix[l], *_pack_w_in(w_in[l]), d_pool, (w_out[l], w_ff1[l]))
        hm, w_fold, (w_ff2_b,) = _mlstm(
            qt, k, vt, gates, _pack_gate_bias(b_igate[l], b_fgate[l]), w_pool[l], pool_scale[l], w_out[l],
            (w_ff2[l],))
        x = _outffn(x, hm, o, u, mod, g_head[l], g_ffn[l], g_final,
                    w_out_b, w_fold, w_ff1_b, w_ff2_b, final_norm=(l == depth - 1))
    return x
```

```python
import functools

import jax
import jax.numpy as jnp
import numpy as np
from jax import lax
from jax.experimental import pallas as pl
from jax.experimental.pallas import tpu as pltpu

F32 = jnp.float32
BF16 = jnp.bfloat16

LANES = 128
BF16_SUBLANES = 16
VMEM_LIMIT_BYTES = 56 * 1024 * 1024

N_HEADS = 4
HEAD_DIM = 128
CHUNK = 128
POOL_WINDOWS = (2, 4, 8, 16)
POOL_BLOCK = 128
POOL_HALO = 16
N_DIRS = 2
N_GATE_ROWS = 2 * N_DIRS
EPS = 1e-6
LOG2_E = 1.4426950408889634

ADALN_ROWS = 128
INPROJ_ROWS = 1024
OUTFFN_ROWS = 512
FF_COLS = 1024
AUG_ROWS = HEAD_DIM + BF16_SUBLANES

ROW_A, ROW_M, ROW_AINTER, ROW_EXPNEG, ROW_WK, ROW_DECAY = range(6)
N_COEF = 6
CHUNKS_PER_STEP = 32


def _const_spec(shape):
    zeros = (0,) * len(shape)
    return pl.BlockSpec(shape, lambda *_: zeros, pipeline_mode=pl.Buffered(1))


def _rms_scale(x):
    return x * lax.rsqrt(jnp.mean(x * x, axis=-1, keepdims=True) + EPS)


def _modulated_norm_bf16(x, gain, shift, scale):
    slope = (gain * (1.0 + scale)).astype(BF16)
    return _rms_scale(x).astype(BF16) * slope + shift.astype(BF16)


def _adaln_kernel(c_ref, w_ref, b_ref, o_ref):
    @pl.when(pl.program_id(0) == 0)
    def _():
        o_ref[...] = jnp.broadcast_to(b_ref[...], o_ref.shape)

    c = c_ref[...]
    act = c * jax.nn.sigmoid(c)
    o_ref[...] += jnp.dot(act.astype(BF16), w_ref[...].astype(BF16), preferred_element_type=F32)


def _adaln(c, w_ada, b_ada):
    batch, d = c.shape
    n = w_ada.shape[1]
    rows = ADALN_ROWS
    assert d % rows == 0
    return pl.pallas_call(
        _adaln_kernel,
        grid=(d // rows,),
        in_specs=[pl.BlockSpec((batch, rows), lambda j: (0, j)),
                  pl.BlockSpec((rows, n), lambda j: (j, 0)),
                  pl.BlockSpec((1, n), lambda j: (0, 0))],
        out_specs=pl.BlockSpec((batch, n), lambda j: (0, 0)),
        out_shape=jax.ShapeDtypeStruct((batch, n), F32),
        compiler_params=pltpu.CompilerParams(dimension_semantics=("arbitrary",),
                                             vmem_limit_bytes=VMEM_LIMIT_BYTES),
    )(c, w_ada, b_ada.reshape(1, n))


def _inproj_kernel(x_ref, mod_ref, g_ref, w_ref, wt_ref, *refs, n_casts):
    cast_in, cast_out = refs[:n_casts], refs[n_casts + 6:]
    qt_ref, k_ref, vt_ref, o_ref, u_ref, gates_ref = refs[n_casts:n_casts + 6]
    for src, dst in zip(cast_in, cast_out):
        dst[...] = src[...].astype(BF16)

    dm = N_HEADS * HEAD_DIM
    x = x_ref[0]
    shift = mod_ref[0, 0:1, :]
    scale = mod_ref[0, 1:2, :]
    hb = _modulated_norm_bf16(x, g_ref[...], shift, scale)

    by_token = jnp.dot(hb, w_ref[...], preferred_element_type=F32)
    for hd in range(N_HEADS):
        k_ref[0, hd] = by_token[:, hd * HEAD_DIM:(hd + 1) * HEAD_DIM].astype(BF16)
    o_ref[0] = by_token[:, dm:2 * dm].astype(BF16)
    u_ref[0] = by_token[:, 2 * dm:].astype(BF16)
    by_feature = lax.dot_general(wt_ref[...], hb, (((1,), (1,)), ((), ())), preferred_element_type=F32)
    for hd in range(N_HEADS):
        rows = slice(hd * HEAD_DIM, (hd + 1) * HEAD_DIM)
        qt_ref[0, hd] = (by_feature[rows, :] * (HEAD_DIM ** -0.5)).astype(BF16)
        vt_ref[0, hd, 0:HEAD_DIM, :] = by_feature[dm + hd * HEAD_DIM:dm + (hd + 1) * HEAD_DIM, :].astype(BF16)
        vt_ref[0, hd, HEAD_DIM:AUG_ROWS, :] = jnp.ones((AUG_ROWS - HEAD_DIM, x.shape[0]), BF16)
    gt = by_feature[2 * dm:, :]
    n_chunks = x.shape[0] // CHUNK
    for hd in range(N_HEADS):
        for r in range(N_GATE_ROWS):
            row = hd * N_GATE_ROWS + r
            for j in range(n_chunks):
                gates_ref[0, hd, r, j:j + 1, :] = gt[row:row + 1, j * CHUNK:(j + 1) * CHUNK]


def _inproj(x, mod, g_mix, w_cat, w_gates_t, d_pool, weights_to_cast):
    batch, seq, d = x.shape
    tm = INPROJ_ROWS
    dm = N_HEADS * HEAD_DIM
    grid = (batch, seq // tm)
    tiles_per_seq = seq // tm
    n_steps = batch * tiles_per_seq
    cast_specs = []
    for w in weights_to_cast:
        assert w.shape[0] % (n_steps * BF16_SUBLANES) == 0
        cast_specs.append(pl.BlockSpec((w.shape[0] // n_steps, w.shape[1]),
                                       lambda b, i: (b * tiles_per_seq + i, 0)))
    head_t = jax.ShapeDtypeStruct((batch, N_HEADS, HEAD_DIM, seq), BF16)
    head_n = jax.ShapeDtypeStruct((batch, N_HEADS, seq, HEAD_DIM), BF16)
    head_aug = jax.ShapeDtypeStruct((batch, N_HEADS, AUG_ROWS, seq), BF16)
    out_shape = (head_t, head_n, head_aug,
                 jax.ShapeDtypeStruct((batch, seq, dm), BF16),
                 jax.ShapeDtypeStruct((batch, seq, d_pool), BF16),
                 jax.ShapeDtypeStruct((batch, N_HEADS, N_GATE_ROWS, seq // CHUNK, CHUNK), F32))
    spec_t = pl.BlockSpec((1, N_HEADS, HEAD_DIM, tm), lambda b, i: (b, 0, 0, i))
    spec_n = pl.BlockSpec((1, N_HEADS, tm, HEAD_DIM), lambda b, i: (b, 0, i, 0))
    spec_aug = pl.BlockSpec((1, N_HEADS, AUG_ROWS, tm), lambda b, i: (b, 0, 0, i))
    out_specs = (spec_t, spec_n, spec_aug,
                 pl.BlockSpec((1, tm, dm), lambda b, i: (b, i, 0)),
                 pl.BlockSpec((1, tm, d_pool), lambda b, i: (b, i, 0)),
                 pl.BlockSpec((1, N_HEADS, N_GATE_ROWS, tm // CHUNK, CHUNK), lambda b, i: (b, 0, 0, i, 0)))
    outs = pl.pallas_call(
        functools.partial(_inproj_kernel, n_casts=len(weights_to_cast)),
        grid=grid,
        in_specs=[pl.BlockSpec((1, tm, d), lambda b, i: (b, i, 0)),
                  pl.BlockSpec((1,) + mod.shape[1:], lambda b, i: (b, 0, 0)),
                  _const_spec((1, d)),
                  _const_spec(w_cat.shape),
                  _const_spec(w_gates_t.shape)] + cast_specs,
        out_specs=list(out_specs) + cast_specs,
        out_shape=list(out_shape) + [jax.ShapeDtypeStruct(w.shape, BF16) for w in weights_to_cast],
        compiler_params=pltpu.CompilerParams(dimension_semantics=("arbitrary", "arbitrary"),
                                             vmem_limit_bytes=VMEM_LIMIT_BYTES),
    )(x, mod, g_mix.reshape(1, d), w_cat, w_gates_t, *weights_to_cast)
    return outs[:6], outs[6:]


def _time_scan(x, op, fill, reverse):
    row = lax.broadcasted_iota(jnp.int32, x.shape, 0)
    step = 1
    while step < CHUNK:
        if reverse:
            shifted = jnp.where(row < CHUNK - step, pltpu.roll(x, CHUNK - step, axis=0), fill)
        else:
            shifted = jnp.where(row >= step, pltpu.roll(x, step, axis=0), fill)
        x = op(x, shifted)
        step *= 2
    return x


def _chunk_carry(decay_log, peak, reverse):
    n = decay_log.shape[0]
    row = lax.broadcasted_iota(jnp.int32, decay_log.shape, 0)

    def earlier(x, dist, fill):
        if reverse:
            return jnp.where(row < n - dist, pltpu.roll(x, n - dist, axis=0), fill)
        return jnp.where(row >= dist, pltpu.roll(x, dist, axis=0), fill)

    p, q = decay_log, decay_log + peak
    dist = 1
    while dist < n:
        q = jnp.maximum(earlier(q, dist, -jnp.inf) + p, q)
        p = earlier(p, dist, 0.0) + p
        dist *= 2
    return earlier(jnp.maximum(p, q), 1, 0.0)


def _mlstm_kernel(qt_ref, k_ref, vt_ref, gates_ref, gbias_ref, wpool_ref, pscale_ref, woutp_ref, *refs,
                  n_casts, n_units):
    cast_in, out_ref, fold_ref = refs[:n_casts], refs[n_casts], refs[n_casts + 1]
    cast_out = refs[n_casts + 2:2 * n_casts + 2]
    coef_ref, sprev_ref, st_ref = refs[2 * n_casts + 2:]

    def weight_slabs():
        for src, dst in zip(cast_in, cast_out):
            dst[...] = src[...].astype(BF16)
        lhs = wpool_ref[...] * pscale_ref[...]
        rhs = woutp_ref[...]
        lhs_hi, rhs_hi = lhs.astype(BF16), rhs.astype(BF16)
        lhs_lo = (lhs - lhs_hi.astype(F32)).astype(BF16)
        rhs_lo = (rhs - rhs_hi.astype(F32)).astype(BF16)
        fold_ref[...] = (jnp.dot(lhs_hi, rhs_hi, preferred_element_type=F32)
                         + jnp.dot(lhs_hi, rhs_lo, preferred_element_type=F32)
                         + jnp.dot(lhs_lo, rhs_hi, preferred_element_type=F32)).astype(BF16)

    n_chunks = gates_ref.shape[3]
    assert 2 * N_DIRS * n_chunks == LANES
    step = pl.program_id(0)

    def prologue(slot):
        log_i, log_f = [], []
        for d in range(N_DIRS):
            log_i.append(gates_ref[0, 0, 2 * d] + gbias_ref[0, 2 * d:2 * d + 1, :])
            z = gates_ref[0, 0, 2 * d + 1] + gbias_ref[0, 2 * d + 1:2 * d + 2, :]
            log_f.append(-(jnp.maximum(-z, 0.0) + jnp.log1p(jnp.exp(-jnp.abs(z)))))
        by_time = jnp.concatenate(log_f + log_i, axis=0).T
        lane = lax.broadcasted_iota(jnp.int32, by_time.shape, 1)
        fwd_lane = (lane % (N_DIRS * n_chunks)) < n_chunks
        b_t = jnp.where(fwd_lane, _time_scan(by_time, jnp.add, 0.0, False),
                        _time_scan(by_time, jnp.add, 0.0, True))
        a_t = pltpu.roll(by_time, N_DIRS * n_chunks, axis=1) - b_t
        cm_t = jnp.where(fwd_lane, _time_scan(a_t, jnp.maximum, -jnp.inf, False),
                         _time_scan(a_t, jnp.maximum, -jnp.inf, True))
        b_rows, a_rows, cm_rows = b_t.T, a_t.T, cm_t.T

        for d in range(N_DIRS):
            reverse = d == 1
            last = 0 if reverse else CHUNK - 1
            b = b_rows[d * n_chunks:(d + 1) * n_chunks]
            a = a_rows[d * n_chunks:(d + 1) * n_chunks]
            cm = cm_rows[d * n_chunks:(d + 1) * n_chunks]
            b_last = jnp.broadcast_to(b[:, last:last + 1], b.shape)
            cm_last = jnp.broadcast_to(cm[:, last:last + 1], b.shape)
            m_prev = _chunk_carry(b_last, cm_last, reverse)
            m_row = jnp.maximum(m_prev, cm)
            a_inter = jnp.exp(m_prev - m_row)
            exp_neg = jnp.exp(-(b + m_row))
            m_end = jnp.broadcast_to(m_row[:, last:last + 1], b.shape)
            wk = jnp.exp(a - m_end)
            decay = jnp.broadcast_to(a_inter[:, last:last + 1], b.shape)
            for r, val in ((ROW_A, a * LOG2_E), (ROW_M, m_row * LOG2_E), (ROW_AINTER, a_inter),
                           (ROW_EXPNEG, exp_neg), (ROW_WK, wk), (ROW_DECAY, decay)):
                coef_ref[slot, d, r] = val

    def passes(slot):
        def coef_row(d, r, c):
            return coef_ref[slot, d, r, pl.ds(c, 1), :]

        def chunk_operands(c):
            off = pl.multiple_of(c * CHUNK, CHUNK)
            return off, qt_ref[0, 0, :, pl.ds(off, CHUNK)], k_ref[0, 0, pl.ds(off, CHUNK), :], \
                vt_ref[0, 0, :, pl.ds(off, CHUNK)]

        key_pos = lax.broadcasted_iota(jnp.int32, (CHUNK, CHUNK), 0)
        qry_pos = lax.broadcasted_iota(jnp.int32, (CHUNK, CHUNK), 1)
        visible = (key_pos <= qry_pos, key_pos >= qry_pos)

        def weighted_scores(c):
            _, qt_c, k_c, _ = chunk_operands(c)
            scores = jnp.dot(k_c, qt_c, preferred_element_type=F32)
            for d in range(N_DIRS):
                a_col = jnp.broadcast_to(coef_row(d, ROW_A, c), (CHUNK, CHUNK)).T
                weight = jnp.exp2(jnp.where(visible[d], a_col - coef_row(d, ROW_M, c), -jnp.inf))
                st_ref[d, c] = (scores * weight).astype(BF16)

        def state_step(d, c, state):
            _, _, k_c, vt_c = chunk_operands(c)
            sprev_ref[d, c] = state.astype(BF16)
            vw = vt_c * coef_row(d, ROW_WK, c).astype(BF16)
            return coef_row(d, ROW_DECAY, c) * state + jnp.dot(vw, k_c, preferred_element_type=F32)

        def state_body(i, states):
            s_fwd, s_bwd = states
            for j in range(CHUNKS_PER_STEP):
                c = i * CHUNKS_PER_STEP + j
                weighted_scores(c)
                s_fwd = state_step(0, c, s_fwd)
                s_bwd = state_step(1, n_chunks - 1 - c, s_bwd)
            return s_fwd, s_bwd

        zero_state = jnp.zeros((AUG_ROWS, HEAD_DIM), F32)
        lax.fori_loop(0, n_chunks // CHUNKS_PER_STEP, state_body, (zero_state, zero_state))

        def output_chunk(c):
            off, qt_c, _, vt_c = chunk_operands(c)
            hsum = None
            for d in range(N_DIRS):
                qa = qt_c * coef_row(d, ROW_AINTER, c).astype(BF16)
                lhs = jnp.concatenate([vt_c, sprev_ref[d, c]], axis=1)
                rhs = jnp.concatenate([st_ref[d, c], qa], axis=0)
                numden = jnp.dot(lhs, rhs, preferred_element_type=F32)
                den = numden[HEAD_DIM:HEAD_DIM + 1, :]
                ht = numden[0:HEAD_DIM, :] * (1.0 / jnp.maximum(jnp.abs(den), coef_row(d, ROW_EXPNEG, c)))
                hsum = ht if hsum is None else hsum + ht
            out_ref[0, 0, pl.ds(off, CHUNK), :] = hsum.T.astype(BF16)

        def output_body(i, carry):
            for j in range(CHUNKS_PER_STEP):
                output_chunk(i * CHUNKS_PER_STEP + j)
            return carry

        lax.fori_loop(0, n_chunks // CHUNKS_PER_STEP, output_body, 0)

    @pl.when(step == 0)
    def _():
        prologue(0)
        weight_slabs()

    @pl.when(step == n_units)
    def _():
        passes((n_units - 1) % 2)
        weight_slabs()

    for fill in range(2):
        @pl.when((step > 0) & (step < n_units) & (step % 2 == fill))
        def _():
            passes(1 - fill)
            prologue(fill)
            weight_slabs()


def _mlstm(qt, k, vt, gates, gbias, w_pool, pool_scale, w_out, weights_to_cast):
    batch, heads, hd, seq = qt.shape
    n_chunks = seq // CHUNK
    assert n_chunks % CHUNKS_PER_STEP == 0
    n_units = batch * heads
    groups, gdim, _ = w_pool.shape
    d = w_out.shape[1]
    first_pool_block = (w_out.shape[0] - groups * gdim) // gdim
    units_per_group = n_units // groups
    fold_rows = gdim // units_per_group
    assert n_units % groups == 0 and fold_rows % BF16_SUBLANES == 0

    def pass_unit(g):
        u = jnp.maximum(g - 1, 0)
        return u // heads, u % heads, 0, 0

    def prologue_unit(g):
        return jnp.minimum(g, n_units - 1)

    spec_t = pl.BlockSpec((1, 1, hd, seq), pass_unit)
    spec_n = pl.BlockSpec((1, 1, seq, hd), pass_unit)
    spec_aug = pl.BlockSpec((1, 1, AUG_ROWS, seq), pass_unit)
    cast_specs = []
    for w in weights_to_cast:
        assert w.shape[0] % (n_units * BF16_SUBLANES) == 0
        cast_specs.append(pl.BlockSpec((w.shape[0] // n_units, w.shape[1]), lambda g: (prologue_unit(g), 0)))
    outs = pl.pallas_call(
        functools.partial(_mlstm_kernel, n_casts=len(weights_to_cast), n_units=n_units),
        grid=(n_units + 1,),
        in_specs=[spec_t, spec_n, spec_aug,
                  pl.BlockSpec((1, 1, N_GATE_ROWS, n_chunks, CHUNK),
                               lambda g: (prologue_unit(g) // heads, prologue_unit(g) % heads, 0, 0, 0)),
                  pl.BlockSpec((1, N_GATE_ROWS, CHUNK), lambda g: (prologue_unit(g) % heads, 0, 0)),
                  pl.BlockSpec((fold_rows, gdim), lambda g: (prologue_unit(g), 0)),
                  pl.BlockSpec((1, gdim), lambda g: (0, prologue_unit(g) // units_per_group)),
                  pl.BlockSpec((gdim, d), lambda g: (first_pool_block + prologue_unit(g) // units_per_group, 0))]
        + cast_specs,
        out_specs=[spec_n, pl.BlockSpec((fold_rows, d), lambda g: (prologue_unit(g), 0))] + cast_specs,
        out_shape=[jax.ShapeDtypeStruct((batch, heads, seq, hd), BF16),
                   jax.ShapeDtypeStruct((groups * gdim, d), BF16)]
        + [jax.ShapeDtypeStruct(w.shape, BF16) for w in weights_to_cast],
        scratch_shapes=[pltpu.VMEM((2, N_DIRS, N_COEF, n_chunks, CHUNK), F32),
                        pltpu.VMEM((N_DIRS, n_chunks, AUG_ROWS, hd), BF16),
                        pltpu.VMEM((N_DIRS, n_chunks, CHUNK, CHUNK), BF16)],
        compiler_params=pltpu.CompilerParams(dimension_semantics=("arbitrary",),
                                             vmem_limit_bytes=VMEM_LIMIT_BYTES),
    )(qt, k, vt, gates, gbias, w_pool.reshape(groups * gdim, gdim), pool_scale.reshape(1, groups * gdim), w_out,
      *weights_to_cast)
    return outs[0], outs[1], outs[2:]


def _outffn_kernel(x_ref, hm_ref, o_ref, u_ref, uprev_ref, unext_ref, mod_ref,
                   ghead_ref, gffn_ref, gfinal_ref,
                   band_ref, wout_ref, wfold_ref, wff1_ref, wff2_ref, out_ref,
                   ubuf_ref, ycat_ref, *, seq, n_tiles, final_norm):
    tm = x_ref.shape[1]
    dm = N_HEADS * HEAD_DIM
    step = pl.program_id(0)
    tiles_per_seq = seq // tm
    tile = step % tiles_per_seq

    def matmul_stage():
        gate1 = mod_ref[0, 2:3, :]
        yproj = (jnp.dot(ycat_ref[:, 0:dm], wout_ref[0:dm, :], preferred_element_type=F32)
                 + jnp.dot(ycat_ref[:, dm:], wfold_ref[...], preferred_element_type=F32))
        x1 = x_ref[0] + gate1 * yproj

        shift2 = mod_ref[0, 3:4, :]
        scale2 = mod_ref[0, 4:5, :]
        gate2 = mod_ref[0, 5:6, :]
        h2 = ((_rms_scale(x1) * gffn_ref[...]) * (1.0 + scale2) + shift2).astype(BF16)
        acc = jnp.zeros(x1.shape, F32)
        for j in range(wff1_ref.shape[1] // FF_COLS):
            hid = jnp.dot(h2, wff1_ref[:, j * FF_COLS:(j + 1) * FF_COLS], preferred_element_type=F32)
            hid = jnp.square(jnp.maximum(hid, 0.0)).astype(BF16)
            acc = acc + jnp.dot(hid, wff2_ref[j * FF_COLS:(j + 1) * FF_COLS, :], preferred_element_type=F32)
        x2 = x1 + gate2 * acc
        if final_norm:
            x2 = _rms_scale(x2) * gfinal_ref[...]
        out_ref[0] = x2

    def elementwise_stage():
        for hd in range(N_HEADS):
            cols = slice(hd * HEAD_DIM, (hd + 1) * HEAD_DIM)
            hn = _rms_scale(hm_ref[0, hd].astype(F32)) * ghead_ref[:, cols]
            gate = jax.nn.sigmoid(o_ref[0, :, cols].astype(F32))
            ycat_ref[:, cols] = (gate * hn).astype(BF16)

        ubuf_ref[0:POOL_HALO, :] = jnp.where(tile > 0, uprev_ref[0], jnp.zeros_like(uprev_ref[0]))
        ubuf_ref[POOL_HALO:POOL_HALO + tm, :] = u_ref[0]
        ubuf_ref[POOL_HALO + tm:, :] = jnp.where(tile < tiles_per_seq - 1, unext_ref[0],
                                                 jnp.zeros_like(unext_ref[0]))
        for gi in range(len(POOL_WINDOWS)):
            cols = slice(gi * HEAD_DIM, (gi + 1) * HEAD_DIM)
            ext = jnp.concatenate([ubuf_ref[r0:r0 + POOL_BLOCK + 2 * POOL_HALO, cols]
                                   for r0 in range(0, tm, POOL_BLOCK)], axis=1)
            mixed = jnp.dot(band_ref[gi], ext, preferred_element_type=F32).astype(BF16)
            for blk, r0 in enumerate(range(0, tm, POOL_BLOCK)):
                ycat_ref[r0:r0 + POOL_BLOCK, dm + gi * HEAD_DIM:dm + (gi + 1) * HEAD_DIM] = (
                    mixed[:, blk * HEAD_DIM:(blk + 1) * HEAD_DIM])

    def sequence_end_rows():
        def clipped_rows(r0):
            ext = ubuf_ref[r0:r0 + 3 * POOL_HALO, :].astype(F32)
            pos = tile * tm + r0 + lax.broadcasted_iota(jnp.int32, (POOL_HALO, HEAD_DIM), 0)
            for gi, win in enumerate(POOL_WINDOWS):
                half = win // 2
                grp = ext[:, gi * HEAD_DIM:(gi + 1) * HEAD_DIM]
                total = grp[POOL_HALO - half:2 * POOL_HALO - half]
                for j in range(1 - half, half):
                    total = total + grp[POOL_HALO + j:2 * POOL_HALO + j]
                cnt = jnp.minimum(pos + half, seq) - jnp.maximum(pos - half, 0)
                mixed = total / cnt.astype(F32) - grp[POOL_HALO:2 * POOL_HALO]
                ycat_ref[r0:r0 + POOL_HALO, dm + gi * HEAD_DIM:dm + (gi + 1) * HEAD_DIM] = mixed.astype(BF16)

        @pl.when(tile == 0)
        def _():
            clipped_rows(0)

        @pl.when(tile == tiles_per_seq - 1)
        def _():
            clipped_rows(tm - POOL_HALO)

    @pl.when(step == 0)
    def _():
        elementwise_stage()
        sequence_end_rows()

    @pl.when(step > 0)
    def _():
        matmul_stage()
        elementwise_stage()
        sequence_end_rows()


def _pool_bands():
    row = np.arange(POOL_BLOCK)[:, None]
    col = np.arange(POOL_BLOCK + 2 * POOL_HALO)[None, :] - POOL_HALO
    bands = [((col >= row - win // 2) & (col < row + win // 2)) / win - (col == row) for win in POOL_WINDOWS]
    return jnp.asarray(np.stack(bands), dtype=BF16)


def _outffn(x, hm, o, u, mod, g_head, g_ffn, g_final, w_out, w_fold, w_ff1, w_ff2, final_norm):
    batch, seq, d = x.shape
    tm = OUTFFN_ROWS
    dm = N_HEADS * HEAD_DIM
    d_pool = u.shape[2]
    halo_per_tile = tm // POOL_HALO
    n_halo_blocks = seq // POOL_HALO
    tiles_per_seq = seq // tm
    n_tiles = batch * tiles_per_seq

    def mm_tile(g):
        t = jnp.maximum(g - 1, 0)
        return t // tiles_per_seq, t % tiles_per_seq

    def ew_tile(g):
        t = jnp.minimum(g, n_tiles - 1)
        return t // tiles_per_seq, t % tiles_per_seq

    def mm_rows(g):
        b, i = mm_tile(g)
        return b, i, 0

    def ew_rows(g):
        b, i = ew_tile(g)
        return b, i, 0

    def ew_heads(g):
        b, i = ew_tile(g)
        return b, 0, i, 0

    def ew_prev(g):
        b, i = ew_tile(g)
        return b, jnp.maximum(i * halo_per_tile - 1, 0), 0

    def ew_next(g):
        b, i = ew_tile(g)
        return b, jnp.minimum((i + 1) * halo_per_tile, n_halo_blocks - 1), 0

    bands = _pool_bands()
    kernel = functools.partial(_outffn_kernel, seq=seq, n_tiles=n_tiles, final_norm=final_norm)
    return pl.pallas_call(
        kernel,
        grid=(n_tiles + 1,),
        in_specs=[pl.BlockSpec((1, tm, d), mm_rows),
                  pl.BlockSpec((1, N_HEADS, tm, HEAD_DIM), ew_heads),
                  pl.BlockSpec((1, tm, dm), ew_rows),
                  pl.BlockSpec((1, tm, d_pool), ew_rows),
                  pl.BlockSpec((1, POOL_HALO, d_pool), ew_prev),
                  pl.BlockSpec((1, POOL_HALO, d_pool), ew_next),
                  pl.BlockSpec((1,) + mod.shape[1:], lambda g: (mm_tile(g)[0], 0, 0)),
                  _const_spec((1, dm)),
                  _const_spec((1, d)),
                  _const_spec((1, d)),
                  _const_spec(bands.shape),
                  _const_spec(w_out.shape),
                  _const_spec(w_fold.shape),
                  _const_spec(w_ff1.shape),
                  _const_spec(w_ff2.shape)],
        out_specs=pl.BlockSpec((1, tm, d), mm_rows),
        out_shape=jax.ShapeDtypeStruct((batch, seq, d), F32),
        scratch_shapes=[pltpu.VMEM((tm + 2 * POOL_HALO, d_pool), BF16),
                        pltpu.VMEM((tm, dm + d_pool), BF16)],
        compiler_params=pltpu.CompilerParams(dimension_semantics=("arbitrary",),
                                             vmem_limit_bytes=VMEM_LIMIT_BYTES),
    )(x, hm, o, u, u, u, mod, g_head.reshape(1, dm),
      g_ffn.reshape(1, d), g_final.reshape(1, d), bands, w_out, w_fold, w_ff1, w_ff2)


def _pack_w_in(w_in):
    dm = N_HEADS * HEAD_DIM
    n_gates = N_DIRS * 2 * N_HEADS
    q, k, v, o = (w_in[:, i * dm:(i + 1) * dm] for i in range(4))
    gates = w_in[:, 4 * dm:4 * dm + n_gates]
    pool = w_in[:, 4 * dm + n_gates:]
    gates_t = gates.reshape(-1, N_DIRS, 2, N_HEADS).transpose(3, 1, 2, 0).reshape(n_gates, -1)
    cols = jnp.concatenate([k, o, pool], axis=1).astype(BF16)
    rows = jnp.concatenate([q.T, v.T, gates_t], axis=0).astype(BF16)
    return cols, rows


def _pack_gate_bias(b_igate, b_fgate):
    rows = jnp.stack([b_igate, b_fgate], axis=1)
    rows = rows.transpose(2, 0, 1).reshape(N_HEADS, N_GATE_ROWS)
    return jnp.broadcast_to(rows[:, :, None], (N_HEADS, N_GATE_ROWS, CHUNK)).astype(F32)


def kernel(x, c, w_ada, b_ada, g_mix, w_in, b_igate, b_fgate, g_head, w_pool, pool_scale, w_out,
           g_ffn, w_ff1, w_ff2, g_final):
    depth = w_ada.shape[0]
    batch, seq, d = x.shape
    d_pool = pool_scale.shape[1]
    assert g_head.shape[1] == N_HEADS * HEAD_DIM and w_pool.shape[1] == len(POOL_WINDOWS)
    assert seq % INPROJ_ROWS == 0 and seq % OUTFFN_ROWS == 0 and w_ff1.shape[2] % FF_COLS == 0
    for l in range(depth):
        mod = _adaln(c, w_ada[l], b_ada[l]).reshape(batch, 6, d)
        (qt, k, vt, o, u, gates), (w_out_b, w_ff1_b) = _inproj(
            x, mod, g_mix[l], *_pack_w_in(w_in[l]), d_pool, (w_out[l], w_ff1[l]))
        hm, w_fold, (w_ff2_b,) = _mlstm(
            qt, k, vt, gates, _pack_gate_bias(b_igate[l], b_fgate[l]), w_pool[l], pool_scale[l], w_out[l],
            (w_ff2[l],))
        x = _outffn(x, hm, o, u, mod, g_head[l], g_ffn[l], g_final,
                    w_out_b, w_fold, w_ff1_b, w_ff2_b, final_norm=(l == depth - 1))
    return x
```

```python
import functools

import jax
import jax.numpy as jnp
import numpy as np
from jax import lax
from jax.experimental import pallas as pl
from jax.experimental.pallas import tpu as pltpu

F32 = jnp.float32
BF16 = jnp.bfloat16

LANES = 128
BF16_SUBLANES = 16
VMEM_LIMIT_BYTES = 56 * 1024 * 1024

N_HEADS = 4
HEAD_DIM = 128
CHUNK = 128
POOL_WINDOWS = (2, 4, 8, 16)
POOL_BLOCK = 128
POOL_HALO = 16
N_DIRS = 2
N_GATE_ROWS = 2 * N_DIRS
EPS = 1e-6
LOG2_E = 1.4426950408889634

ADALN_ROWS = 128
INPROJ_ROWS = 1024
OUTFFN_ROWS = 512
FF_COLS = 1024
OUT_ROWS = 256
AUG_ROWS = HEAD_DIM + BF16_SUBLANES

ROW_A, ROW_M, ROW_AINTER, ROW_EXPNEG, ROW_WK, ROW_DECAY = range(6)
N_COEF = 6
CHUNKS_PER_STEP = 32


def _const_spec(shape):
    zeros = (0,) * len(shape)
    return pl.BlockSpec(shape, lambda *_: zeros, pipeline_mode=pl.Buffered(1))


def _rms_scale(x):
    return x * lax.rsqrt(jnp.mean(x * x, axis=-1, keepdims=True) + EPS)


def _modulated_norm_bf16(x, gain, shift, scale):
    slope = (gain * (1.0 + scale)).astype(BF16)
    return _rms_scale(x).astype(BF16) * slope + shift.astype(BF16)


def _adaln_kernel(c_ref, w_ref, b_ref, o_ref):
    @pl.when(pl.program_id(0) == 0)
    def _():
        o_ref[...] = jnp.broadcast_to(b_ref[...], o_ref.shape)

    c = c_ref[...]
    act = c * jax.nn.sigmoid(c)
    o_ref[...] += jnp.dot(act.astype(BF16), w_ref[...].astype(BF16), preferred_element_type=F32)


def _adaln(c, w_ada, b_ada):
    batch, d = c.shape
    n = w_ada.shape[1]
    rows = ADALN_ROWS
    assert d % rows == 0
    return pl.pallas_call(
        _adaln_kernel,
        grid=(d // rows,),
        in_specs=[pl.BlockSpec((batch, rows), lambda j: (0, j)),
                  pl.BlockSpec((rows, n), lambda j: (j, 0)),
                  pl.BlockSpec((1, n), lambda j: (0, 0))],
        out_specs=pl.BlockSpec((batch, n), lambda j: (0, 0)),
        out_shape=jax.ShapeDtypeStruct((batch, n), F32),
        compiler_params=pltpu.CompilerParams(dimension_semantics=("arbitrary",),
                                             vmem_limit_bytes=VMEM_LIMIT_BYTES),
    )(c, w_ada, b_ada.reshape(1, n))


def _inproj_kernel(x_ref, mod_ref, g_ref, w_ref, wt_ref, *refs, n_casts):
    cast_in, cast_out = refs[:n_casts], refs[n_casts + 6:]
    qt_ref, k_ref, vt_ref, o_ref, u_ref, gates_ref = refs[n_casts:n_casts + 6]
    for src, dst in zip(cast_in, cast_out):
        dst[...] = src[...].astype(BF16)

    dm = N_HEADS * HEAD_DIM
    x = x_ref[0]
    shift = mod_ref[0, 0:1, :]
    scale = mod_ref[0, 1:2, :]
    hb = _modulated_norm_bf16(x, g_ref[...], shift, scale)

    by_token = jnp.dot(hb, w_ref[...], preferred_element_type=F32)
    for hd in range(N_HEADS):
        k_ref[0, hd] = by_token[:, hd * HEAD_DIM:(hd + 1) * HEAD_DIM].astype(BF16)
    o_ref[0] = by_token[:, dm:2 * dm].astype(BF16)
    u_ref[0] = by_token[:, 2 * dm:].astype(BF16)
    by_feature = lax.dot_general(wt_ref[...], hb, (((1,), (1,)), ((), ())), preferred_element_type=F32)
    for hd in range(N_HEADS):
        rows = slice(hd * HEAD_DIM, (hd + 1) * HEAD_DIM)
        qt_ref[0, hd] = (by_feature[rows, :] * (HEAD_DIM ** -0.5)).astype(BF16)
        vt_ref[0, hd, 0:HEAD_DIM, :] = by_feature[dm + hd * HEAD_DIM:dm + (hd + 1) * HEAD_DIM, :].astype(BF16)
        vt_ref[0, hd, HEAD_DIM:AUG_ROWS, :] = jnp.ones((AUG_ROWS - HEAD_DIM, x.shape[0]), BF16)
    gt = by_feature[2 * dm:, :]
    n_chunks = x.shape[0] // CHUNK
    for hd in range(N_HEADS):
        for r in range(N_GATE_ROWS):
            row = hd * N_GATE_ROWS + r
            for j in range(n_chunks):
                gates_ref[0, hd, r, j:j + 1, :] = gt[row:row + 1, j * CHUNK:(j + 1) * CHUNK]


def _inproj(x, mod, g_mix, w_cat, w_gates_t, d_pool, weights_to_cast):
    batch, seq, d = x.shape
    tm = INPROJ_ROWS
    dm = N_HEADS * HEAD_DIM
    grid = (batch, seq // tm)
    tiles_per_seq = seq // tm
    n_steps = batch * tiles_per_seq
    cast_specs = []
    for w in weights_to_cast:
        assert w.shape[0] % (n_steps * BF16_SUBLANES) == 0
        cast_specs.append(pl.BlockSpec((w.shape[0] // n_steps, w.shape[1]),
                                       lambda b, i: (b * tiles_per_seq + i, 0)))
    head_t = jax.ShapeDtypeStruct((batch, N_HEADS, HEAD_DIM, seq), BF16)
    head_n = jax.ShapeDtypeStruct((batch, N_HEADS, seq, HEAD_DIM), BF16)
    head_aug = jax.ShapeDtypeStruct((batch, N_HEADS, AUG_ROWS, seq), BF16)
    out_shape = (head_t, head_n, head_aug,
                 jax.ShapeDtypeStruct((batch, seq, dm), BF16),
                 jax.ShapeDtypeStruct((batch, seq, d_pool), BF16),
                 jax.ShapeDtypeStruct((batch, N_HEADS, N_GATE_ROWS, seq // CHUNK, CHUNK), F32))
    spec_t = pl.BlockSpec((1, N_HEADS, HEAD_DIM, tm), lambda b, i: (b, 0, 0, i))
    spec_n = pl.BlockSpec((1, N_HEADS, tm, HEAD_DIM), lambda b, i: (b, 0, i, 0))
    spec_aug = pl.BlockSpec((1, N_HEADS, AUG_ROWS, tm), lambda b, i: (b, 0, 0, i))
    out_specs = (spec_t, spec_n, spec_aug,
                 pl.BlockSpec((1, tm, dm), lambda b, i: (b, i, 0)),
                 pl.BlockSpec((1, tm, d_pool), lambda b, i: (b, i, 0)),
                 pl.BlockSpec((1, N_HEADS, N_GATE_ROWS, tm // CHUNK, CHUNK), lambda b, i: (b, 0, 0, i, 0)))
    outs = pl.pallas_call(
        functools.partial(_inproj_kernel, n_casts=len(weights_to_cast)),
        grid=grid,
        in_specs=[pl.BlockSpec((1, tm, d), lambda b, i: (b, i, 0)),
                  pl.BlockSpec((1,) + mod.shape[1:], lambda b, i: (b, 0, 0)),
                  _const_spec((1, d)),
                  _const_spec(w_cat.shape),
                  _const_spec(w_gates_t.shape)] + cast_specs,
        out_specs=list(out_specs) + cast_specs,
        out_shape=list(out_shape) + [jax.ShapeDtypeStruct(w.shape, BF16) for w in weights_to_cast],
        compiler_params=pltpu.CompilerParams(dimension_semantics=("arbitrary", "arbitrary"),
                                             vmem_limit_bytes=VMEM_LIMIT_BYTES),
    )(x, mod, g_mix.reshape(1, d), w_cat, w_gates_t, *weights_to_cast)
    return outs[:6], outs[6:]


def _time_scan(x, op, fill, reverse):
    row = lax.broadcasted_iota(jnp.int32, x.shape, 0)
    step = 1
    while step < CHUNK:
        if reverse:
            shifted = jnp.where(row < CHUNK - step, pltpu.roll(x, CHUNK - step, axis=0), fill)
        else:
            shifted = jnp.where(row >= step, pltpu.roll(x, step, axis=0), fill)
        x = op(x, shifted)
        step *= 2
    return x


def _chunk_carry(decay_log, peak, reverse):
    n = decay_log.shape[0]
    row = lax.broadcasted_iota(jnp.int32, decay_log.shape, 0)

    def earlier(x, dist, fill):
        if reverse:
            return jnp.where(row < n - dist, pltpu.roll(x, n - dist, axis=0), fill)
        return jnp.where(row >= dist, pltpu.roll(x, dist, axis=0), fill)

    p, q = decay_log, decay_log + peak
    dist = 1
    while dist < n:
        q = jnp.maximum(earlier(q, dist, -jnp.inf) + p, q)
        p = earlier(p, dist, 0.0) + p
        dist *= 2
    return earlier(jnp.maximum(p, q), 1, 0.0)


def _mlstm_kernel(qt_ref, k_ref, vt_ref, gates_ref, gbias_ref, wpool_ref, pscale_ref, woutp_ref, *refs,
                  n_casts, n_units):
    cast_in, out_ref, fold_ref = refs[:n_casts], refs[n_casts], refs[n_casts + 1]
    cast_out = refs[n_casts + 2:2 * n_casts + 2]
    coef_ref, sprev_ref, st_ref = refs[2 * n_casts + 2:]

    def weight_slabs():
        for src, dst in zip(cast_in, cast_out):
            dst[...] = src[...].astype(BF16)
        lhs = wpool_ref[...] * pscale_ref[...]
        rhs = woutp_ref[...]
        lhs_hi, rhs_hi = lhs.astype(BF16), rhs.astype(BF16)
        lhs_lo = (lhs - lhs_hi.astype(F32)).astype(BF16)
        rhs_lo = (rhs - rhs_hi.astype(F32)).astype(BF16)
        fold_ref[...] = (jnp.dot(lhs_hi, rhs_hi, preferred_element_type=F32)
                         + jnp.dot(lhs_hi, rhs_lo, preferred_element_type=F32)
                         + jnp.dot(lhs_lo, rhs_hi, preferred_element_type=F32)).astype(BF16)

    n_chunks = gates_ref.shape[3]
    assert 2 * N_DIRS * n_chunks == LANES
    step = pl.program_id(0)

    def prologue(slot):
        log_i, log_f = [], []
        for d in range(N_DIRS):
            log_i.append(gates_ref[0, 0, 2 * d] + gbias_ref[0, 2 * d:2 * d + 1, :])
            z = gates_ref[0, 0, 2 * d + 1] + gbias_ref[0, 2 * d + 1:2 * d + 2, :]
            log_f.append(-(jnp.maximum(-z, 0.0) + jnp.log1p(jnp.exp(-jnp.abs(z)))))
        by_time = jnp.concatenate(log_f + log_i, axis=0).T
        lane = lax.broadcasted_iota(jnp.int32, by_time.shape, 1)
        fwd_lane = (lane % (N_DIRS * n_chunks)) < n_chunks
        b_t = jnp.where(fwd_lane, _time_scan(by_time, jnp.add, 0.0, False),
                        _time_scan(by_time, jnp.add, 0.0, True))
        a_t = pltpu.roll(by_time, N_DIRS * n_chunks, axis=1) - b_t
        cm_t = jnp.where(fwd_lane, _time_scan(a_t, jnp.maximum, -jnp.inf, False),
                         _time_scan(a_t, jnp.maximum, -jnp.inf, True))
        b_rows, a_rows, cm_rows = b_t.T, a_t.T, cm_t.T

        for d in range(N_DIRS):
            reverse = d == 1
            last = 0 if reverse else CHUNK - 1
            b = b_rows[d * n_chunks:(d + 1) * n_chunks]
            a = a_rows[d * n_chunks:(d + 1) * n_chunks]
            cm = cm_rows[d * n_chunks:(d + 1) * n_chunks]
            b_last = jnp.broadcast_to(b[:, last:last + 1], b.shape)
            cm_last = jnp.broadcast_to(cm[:, last:last + 1], b.shape)
            m_prev = _chunk_carry(b_last, cm_last, reverse)
            m_row = jnp.maximum(m_prev, cm)
            a_inter = jnp.exp(m_prev - m_row)
            exp_neg = jnp.exp(-(b + m_row))
            m_end = jnp.broadcast_to(m_row[:, last:last + 1], b.shape)
            wk = jnp.exp(a - m_end)
            decay = jnp.broadcast_to(a_inter[:, last:last + 1], b.shape)
            for r, val in ((ROW_A, a * LOG2_E), (ROW_M, m_row * LOG2_E), (ROW_AINTER, a_inter),
                           (ROW_EXPNEG, exp_neg), (ROW_WK, wk), (ROW_DECAY, decay)):
                coef_ref[slot, d, r] = val

    def passes(slot):
        def coef_row(d, r, c):
            return coef_ref[slot, d, r, pl.ds(c, 1), :]

        def chunk_operands(c):
            off = pl.multiple_of(c * CHUNK, CHUNK)
            return off, qt_ref[0, 0, :, pl.ds(off, CHUNK)], k_ref[0, 0, pl.ds(off, CHUNK), :], \
                vt_ref[0, 0, :, pl.ds(off, CHUNK)]

        key_pos = lax.broadcasted_iota(jnp.int32, (CHUNK, CHUNK), 0)
        qry_pos = lax.broadcasted_iota(jnp.int32, (CHUNK, CHUNK), 1)
        visible = (key_pos <= qry_pos, key_pos >= qry_pos)

        def weighted_scores(c):
            _, qt_c, k_c, _ = chunk_operands(c)
            scores = jnp.dot(k_c, qt_c, preferred_element_type=F32)
            for d in range(N_DIRS):
                a_col = jnp.broadcast_to(coef_row(d, ROW_A, c), (CHUNK, CHUNK)).T
                weight = jnp.exp2(jnp.where(visible[d], a_col - coef_row(d, ROW_M, c), -jnp.inf))
                st_ref[d, c] = (scores * weight).astype(BF16)

        def state_step(d, c, state):
            _, _, k_c, vt_c = chunk_operands(c)
            sprev_ref[d, c] = state.astype(BF16)
            vw = vt_c * coef_row(d, ROW_WK, c).astype(BF16)
            return coef_row(d, ROW_DECAY, c) * state + jnp.dot(vw, k_c, preferred_element_type=F32)

        def state_body(i, states):
            s_fwd, s_bwd = states
            for j in range(CHUNKS_PER_STEP):
                c = i * CHUNKS_PER_STEP + j
                weighted_scores(c)
                s_fwd = state_step(0, c, s_fwd)
                s_bwd = state_step(1, n_chunks - 1 - c, s_bwd)
            return s_fwd, s_bwd

        zero_state = jnp.zeros((AUG_ROWS, HEAD_DIM), F32)
        lax.fori_loop(0, n_chunks // CHUNKS_PER_STEP, state_body, (zero_state, zero_state))

        def output_chunk(c):
            off, qt_c, _, vt_c = chunk_operands(c)
            hsum = None
            for d in range(N_DIRS):
                qa = qt_c * coef_row(d, ROW_AINTER, c).astype(BF16)
                lhs = jnp.concatenate([vt_c, sprev_ref[d, c]], axis=1)
                rhs = jnp.concatenate([st_ref[d, c], qa], axis=0)
                numden = jnp.dot(lhs, rhs, preferred_element_type=F32)
                den = numden[HEAD_DIM:HEAD_DIM + 1, :]
                ht = numden[0:HEAD_DIM, :] * (1.0 / jnp.maximum(jnp.abs(den), coef_row(d, ROW_EXPNEG, c)))
                hsum = ht if hsum is None else hsum + ht
            out_ref[0, 0, pl.ds(off, CHUNK), :] = hsum.T.astype(BF16)

        def output_body(i, carry):
            for j in range(CHUNKS_PER_STEP):
                output_chunk(i * CHUNKS_PER_STEP + j)
            return carry

        lax.fori_loop(0, n_chunks // CHUNKS_PER_STEP, output_body, 0)

    @pl.when(step == 0)
    def _():
        prologue(0)
        weight_slabs()

    @pl.when(step == n_units)
    def _():
        passes((n_units - 1) % 2)
        weight_slabs()

    for fill in range(2):
        @pl.when((step > 0) & (step < n_units) & (step % 2 == fill))
        def _():
            passes(1 - fill)
            prologue(fill)
            weight_slabs()


def _mlstm(qt, k, vt, gates, gbias, w_pool, pool_scale, w_out, weights_to_cast):
    batch, heads, hd, seq = qt.shape
    n_chunks = seq // CHUNK
    assert n_chunks % CHUNKS_PER_STEP == 0
    n_units = batch * heads
    groups, gdim, _ = w_pool.shape
    d = w_out.shape[1]
    first_pool_block = (w_out.shape[0] - groups * gdim) // gdim
    units_per_group = n_units // groups
    fold_rows = gdim // units_per_group
    assert n_units % groups == 0 and fold_rows % BF16_SUBLANES == 0

    def pass_unit(g):
        u = jnp.maximum(g - 1, 0)
        return u // heads, u % heads, 0, 0

    def prologue_unit(g):
        return jnp.minimum(g, n_units - 1)

    spec_t = pl.BlockSpec((1, 1, hd, seq), pass_unit)
    spec_n = pl.BlockSpec((1, 1, seq, hd), pass_unit)
    spec_aug = pl.BlockSpec((1, 1, AUG_ROWS, seq), pass_unit)
    cast_specs = []
    for w in weights_to_cast:
        assert w.shape[0] % (n_units * BF16_SUBLANES) == 0
        cast_specs.append(pl.BlockSpec((w.shape[0] // n_units, w.shape[1]), lambda g: (prologue_unit(g), 0)))
    outs = pl.pallas_call(
        functools.partial(_mlstm_kernel, n_casts=len(weights_to_cast), n_units=n_units),
        grid=(n_units + 1,),
        in_specs=[spec_t, spec_n, spec_aug,
                  pl.BlockSpec((1, 1, N_GATE_ROWS, n_chunks, CHUNK),
                               lambda g: (prologue_unit(g) // heads, prologue_unit(g) % heads, 0, 0, 0)),
                  pl.BlockSpec((1, N_GATE_ROWS, CHUNK), lambda g: (prologue_unit(g) % heads, 0, 0)),
                  pl.BlockSpec((fold_rows, gdim), lambda g: (prologue_unit(g), 0)),
                  pl.BlockSpec((1, gdim), lambda g: (0, prologue_unit(g) // units_per_group)),
                  pl.BlockSpec((gdim, d), lambda g: (first_pool_block + prologue_unit(g) // units_per_group, 0))]
        + cast_specs,
        out_specs=[spec_n, pl.BlockSpec((fold_rows, d), lambda g: (prologue_unit(g), 0))] + cast_specs,
        out_shape=[jax.ShapeDtypeStruct((batch, heads, seq, hd), BF16),
                   jax.ShapeDtypeStruct((groups * gdim, d), BF16)]
        + [jax.ShapeDtypeStruct(w.shape, BF16) for w in weights_to_cast],
        scratch_shapes=[pltpu.VMEM((2, N_DIRS, N_COEF, n_chunks, CHUNK), F32),
                        pltpu.VMEM((N_DIRS, n_chunks, AUG_ROWS, hd), BF16),
                        pltpu.VMEM((N_DIRS, n_chunks, CHUNK, CHUNK), BF16)],
        compiler_params=pltpu.CompilerParams(dimension_semantics=("arbitrary",),
                                             vmem_limit_bytes=VMEM_LIMIT_BYTES),
    )(qt, k, vt, gates, gbias, w_pool.reshape(groups * gdim, gdim), pool_scale.reshape(1, groups * gdim), w_out,
      *weights_to_cast)
    return outs[0], outs[1], outs[2:]


def _outffn_kernel(x_ref, hm_ref, o_ref, u_ref, uprev_ref, unext_ref, mod_ref,
                   ghead_ref, gffn_ref, gfinal_ref,
                   band_ref, wout_ref, wfold_ref, wff1_ref, wff2_ref, out_ref,
                   ubuf_ref, ycat_ref, *, seq, n_tiles, final_norm):
    tm = x_ref.shape[1]
    dm = N_HEADS * HEAD_DIM
    step = pl.program_id(0)
    tiles_per_seq = seq // tm
    tile = step % tiles_per_seq

    def matmul_stage():
        gate1 = mod_ref[0, 2:3, :]
        shift2 = mod_ref[0, 3:4, :]
        scale2 = mod_ref[0, 4:5, :]
        gate2 = mod_ref[0, 5:6, :]
        x1_parts, h2_parts = [], []
        for r0 in range(0, tm, tm // 2):
            rows = slice(r0, r0 + tm // 2)
            yproj = (jnp.dot(ycat_ref[rows, 0:dm], wout_ref[0:dm, :], preferred_element_type=F32)
                     + jnp.dot(ycat_ref[rows, dm:], wfold_ref[...], preferred_element_type=F32))
            x1_parts.append(x_ref[0, rows, :] + gate1 * yproj)
            h2_parts.append(((_rms_scale(x1_parts[-1]) * gffn_ref[...]) * (1.0 + scale2) + shift2).astype(BF16))
        x1 = jnp.concatenate(x1_parts, axis=0)
        h2 = jnp.concatenate(h2_parts, axis=0)

        n_slabs = wff1_ref.shape[1] // FF_COLS
        acc = jnp.zeros(x1.shape, F32)
        for j in range(n_slabs):
            hid = jnp.dot(h2, wff1_ref[:, j * FF_COLS:(j + 1) * FF_COLS], preferred_element_type=F32)
            hid = jnp.square(jnp.maximum(hid, 0.0)).astype(BF16)
            w2 = wff2_ref[j * FF_COLS:(j + 1) * FF_COLS, :]
            if j < n_slabs - 1:
                acc = acc + jnp.dot(hid, w2, preferred_element_type=F32)
                continue
            for r0 in range(0, tm, OUT_ROWS):
                rows = slice(r0, r0 + OUT_ROWS)
                x2 = x1[rows] + gate2 * (acc[rows] + jnp.dot(hid[rows], w2, preferred_element_type=F32))
                if final_norm:
                    x2 = _rms_scale(x2) * gfinal_ref[...]
                out_ref[0, rows, :] = x2

    def elementwise_stage():
        for hd in range(N_HEADS):
            cols = slice(hd * HEAD_DIM, (hd + 1) * HEAD_DIM)
            hn = _rms_scale(hm_ref[0, hd].astype(F32)) * ghead_ref[:, cols]
            gate = jax.nn.sigmoid(o_ref[0, :, cols].astype(F32))
            ycat_ref[:, cols] = (gate * hn).astype(BF16)

        ubuf_ref[0:POOL_HALO, :] = jnp.where(tile > 0, uprev_ref[0], jnp.zeros_like(uprev_ref[0]))
        ubuf_ref[POOL_HALO:POOL_HALO + tm, :] = u_ref[0]
        ubuf_ref[POOL_HALO + tm:, :] = jnp.where(tile < tiles_per_seq - 1, unext_ref[0],
                                                 jnp.zeros_like(unext_ref[0]))
        for gi in range(len(POOL_WINDOWS)):
            cols = slice(gi * HEAD_DIM, (gi + 1) * HEAD_DIM)
            ext = jnp.concatenate([ubuf_ref[r0:r0 + POOL_BLOCK + 2 * POOL_HALO, cols]
                                   for r0 in range(0, tm, POOL_BLOCK)], axis=1)
            mixed = jnp.dot(band_ref[gi], ext, preferred_element_type=F32).astype(BF16)
            for blk, r0 in enumerate(range(0, tm, POOL_BLOCK)):
                ycat_ref[r0:r0 + POOL_BLOCK, dm + gi * HEAD_DIM:dm + (gi + 1) * HEAD_DIM] = (
                    mixed[:, blk * HEAD_DIM:(blk + 1) * HEAD_DIM])

    def sequence_end_rows():
        def clipped_rows(r0):
            ext = ubuf_ref[r0:r0 + 3 * POOL_HALO, :].astype(F32)
            pos = tile * tm + r0 + lax.broadcasted_iota(jnp.int32, (POOL_HALO, HEAD_DIM), 0)
            for gi, win in enumerate(POOL_WINDOWS):
                half = win // 2
                grp = ext[:, gi * HEAD_DIM:(gi + 1) * HEAD_DIM]
                total = grp[POOL_HALO - half:2 * POOL_HALO - half]
                for j in range(1 - half, half):
                    total = total + grp[POOL_HALO + j:2 * POOL_HALO + j]
                cnt = jnp.minimum(pos + half, seq) - jnp.maximum(pos - half, 0)
                mixed = total / cnt.astype(F32) - grp[POOL_HALO:2 * POOL_HALO]
                ycat_ref[r0:r0 + POOL_HALO, dm + gi * HEAD_DIM:dm + (gi + 1) * HEAD_DIM] = mixed.astype(BF16)

        @pl.when(tile == 0)
        def _():
            clipped_rows(0)

        @pl.when(tile == tiles_per_seq - 1)
        def _():
            clipped_rows(tm - POOL_HALO)

    @pl.when(step == 0)
    def _():
        elementwise_stage()
        sequence_end_rows()

    @pl.when(step > 0)
    def _():
        matmul_stage()
        elementwise_stage()
        sequence_end_rows()


def _pool_bands():
    row = np.arange(POOL_BLOCK)[:, None]
    col = np.arange(POOL_BLOCK + 2 * POOL_HALO)[None, :] - POOL_HALO
    bands = [((col >= row - win // 2) & (col < row + win // 2)) / win - (col == row) for win in POOL_WINDOWS]
    return jnp.asarray(np.stack(bands), dtype=BF16)


def _outffn(x, hm, o, u, mod, g_head, g_ffn, g_final, w_out, w_fold, w_ff1, w_ff2, final_norm):
    batch, seq, d = x.shape
    tm = OUTFFN_ROWS
    dm = N_HEADS * HEAD_DIM
    d_pool = u.shape[2]
    halo_per_tile = tm // POOL_HALO
    n_halo_blocks = seq // POOL_HALO
    tiles_per_seq = seq // tm
    n_tiles = batch * tiles_per_seq

    def mm_tile(g):
        t = jnp.maximum(g - 1, 0)
        return t // tiles_per_seq, t % tiles_per_seq

    def ew_tile(g):
        t = jnp.minimum(g, n_tiles - 1)
        return t // tiles_per_seq, t % tiles_per_seq

    def mm_rows(g):
        b, i = mm_tile(g)
        return b, i, 0

    def ew_rows(g):
        b, i = ew_tile(g)
        return b, i, 0

    def ew_heads(g):
        b, i = ew_tile(g)
        return b, 0, i, 0

    def ew_prev(g):
        b, i = ew_tile(g)
        return b, jnp.maximum(i * halo_per_tile - 1, 0), 0

    def ew_next(g):
        b, i = ew_tile(g)
        return b, jnp.minimum((i + 1) * halo_per_tile, n_halo_blocks - 1), 0

    bands = _pool_bands()
    kernel = functools.partial(_outffn_kernel, seq=seq, n_tiles=n_tiles, final_norm=final_norm)
    return pl.pallas_call(
        kernel,
        grid=(n_tiles + 1,),
        in_specs=[pl.BlockSpec((1, tm, d), mm_rows),
                  pl.BlockSpec((1, N_HEADS, tm, HEAD_DIM), ew_heads),
                  pl.BlockSpec((1, tm, dm), ew_rows),
                  pl.BlockSpec((1, tm, d_pool), ew_rows),
                  pl.BlockSpec((1, POOL_HALO, d_pool), ew_prev),
                  pl.BlockSpec((1, POOL_HALO, d_pool), ew_next),
                  pl.BlockSpec((1,) + mod.shape[1:], lambda g: (mm_tile(g)[0], 0, 0)),
                  _const_spec((1, dm)),
                  _const_spec((1, d)),
                  _const_spec((1, d)),
                  _const_spec(bands.shape),
                  _const_spec(w_out.shape),
                  _const_spec(w_fold.shape),
                  _const_spec(w_ff1.shape),
                  _const_spec(w_ff2.shape)],
        out_specs=pl.BlockSpec((1, tm, d), mm_rows),
        out_shape=jax.ShapeDtypeStruct((batch, seq, d), F32),
        scratch_shapes=[pltpu.VMEM((tm + 2 * POOL_HALO, d_pool), BF16),
                        pltpu.VMEM((tm, dm + d_pool), BF16)],
        compiler_params=pltpu.CompilerParams(dimension_semantics=("arbitrary",),
                                             vmem_limit_bytes=VMEM_LIMIT_BYTES),
    )(x, hm, o, u, u, u, mod, g_head.reshape(1, dm),
      g_ffn.reshape(1, d), g_final.reshape(1, d), bands, w_out, w_fold, w_ff1, w_ff2)


def _pack_w_in(w_in):
    dm = N_HEADS * HEAD_DIM
    n_gates = N_DIRS * 2 * N_HEADS
    q, k, v, o = (w_in[:, i * dm:(i + 1) * dm] for i in range(4))
    gates = w_in[:, 4 * dm:4 * dm + n_gates]
    pool = w_in[:, 4 * dm + n_gates:]
    gates_t = gates.reshape(-1, N_DIRS, 2, N_HEADS).transpose(3, 1, 2, 0).reshape(n_gates, -1)
    cols = jnp.concatenate([k, o, pool], axis=1).astype(BF16)
    rows = jnp.concatenate([q.T, v.T, gates_t], axis=0).astype(BF16)
    return cols, rows


def _pack_gate_bias(b_igate, b_fgate):
    rows = jnp.stack([b_igate, b_fgate], axis=1)
    rows = rows.transpose(2, 0, 1).reshape(N_HEADS, N_GATE_ROWS)
    return jnp.broadcast_to(rows[:, :, None], (N_HEADS, N_GATE_ROWS, CHUNK)).astype(F32)


def kernel(x, c, w_ada, b_ada, g_mix, w_in, b_igate, b_fgate, g_head, w_pool, pool_scale, w_out,
           g_ffn, w_ff1, w_ff2, g_final):
    depth = w_ada.shape[0]
    batch, seq, d = x.shape
    d_pool = pool_scale.shape[1]
    assert g_head.shape[1] == N_HEADS * HEAD_DIM and w_pool.shape[1] == len(POOL_WINDOWS)
    assert seq % INPROJ_ROWS == 0 and seq % OUTFFN_ROWS == 0 and w_ff1.shape[2] % FF_COLS == 0
    for l in range(depth):
        mod = _adaln(c, w_ada[l], b_ada[l]).reshape(batch, 6, d)
        (qt, k, vt, o, u, gates), (w_out_b, w_ff1_b) = _inproj(
            x, mod, g_mix[l], *_pack_w_in(w_in[l]), d_pool, (w_out[l], w_ff1[l]))
        hm, w_fold, (w_ff2_b,) = _mlstm(
            qt, k, vt, gates, _pack_gate_bias(b_igate[l], b_fgate[l]), w_pool[l], pool_scale[l], w_out[l],
            (w_ff2[l],))
        x = _outffn(x, hm, o, u, mod, g_head[l], g_ffn[l], g_final,
                    w_out_b, w_fold, w_ff1_b, w_ff2_b, final_norm=(l == depth - 1))
    return x
```

```python
import functools

import jax
import jax.numpy as jnp
import numpy as np
from jax import lax
from jax.experimental import pallas as pl
from jax.experimental.pallas import tpu as pltpu

F32 = jnp.float32
BF16 = jnp.bfloat16

LANES = 128
BF16_SUBLANES = 16
VMEM_LIMIT_BYTES = 58 * 1024 * 1024

N_HEADS = 4
HEAD_DIM = 128
CHUNK = 128
POOL_WINDOWS = (2, 4, 8, 16)
POOL_BLOCK = 128
POOL_HALO = 16
N_DIRS = 2
N_GATE_ROWS = 2 * N_DIRS
EPS = 1e-6
LOG2_E = 1.4426950408889634

ADALN_ROWS = 128
INPROJ_ROWS = 1024
OUTFFN_ROWS = 1024
FF_COLS = 512
AUG_ROWS = HEAD_DIM + BF16_SUBLANES

ROW_A, ROW_M, ROW_AINTER, ROW_EXPNEG, ROW_WK, ROW_DECAY = range(6)
N_COEF = 6
CHUNKS_PER_STEP = 32


def _const_spec(shape):
    zeros = (0,) * len(shape)
    return pl.BlockSpec(shape, lambda *_: zeros, pipeline_mode=pl.Buffered(1))


def _rms_scale(x):
    return x * lax.rsqrt(jnp.mean(x * x, axis=-1, keepdims=True) + EPS)


def _modulated_norm_bf16(x, gain, shift, scale):
    slope = (gain * (1.0 + scale)).astype(BF16)
    return _rms_scale(x).astype(BF16) * slope + shift.astype(BF16)


def _adaln_kernel(c_ref, w_ref, b_ref, o_ref):
    @pl.when(pl.program_id(0) == 0)
    def _():
        o_ref[...] = jnp.broadcast_to(b_ref[...], o_ref.shape)

    c = c_ref[...]
    act = c * jax.nn.sigmoid(c)
    o_ref[...] += jnp.dot(act.astype(BF16), w_ref[...].astype(BF16), preferred_element_type=F32)


def _adaln(c, w_ada, b_ada):
    batch, d = c.shape
    n = w_ada.shape[1]
    rows = ADALN_ROWS
    assert d % rows == 0
    return pl.pallas_call(
        _adaln_kernel,
        grid=(d // rows,),
        in_specs=[pl.BlockSpec((batch, rows), lambda j: (0, j)),
                  pl.BlockSpec((rows, n), lambda j: (j, 0)),
                  pl.BlockSpec((1, n), lambda j: (0, 0))],
        out_specs=pl.BlockSpec((batch, n), lambda j: (0, 0)),
        out_shape=jax.ShapeDtypeStruct((batch, n), F32),
        compiler_params=pltpu.CompilerParams(dimension_semantics=("arbitrary",),
                                             vmem_limit_bytes=VMEM_LIMIT_BYTES),
    )(c, w_ada, b_ada.reshape(1, n))


def _inproj_kernel(x_ref, mod_ref, g_ref, w_ref, wt_ref, *refs, n_casts):
    cast_in, cast_out = refs[:n_casts], refs[n_casts + 6:]
    qt_ref, k_ref, vt_ref, o_ref, u_ref, gates_ref = refs[n_casts:n_casts + 6]
    for src, dst in zip(cast_in, cast_out):
        dst[...] = src[...].astype(BF16)

    dm = N_HEADS * HEAD_DIM
    x = x_ref[0]
    shift = mod_ref[0, 0:1, :]
    scale = mod_ref[0, 1:2, :]
    hb = _modulated_norm_bf16(x, g_ref[...], shift, scale)

    by_token = jnp.dot(hb, w_ref[...], preferred_element_type=F32)
    for hd in range(N_HEADS):
        k_ref[0, hd] = by_token[:, hd * HEAD_DIM:(hd + 1) * HEAD_DIM].astype(BF16)
    o_ref[0] = by_token[:, dm:2 * dm].astype(BF16)
    u_ref[0] = by_token[:, 2 * dm:].astype(BF16)
    by_feature = lax.dot_general(wt_ref[...], hb, (((1,), (1,)), ((), ())), preferred_element_type=F32)
    for hd in range(N_HEADS):
        rows = slice(hd * HEAD_DIM, (hd + 1) * HEAD_DIM)
        qt_ref[0, hd] = (by_feature[rows, :] * (HEAD_DIM ** -0.5)).astype(BF16)
        vt_ref[0, hd, 0:HEAD_DIM, :] = by_feature[dm + hd * HEAD_DIM:dm + (hd + 1) * HEAD_DIM, :].astype(BF16)
        vt_ref[0, hd, HEAD_DIM:AUG_ROWS, :] = jnp.ones((AUG_ROWS - HEAD_DIM, x.shape[0]), BF16)
    gt = by_feature[2 * dm:, :]
    n_chunks = x.shape[0] // CHUNK
    for hd in range(N_HEADS):
        for r in range(N_GATE_ROWS):
            row = hd * N_GATE_ROWS + r
            for j in range(n_chunks):
                gates_ref[0, hd, r, j:j + 1, :] = gt[row:row + 1, j * CHUNK:(j + 1) * CHUNK]


def _inproj(x, mod, g_mix, w_cat, w_gates_t, d_pool, weights_to_cast):
    batch, seq, d = x.shape
    tm = INPROJ_ROWS
    dm = N_HEADS * HEAD_DIM
    grid = (batch, seq // tm)
    tiles_per_seq = seq // tm
    n_steps = batch * tiles_per_seq
    cast_specs = []
    for w in weights_to_cast:
        assert w.shape[0] % (n_steps * BF16_SUBLANES) == 0
        cast_specs.append(pl.BlockSpec((w.shape[0] // n_steps, w.shape[1]),
                                       lambda b, i: (b * tiles_per_seq + i, 0)))
    head_t = jax.ShapeDtypeStruct((batch, N_HEADS, HEAD_DIM, seq), BF16)
    head_n = jax.ShapeDtypeStruct((batch, N_HEADS, seq, HEAD_DIM), BF16)
    head_aug = jax.ShapeDtypeStruct((batch, N_HEADS, AUG_ROWS, seq), BF16)
    out_shape = (head_t, head_n, head_aug,
                 jax.ShapeDtypeStruct((batch, seq, dm), BF16),
                 jax.ShapeDtypeStruct((batch, seq, d_pool), BF16),
                 jax.ShapeDtypeStruct((batch, N_HEADS, N_GATE_ROWS, seq // CHUNK, CHUNK), F32))
    spec_t = pl.BlockSpec((1, N_HEADS, HEAD_DIM, tm), lambda b, i: (b, 0, 0, i))
    spec_n = pl.BlockSpec((1, N_HEADS, tm, HEAD_DIM), lambda b, i: (b, 0, i, 0))
    spec_aug = pl.BlockSpec((1, N_HEADS, AUG_ROWS, tm), lambda b, i: (b, 0, 0, i))
    out_specs = (spec_t, spec_n, spec_aug,
                 pl.BlockSpec((1, tm, dm), lambda b, i: (b, i, 0)),
                 pl.BlockSpec((1, tm, d_pool), lambda b, i: (b, i, 0)),
                 pl.BlockSpec((1, N_HEADS, N_GATE_ROWS, tm // CHUNK, CHUNK), lambda b, i: (b, 0, 0, i, 0)))
    outs = pl.pallas_call(
        functools.partial(_inproj_kernel, n_casts=len(weights_to_cast)),
        grid=grid,
        in_specs=[pl.BlockSpec((1, tm, d), lambda b, i: (b, i, 0)),
                  pl.BlockSpec((1,) + mod.shape[1:], lambda b, i: (b, 0, 0)),
                  _const_spec((1, d)),
                  _const_spec(w_cat.shape),
                  _const_spec(w_gates_t.shape)] + cast_specs,
        out_specs=list(out_specs) + cast_specs,
        out_shape=list(out_shape) + [jax.ShapeDtypeStruct(w.shape, BF16) for w in weights_to_cast],
        compiler_params=pltpu.CompilerParams(dimension_semantics=("arbitrary", "arbitrary"),
                                             vmem_limit_bytes=VMEM_LIMIT_BYTES),
    )(x, mod, g_mix.reshape(1, d), w_cat, w_gates_t, *weights_to_cast)
    return outs[:6], outs[6:]


def _time_scan(x, op, fill, reverse):
    row = lax.broadcasted_iota(jnp.int32, x.shape, 0)
    step = 1
    while step < CHUNK:
        if reverse:
            shifted = jnp.where(row < CHUNK - step, pltpu.roll(x, CHUNK - step, axis=0), fill)
        else:
            shifted = jnp.where(row >= step, pltpu.roll(x, step, axis=0), fill)
        x = op(x, shifted)
        step *= 2
    return x


def _chunk_carry(decay_log, peak, reverse):
    n = decay_log.shape[0]
    row = lax.broadcasted_iota(jnp.int32, decay_log.shape, 0)

    def earlier(x, dist, fill):
        if reverse:
            return jnp.where(row < n - dist, pltpu.roll(x, n - dist, axis=0), fill)
        return jnp.where(row >= dist, pltpu.roll(x, dist, axis=0), fill)

    p, q = decay_log, decay_log + peak
    dist = 1
    while dist < n:
        q = jnp.maximum(earlier(q, dist, -jnp.inf) + p, q)
        p = earlier(p, dist, 0.0) + p
        dist *= 2
    return earlier(jnp.maximum(p, q), 1, 0.0)


def _mlstm_kernel(qt_ref, k_ref, vt_ref, gates_ref, gbias_ref, wpool_ref, pscale_ref, woutp_ref, *refs,
                  n_casts, n_units):
    cast_in, out_ref, fold_ref = refs[:n_casts], refs[n_casts], refs[n_casts + 1]
    cast_out = refs[n_casts + 2:2 * n_casts + 2]
    coef_ref, sprev_ref, st_ref = refs[2 * n_casts + 2:]

    def weight_slabs():
        for src, dst in zip(cast_in, cast_out):
            dst[...] = src[...].astype(BF16)
        lhs = wpool_ref[...] * pscale_ref[...]
        rhs = woutp_ref[...]
        lhs_hi, rhs_hi = lhs.astype(BF16), rhs.astype(BF16)
        lhs_lo = (lhs - lhs_hi.astype(F32)).astype(BF16)
        rhs_lo = (rhs - rhs_hi.astype(F32)).astype(BF16)
        fold_ref[...] = (jnp.dot(lhs_hi, rhs_hi, preferred_element_type=F32)
                         + jnp.dot(lhs_hi, rhs_lo, preferred_element_type=F32)
                         + jnp.dot(lhs_lo, rhs_hi, preferred_element_type=F32)).astype(BF16)

    n_chunks = gates_ref.shape[3]
    assert 2 * N_DIRS * n_chunks == LANES
    step = pl.program_id(0)

    def prologue(slot):
        log_i, log_f = [], []
        for d in range(N_DIRS):
            log_i.append(gates_ref[0, 0, 2 * d] + gbias_ref[0, 2 * d:2 * d + 1, :])
            z = gates_ref[0, 0, 2 * d + 1] + gbias_ref[0, 2 * d + 1:2 * d + 2, :]
            log_f.append(-(jnp.maximum(-z, 0.0) + jnp.log1p(jnp.exp(-jnp.abs(z)))))
        by_time = jnp.concatenate(log_f + log_i, axis=0).T
        lane = lax.broadcasted_iota(jnp.int32, by_time.shape, 1)
        fwd_lane = (lane % (N_DIRS * n_chunks)) < n_chunks
        b_t = jnp.where(fwd_lane, _time_scan(by_time, jnp.add, 0.0, False),
                        _time_scan(by_time, jnp.add, 0.0, True))
        a_t = pltpu.roll(by_time, N_DIRS * n_chunks, axis=1) - b_t
        cm_t = jnp.where(fwd_lane, _time_scan(a_t, jnp.maximum, -jnp.inf, False),
                         _time_scan(a_t, jnp.maximum, -jnp.inf, True))
        b_rows, a_rows, cm_rows = b_t.T, a_t.T, cm_t.T

        for d in range(N_DIRS):
            reverse = d == 1
            last = 0 if reverse else CHUNK - 1
            b = b_rows[d * n_chunks:(d + 1) * n_chunks]
            a = a_rows[d * n_chunks:(d + 1) * n_chunks]
            cm = cm_rows[d * n_chunks:(d + 1) * n_chunks]
            b_last = jnp.broadcast_to(b[:, last:last + 1], b.shape)
            cm_last = jnp.broadcast_to(cm[:, last:last + 1], b.shape)
            m_prev = _chunk_carry(b_last, cm_last, reverse)
            m_row = jnp.maximum(m_prev, cm)
            a_inter = jnp.exp(m_prev - m_row)
            exp_neg = jnp.exp(-(b + m_row))
            m_end = jnp.broadcast_to(m_row[:, last:last + 1], b.shape)
            wk = jnp.exp(a - m_end)
            decay = jnp.broadcast_to(a_inter[:, last:last + 1], b.shape)
            for r, val in ((ROW_A, a * LOG2_E), (ROW_M, m_row * LOG2_E), (ROW_AINTER, a_inter),
                           (ROW_EXPNEG, exp_neg), (ROW_WK, wk), (ROW_DECAY, decay)):
                coef_ref[slot, d, r] = val

    def passes(slot):
        def coef_row(d, r, c):
            return coef_ref[slot, d, r, pl.ds(c, 1), :]

        def chunk_operands(c):
            off = pl.multiple_of(c * CHUNK, CHUNK)
            return off, qt_ref[0, 0, :, pl.ds(off, CHUNK)], k_ref[0, 0, pl.ds(off, CHUNK), :], \
                vt_ref[0, 0, :, pl.ds(off, CHUNK)]

        key_pos = lax.broadcasted_iota(jnp.int32, (CHUNK, CHUNK), 0)
        qry_pos = lax.broadcasted_iota(jnp.int32, (CHUNK, CHUNK), 1)
        visible = (key_pos <= qry_pos, key_pos >= qry_pos)

        def weighted_scores(c):
            _, qt_c, k_c, _ = chunk_operands(c)
            scores = jnp.dot(k_c, qt_c, preferred_element_type=F32)
            for d in range(N_DIRS):
                a_col = jnp.broadcast_to(coef_row(d, ROW_A, c), (CHUNK, CHUNK)).T
                weight = jnp.exp2(jnp.where(visible[d], a_col - coef_row(d, ROW_M, c), -jnp.inf))
                st_ref[d, c] = (scores * weight).astype(BF16)

        def state_step(d, c, state):
            _, _, k_c, vt_c = chunk_operands(c)
            sprev_ref[d, c] = state.astype(BF16)
            vw = vt_c * coef_row(d, ROW_WK, c).astype(BF16)
            return coef_row(d, ROW_DECAY, c) * state + jnp.dot(vw, k_c, preferred_element_type=F32)

        def state_body(i, states):
            s_fwd, s_bwd = states
            for j in range(CHUNKS_PER_STEP):
                c = i * CHUNKS_PER_STEP + j
                weighted_scores(c)
                s_fwd = state_step(0, c, s_fwd)
                s_bwd = state_step(1, n_chunks - 1 - c, s_bwd)
            return s_fwd, s_bwd

        zero_state = jnp.zeros((AUG_ROWS, HEAD_DIM), F32)
        lax.fori_loop(0, n_chunks // CHUNKS_PER_STEP, state_body, (zero_state, zero_state))

        def output_chunk(c):
            off, qt_c, _, vt_c = chunk_operands(c)
            hsum = None
            for d in range(N_DIRS):
                qa = qt_c * coef_row(d, ROW_AINTER, c).astype(BF16)
                lhs = jnp.concatenate([vt_c, sprev_ref[d, c]], axis=1)
                rhs = jnp.concatenate([st_ref[d, c], qa], axis=0)
                numden = jnp.dot(lhs, rhs, preferred_element_type=F32)
                den = numden[HEAD_DIM:HEAD_DIM + 1, :]
                ht = numden[0:HEAD_DIM, :] * (1.0 / jnp.maximum(jnp.abs(den), coef_row(d, ROW_EXPNEG, c)))
                hsum = ht if hsum is None else hsum + ht
            out_ref[0, 0, pl.ds(off, CHUNK), :] = hsum.T.astype(BF16)

        def output_body(i, carry):
            for j in range(CHUNKS_PER_STEP):
                output_chunk(i * CHUNKS_PER_STEP + j)
            return carry

        lax.fori_loop(0, n_chunks // CHUNKS_PER_STEP, output_body, 0)

    @pl.when(step == 0)
    def _():
        prologue(0)
        weight_slabs()

    @pl.when(step == n_units)
    def _():
        passes((n_units - 1) % 2)
        weight_slabs()

    for fill in range(2):
        @pl.when((step > 0) & (step < n_units) & (step % 2 == fill))
        def _():
            passes(1 - fill)
            prologue(fill)
            weight_slabs()


def _mlstm(qt, k, vt, gates, gbias, w_pool, pool_scale, w_out, weights_to_cast):
    batch, heads, hd, seq = qt.shape
    n_chunks = seq // CHUNK
    assert n_chunks % CHUNKS_PER_STEP == 0
    n_units = batch * heads
    groups, gdim, _ = w_pool.shape
    d = w_out.shape[1]
    first_pool_block = (w_out.shape[0] - groups * gdim) // gdim
    units_per_group = n_units // groups
    fold_rows = gdim // units_per_group
    assert n_units % groups == 0 and fold_rows % BF16_SUBLANES == 0

    def pass_unit(g):
        u = jnp.maximum(g - 1, 0)
        return u // heads, u % heads, 0, 0

    def prologue_unit(g):
        return jnp.minimum(g, n_units - 1)

    spec_t = pl.BlockSpec((1, 1, hd, seq), pass_unit)
    spec_n = pl.BlockSpec((1, 1, seq, hd), pass_unit)
    spec_aug = pl.BlockSpec((1, 1, AUG_ROWS, seq), pass_unit)
    cast_specs = []
    for w in weights_to_cast:
        assert w.shape[0] % (n_units * BF16_SUBLANES) == 0
        cast_specs.append(pl.BlockSpec((w.shape[0] // n_units, w.shape[1]), lambda g: (prologue_unit(g), 0)))
    outs = pl.pallas_call(
        functools.partial(_mlstm_kernel, n_casts=len(weights_to_cast), n_units=n_units),
        grid=(n_units + 1,),
        in_specs=[spec_t, spec_n, spec_aug,
                  pl.BlockSpec((1, 1, N_GATE_ROWS, n_chunks, CHUNK),
                               lambda g: (prologue_unit(g) // heads, prologue_unit(g) % heads, 0, 0, 0)),
                  pl.BlockSpec((1, N_GATE_ROWS, CHUNK), lambda g: (prologue_unit(g) % heads, 0, 0)),
                  pl.BlockSpec((fold_rows, gdim), lambda g: (prologue_unit(g), 0)),
                  pl.BlockSpec((1, gdim), lambda g: (0, prologue_unit(g) // units_per_group)),
                  pl.BlockSpec((gdim, d), lambda g: (first_pool_block + prologue_unit(g) // units_per_group, 0))]
        + cast_specs,
        out_specs=[spec_n, pl.BlockSpec((fold_rows, d), lambda g: (prologue_unit(g), 0))] + cast_specs,
        out_shape=[jax.ShapeDtypeStruct((batch, heads, seq, hd), BF16),
                   jax.ShapeDtypeStruct((groups * gdim, d), BF16)]
        + [jax.ShapeDtypeStruct(w.shape, BF16) for w in weights_to_cast],
        scratch_shapes=[pltpu.VMEM((2, N_DIRS, N_COEF, n_chunks, CHUNK), F32),
                        pltpu.VMEM((N_DIRS, n_chunks, AUG_ROWS, hd), BF16),
                        pltpu.VMEM((N_DIRS, n_chunks, CHUNK, CHUNK), BF16)],
        compiler_params=pltpu.CompilerParams(dimension_semantics=("arbitrary",),
                                             vmem_limit_bytes=VMEM_LIMIT_BYTES),
    )(qt, k, vt, gates, gbias, w_pool.reshape(groups * gdim, gdim), pool_scale.reshape(1, groups * gdim), w_out,
      *weights_to_cast)
    return outs[0], outs[1], outs[2:]


def _outffn_kernel(x_ref, hm_ref, o_ref, u_ref, uprev_ref, unext_ref, mod_ref,
                   ghead_ref, gffn_ref, gfinal_ref,
                   band_ref, wout_ref, wfold_ref, wff1_ref, wff2_ref, out_ref,
                   ubuf_ref, ycat_ref, *, seq, n_tiles, final_norm):
    tm = x_ref.shape[1]
    dm = N_HEADS * HEAD_DIM
    step = pl.program_id(0)
    tiles_per_seq = seq // tm
    tile = step % tiles_per_seq

    def matmul_stage():
        gate1 = mod_ref[0, 2:3, :]
        yproj = (jnp.dot(ycat_ref[:, 0:dm], wout_ref[0:dm, :], preferred_element_type=F32)
                 + jnp.dot(ycat_ref[:, dm:], wfold_ref[...], preferred_element_type=F32))
        x1 = x_ref[0] + gate1 * yproj

        shift2 = mod_ref[0, 3:4, :]
        scale2 = mod_ref[0, 4:5, :]
        gate2 = mod_ref[0, 5:6, :]
        h2 = ((_rms_scale(x1) * gffn_ref[...]) * (1.0 + scale2) + shift2).astype(BF16)
        acc = jnp.zeros(x1.shape, F32)
        for j in range(wff1_ref.shape[1] // FF_COLS):
            hid = jnp.dot(h2, wff1_ref[:, j * FF_COLS:(j + 1) * FF_COLS], preferred_element_type=F32)
            hid = jnp.square(jnp.maximum(hid, 0.0)).astype(BF16)
            acc = acc + jnp.dot(hid, wff2_ref[j * FF_COLS:(j + 1) * FF_COLS, :], preferred_element_type=F32)
        x2 = x1 + gate2 * acc
        if final_norm:
            x2 = _rms_scale(x2) * gfinal_ref[...]
        out_ref[0] = x2

    def elementwise_stage():
        for hd in range(N_HEADS):
            cols = slice(hd * HEAD_DIM, (hd + 1) * HEAD_DIM)
            hn = _rms_scale(hm_ref[0, hd].astype(F32)) * ghead_ref[:, cols]
            gate = jax.nn.sigmoid(o_ref[0, :, cols].astype(F32))
            ycat_ref[:, cols] = (gate * hn).astype(BF16)

        ubuf_ref[0:POOL_HALO, :] = jnp.where(tile > 0, uprev_ref[0], jnp.zeros_like(uprev_ref[0]))
        ubuf_ref[POOL_HALO:POOL_HALO + tm, :] = u_ref[0]
        ubuf_ref[POOL_HALO + tm:, :] = jnp.where(tile < tiles_per_seq - 1, unext_ref[0],
                                                 jnp.zeros_like(unext_ref[0]))
        for gi in range(len(POOL_WINDOWS)):
            cols = slice(gi * HEAD_DIM, (gi + 1) * HEAD_DIM)
            ext = jnp.concatenate([ubuf_ref[r0:r0 + POOL_BLOCK + 2 * POOL_HALO, cols]
                                   for r0 in range(0, tm, POOL_BLOCK)], axis=1)
            mixed = jnp.dot(band_ref[gi], ext, preferred_element_type=F32).astype(BF16)
            for blk, r0 in enumerate(range(0, tm, POOL_BLOCK)):
                ycat_ref[r0:r0 + POOL_BLOCK, dm + gi * HEAD_DIM:dm + (gi + 1) * HEAD_DIM] = (
                    mixed[:, blk * HEAD_DIM:(blk + 1) * HEAD_DIM])

    def sequence_end_rows():
        def clipped_rows(r0):
            ext = ubuf_ref[r0:r0 + 3 * POOL_HALO, :].astype(F32)
            pos = tile * tm + r0 + lax.broadcasted_iota(jnp.int32, (POOL_HALO, HEAD_DIM), 0)
            for gi, win in enumerate(POOL_WINDOWS):
                half = win // 2
                grp = ext[:, gi * HEAD_DIM:(gi + 1) * HEAD_DIM]
                total = grp[POOL_HALO - half:2 * POOL_HALO - half]
                for j in range(1 - half, half):
                    total = total + grp[POOL_HALO + j:2 * POOL_HALO + j]
                cnt = jnp.minimum(pos + half, seq) - jnp.maximum(pos - half, 0)
                mixed = total / cnt.astype(F32) - grp[POOL_HALO:2 * POOL_HALO]
                ycat_ref[r0:r0 + POOL_HALO, dm + gi * HEAD_DIM:dm + (gi + 1) * HEAD_DIM] = mixed.astype(BF16)

        @pl.when(tile == 0)
        def _():
            clipped_rows(0)

        @pl.when(tile == tiles_per_seq - 1)
        def _():
            clipped_rows(tm - POOL_HALO)

    @pl.when(step == 0)
    def _():
        elementwise_stage()
        sequence_end_rows()

    @pl.when(step > 0)
    def _():
        matmul_stage()
        elementwise_stage()
        sequence_end_rows()


def _pool_bands():
    row = np.arange(POOL_BLOCK)[:, None]
    col = np.arange(POOL_BLOCK + 2 * POOL_HALO)[None, :] - POOL_HALO
    bands = [((col >= row - win // 2) & (col < row + win // 2)) / win - (col == row) for win in POOL_WINDOWS]
    return jnp.asarray(np.stack(bands), dtype=BF16)


def _outffn(x, hm, o, u, mod, g_head, g_ffn, g_final, w_out, w_fold, w_ff1, w_ff2, final_norm):
    batch, seq, d = x.shape
    tm = OUTFFN_ROWS
    dm = N_HEADS * HEAD_DIM
    d_pool = u.shape[2]
    halo_per_tile = tm // POOL_HALO
    n_halo_blocks = seq // POOL_HALO
    tiles_per_seq = seq // tm
    n_tiles = batch * tiles_per_seq

    def mm_tile(g):
        t = jnp.maximum(g - 1, 0)
        return t // tiles_per_seq, t % tiles_per_seq

    def ew_tile(g):
        t = jnp.minimum(g, n_tiles - 1)
        return t // tiles_per_seq, t % tiles_per_seq

    def mm_rows(g):
        b, i = mm_tile(g)
        return b, i, 0

    def ew_rows(g):
        b, i = ew_tile(g)
        return b, i, 0

    def ew_heads(g):
        b, i = ew_tile(g)
        return b, 0, i, 0

    def ew_prev(g):
        b, i = ew_tile(g)
        return b, jnp.maximum(i * halo_per_tile - 1, 0), 0

    def ew_next(g):
        b, i = ew_tile(g)
        return b, jnp.minimum((i + 1) * halo_per_tile, n_halo_blocks - 1), 0

    bands = _pool_bands()
    kernel = functools.partial(_outffn_kernel, seq=seq, n_tiles=n_tiles, final_norm=final_norm)
    return pl.pallas_call(
        kernel,
        grid=(n_tiles + 1,),
        in_specs=[pl.BlockSpec((1, tm, d), mm_rows),
                  pl.BlockSpec((1, N_HEADS, tm, HEAD_DIM), ew_heads),
                  pl.BlockSpec((1, tm, dm), ew_rows),
                  pl.BlockSpec((1, tm, d_pool), ew_rows),
                  pl.BlockSpec((1, POOL_HALO, d_pool), ew_prev),
                  pl.BlockSpec((1, POOL_HALO, d_pool), ew_next),
                  pl.BlockSpec((1,) + mod.shape[1:], lambda g: (mm_tile(g)[0], 0, 0)),
                  _const_spec((1, dm)),
                  _const_spec((1, d)),
                  _const_spec((1, d)),
                  _const_spec(bands.shape),
                  _const_spec(w_out.shape),
                  _const_spec(w_fold.shape),
                  _const_spec(w_ff1.shape),
                  _const_spec(w_ff2.shape)],
        out_specs=pl.BlockSpec((1, tm, d), mm_rows),
        out_shape=jax.ShapeDtypeStruct((batch, seq, d), F32),
        scratch_shapes=[pltpu.VMEM((tm + 2 * POOL_HALO, d_pool), BF16),
                        pltpu.VMEM((tm, dm + d_pool), BF16)],
        compiler_params=pltpu.CompilerParams(dimension_semantics=("arbitrary",),
                                             vmem_limit_bytes=VMEM_LIMIT_BYTES),
    )(x, hm, o, u, u, u, mod, g_head.reshape(1, dm),
      g_ffn.reshape(1, d), g_final.reshape(1, d), bands, w_out, w_fold, w_ff1, w_ff2)


def _pack_w_in(w_in):
    dm = N_HEADS * HEAD_DIM
    n_gates = N_DIRS * 2 * N_HEADS
    q, k, v, o = (w_in[:, i * dm:(i + 1) * dm] for i in range(4))
    gates = w_in[:, 4 * dm:4 * dm + n_gates]
    pool = w_in[:, 4 * dm + n_gates:]
    gates_t = gates.reshape(-1, N_DIRS, 2, N_HEADS).transpose(3, 1, 2, 0).reshape(n_gates, -1)
    cols = jnp.concatenate([k, o, pool], axis=1).astype(BF16)
    rows = jnp.concatenate([q.T, v.T, gates_t], axis=0).astype(BF16)
    return cols, rows


def _pack_gate_bias(b_igate, b_fgate):
    rows = jnp.stack([b_igate, b_fgate], axis=1)
    rows = rows.transpose(2, 0, 1).reshape(N_HEADS, N_GATE_ROWS)
    return jnp.broadcast_to(rows[:, :, None], (N_HEADS, N_GATE_ROWS, CHUNK)).astype(F32)


def kernel(x, c, w_ada, b_ada, g_mix, w_in, b_igate, b_fgate, g_head, w_pool, pool_scale, w_out,
           g_ffn, w_ff1, w_ff2, g_final):
    depth = w_ada.shape[0]
    batch, seq, d = x.shape
    d_pool = pool_scale.shape[1]
    assert g_head.shape[1] == N_HEADS * HEAD_DIM and w_pool.shape[1] == len(POOL_WINDOWS)
    assert seq % INPROJ_ROWS == 0 and seq % OUTFFN_ROWS == 0 and w_ff1.shape[2] % FF_COLS == 0
    for l in range(depth):
        mod = _adaln(c, w_ada[l], b_ada[l]).reshape(batch, 6, d)
        (qt, k, vt, o, u, gates), (w_out_b, w_ff1_b) = _inproj(
            x, mod, g_mix[l], *_pack_w_in(w_in[l]), d_pool, (w_out[l], w_ff1[l]))
        hm, w_fold, (w_ff2_b,) = _mlstm(
            qt, k, vt, gates, _pack_gate_bias(b_igate[l], b_fgate[l]), w_pool[l], pool_scale[l], w_out[l],
            (w_ff2[l],))
        x = _outffn(x, hm, o, u, mod, g_head[l], g_ffn[l], g_final,
                    w_out_b, w_fold, w_ff1_b, w_ff2_b, final_norm=(l == depth - 1))
    return x
```

```python
import functools

import jax
import jax.numpy as jnp
import numpy as np
from jax import lax
from jax.experimental import pallas as pl
from jax.experimental.pallas import tpu as pltpu

F32 = jnp.float32
BF16 = jnp.bfloat16

LANES = 128
BF16_SUBLANES = 16
VMEM_LIMIT_BYTES = 56 * 1024 * 1024

N_HEADS = 4
HEAD_DIM = 128
CHUNK = 128
POOL_WINDOWS = (2, 4, 8, 16)
POOL_BLOCK = 128
POOL_HALO = 16
N_DIRS = 2
N_GATE_ROWS = 2 * N_DIRS
EPS = 1e-6
LOG2_E = 1.4426950408889634

ADALN_ROWS = 128
INPROJ_ROWS = 1024
OUTFFN_ROWS = 512
FF_COLS = 1024
AUG_ROWS = HEAD_DIM + BF16_SUBLANES

ROW_A, ROW_M, ROW_AINTER, ROW_EXPNEG, ROW_WK, ROW_DECAY = range(6)
N_COEF = 6
CHUNKS_PER_STEP = 32


def _const_spec(shape):
    zeros = (0,) * len(shape)
    return pl.BlockSpec(shape, lambda *_: zeros, pipeline_mode=pl.Buffered(1))


def _rms_scale(x):
    return x * lax.rsqrt(jnp.mean(x * x, axis=-1, keepdims=True) + EPS)


def _modulated_norm_bf16(x, gain, shift, scale):
    slope = (gain * (1.0 + scale)).astype(BF16)
    return _rms_scale(x).astype(BF16) * slope + shift.astype(BF16)


def _adaln_kernel(c_ref, w_ref, b_ref, o_ref):
    @pl.when(pl.program_id(0) == 0)
    def _():
        o_ref[...] = jnp.broadcast_to(b_ref[...], o_ref.shape)

    c = c_ref[...]
    act = c * jax.nn.sigmoid(c)
    o_ref[...] += jnp.dot(act.astype(BF16), w_ref[...].astype(BF16), preferred_element_type=F32)


def _adaln(c, w_ada, b_ada):
    batch, d = c.shape
    n = w_ada.shape[1]
    rows = ADALN_ROWS
    assert d % rows == 0
    return pl.pallas_call(
        _adaln_kernel,
        grid=(d // rows,),
        in_specs=[pl.BlockSpec((batch, rows), lambda j: (0, j)),
                  pl.BlockSpec((rows, n), lambda j: (j, 0)),
                  pl.BlockSpec((1, n), lambda j: (0, 0))],
        out_specs=pl.BlockSpec((batch, n), lambda j: (0, 0)),
        out_shape=jax.ShapeDtypeStruct((batch, n), F32),
        compiler_params=pltpu.CompilerParams(dimension_semantics=("arbitrary",),
                                             vmem_limit_bytes=VMEM_LIMIT_BYTES),
    )(c, w_ada, b_ada.reshape(1, n))


def _inproj_kernel(x_ref, mod_ref, g_ref, w_ref, wt_ref, *refs, n_casts):
    cast_in, cast_out = refs[:n_casts], refs[n_casts + 6:]
    qt_ref, k_ref, vt_ref, o_ref, u_ref, gates_ref = refs[n_casts:n_casts + 6]
    for src, dst in zip(cast_in, cast_out):
        dst[...] = src[...].astype(BF16)

    dm = N_HEADS * HEAD_DIM
    x = x_ref[0]
    shift = mod_ref[0, 0:1, :]
    scale = mod_ref[0, 1:2, :]
    hb = _modulated_norm_bf16(x, g_ref[...], shift, scale)

    by_token = lax.dot_general(hb, w_ref[...], (((1,), (1,)), ((), ())), preferred_element_type=F32)
    for hd in range(N_HEADS):
        k_ref[0, hd] = by_token[:, hd * HEAD_DIM:(hd + 1) * HEAD_DIM].astype(BF16)
    o_ref[0] = by_token[:, dm:2 * dm].astype(BF16)
    u_ref[0] = by_token[:, 2 * dm:].astype(BF16)
    by_feature = lax.dot_general(wt_ref[...], hb, (((1,), (1,)), ((), ())), preferred_element_type=F32)
    for hd in range(N_HEADS):
        rows = slice(hd * HEAD_DIM, (hd + 1) * HEAD_DIM)
        qt_ref[0, hd] = (by_feature[rows, :] * (HEAD_DIM ** -0.5)).astype(BF16)
        vt_ref[0, hd, 0:HEAD_DIM, :] = by_feature[dm + hd * HEAD_DIM:dm + (hd + 1) * HEAD_DIM, :].astype(BF16)
        vt_ref[0, hd, HEAD_DIM:AUG_ROWS, :] = jnp.ones((AUG_ROWS - HEAD_DIM, x.shape[0]), BF16)
    gt = by_feature[2 * dm:, :]
    n_chunks = x.shape[0] // CHUNK
    for hd in range(N_HEADS):
        for r in range(N_GATE_ROWS):
            row = hd * N_GATE_ROWS + r
            for j in range(n_chunks):
                gates_ref[0, hd, r, j:j + 1, :] = gt[row:row + 1, j * CHUNK:(j + 1) * CHUNK]


def _inproj(x, mod, g_mix, w_cat, w_gates_t, d_pool, weights_to_cast):
    batch, seq, d = x.shape
    tm = INPROJ_ROWS
    dm = N_HEADS * HEAD_DIM
    grid = (batch, seq // tm)
    tiles_per_seq = seq // tm
    n_steps = batch * tiles_per_seq
    cast_specs = []
    for w in weights_to_cast:
        assert w.shape[0] % (n_steps * BF16_SUBLANES) == 0
        cast_specs.append(pl.BlockSpec((w.shape[0] // n_steps, w.shape[1]),
                                       lambda b, i: (b * tiles_per_seq + i, 0)))
    head_t = jax.ShapeDtypeStruct((batch, N_HEADS, HEAD_DIM, seq), BF16)
    head_n = jax.ShapeDtypeStruct((batch, N_HEADS, seq, HEAD_DIM), BF16)
    head_aug = jax.ShapeDtypeStruct((batch, N_HEADS, AUG_ROWS, seq), BF16)
    out_shape = (head_t, head_n, head_aug,
                 jax.ShapeDtypeStruct((batch, seq, dm), BF16),
                 jax.ShapeDtypeStruct((batch, seq, d_pool), BF16),
                 jax.ShapeDtypeStruct((batch, N_HEADS, N_GATE_ROWS, seq // CHUNK, CHUNK), F32))
    spec_t = pl.BlockSpec((1, N_HEADS, HEAD_DIM, tm), lambda b, i: (b, 0, 0, i))
    spec_n = pl.BlockSpec((1, N_HEADS, tm, HEAD_DIM), lambda b, i: (b, 0, i, 0))
    spec_aug = pl.BlockSpec((1, N_HEADS, AUG_ROWS, tm), lambda b, i: (b, 0, 0, i))
    out_specs = (spec_t, spec_n, spec_aug,
                 pl.BlockSpec((1, tm, dm), lambda b, i: (b, i, 0)),
                 pl.BlockSpec((1, tm, d_pool), lambda b, i: (b, i, 0)),
                 pl.BlockSpec((1, N_HEADS, N_GATE_ROWS, tm // CHUNK, CHUNK), lambda b, i: (b, 0, 0, i, 0)))
    outs = pl.pallas_call(
        functools.partial(_inproj_kernel, n_casts=len(weights_to_cast)),
        grid=grid,
        in_specs=[pl.BlockSpec((1, tm, d), lambda b, i: (b, i, 0)),
                  pl.BlockSpec((1,) + mod.shape[1:], lambda b, i: (b, 0, 0)),
                  _const_spec((1, d)),
                  _const_spec(w_cat.shape),
                  _const_spec(w_gates_t.shape)] + cast_specs,
        out_specs=list(out_specs) + cast_specs,
        out_shape=list(out_shape) + [jax.ShapeDtypeStruct(w.shape, BF16) for w in weights_to_cast],
        compiler_params=pltpu.CompilerParams(dimension_semantics=("arbitrary", "arbitrary"),
                                             vmem_limit_bytes=VMEM_LIMIT_BYTES),
    )(x, mod, g_mix.reshape(1, d), w_cat, w_gates_t, *weights_to_cast)
    return outs[:6], outs[6:]


def _time_scan(x, op, fill, reverse):
    row = lax.broadcasted_iota(jnp.int32, x.shape, 0)
    step = 1
    while step < CHUNK:
        if reverse:
            shifted = jnp.where(row < CHUNK - step, pltpu.roll(x, CHUNK - step, axis=0), fill)
        else:
            shifted = jnp.where(row >= step, pltpu.roll(x, step, axis=0), fill)
        x = op(x, shifted)
        step *= 2
    return x


def _chunk_carry(decay_log, peak, reverse):
    n = decay_log.shape[0]
    row = lax.broadcasted_iota(jnp.int32, decay_log.shape, 0)

    def earlier(x, dist, fill):
        if reverse:
            return jnp.where(row < n - dist, pltpu.roll(x, n - dist, axis=0), fill)
        return jnp.where(row >= dist, pltpu.roll(x, dist, axis=0), fill)

    p, q = decay_log, decay_log + peak
    dist = 1
    while dist < n:
        q = jnp.maximum(earlier(q, dist, -jnp.inf) + p, q)
        p = earlier(p, dist, 0.0) + p
        dist *= 2
    return earlier(jnp.maximum(p, q), 1, 0.0)


def _mlstm_kernel(qt_ref, k_ref, vt_ref, gates_ref, gbias_ref, wpool_ref, pscale_ref, woutp_ref, *refs,
                  n_casts, n_units):
    cast_in, out_ref, fold_ref = refs[:n_casts], refs[n_casts], refs[n_casts + 1]
    cast_out = refs[n_casts + 2:2 * n_casts + 2]
    coef_ref, sprev_ref, st_ref = refs[2 * n_casts + 2:]

    def weight_slabs():
        for src, dst in zip(cast_in, cast_out):
            dst[...] = src[...].astype(BF16)
        lhs = wpool_ref[...] * pscale_ref[...]
        rhs = woutp_ref[...]
        lhs_hi, rhs_hi = lhs.astype(BF16), rhs.astype(BF16)
        lhs_lo = (lhs - lhs_hi.astype(F32)).astype(BF16)
        rhs_lo = (rhs - rhs_hi.astype(F32)).astype(BF16)
        fold_ref[...] = (jnp.dot(lhs_hi, rhs_hi, preferred_element_type=F32)
                         + jnp.dot(lhs_hi, rhs_lo, preferred_element_type=F32)
                         + jnp.dot(lhs_lo, rhs_hi, preferred_element_type=F32)).astype(BF16)

    n_chunks = gates_ref.shape[3]
    assert 2 * N_DIRS * n_chunks == LANES
    step = pl.program_id(0)

    def prologue(slot):
        log_i, log_f = [], []
        for d in range(N_DIRS):
            log_i.append(gates_ref[0, 0, 2 * d] + gbias_ref[0, 2 * d:2 * d + 1, :])
            z = gates_ref[0, 0, 2 * d + 1] + gbias_ref[0, 2 * d + 1:2 * d + 2, :]
            log_f.append(-(jnp.maximum(-z, 0.0) + jnp.log1p(jnp.exp(-jnp.abs(z)))))
        by_time = jnp.concatenate(log_f + log_i, axis=0).T
        lane = lax.broadcasted_iota(jnp.int32, by_time.shape, 1)
        fwd_lane = (lane % (N_DIRS * n_chunks)) < n_chunks
        b_t = jnp.where(fwd_lane, _time_scan(by_time, jnp.add, 0.0, False),
                        _time_scan(by_time, jnp.add, 0.0, True))
        a_t = pltpu.roll(by_time, N_DIRS * n_chunks, axis=1) - b_t
        cm_t = jnp.where(fwd_lane, _time_scan(a_t, jnp.maximum, -jnp.inf, False),
                         _time_scan(a_t, jnp.maximum, -jnp.inf, True))
        b_rows, a_rows, cm_rows = b_t.T, a_t.T, cm_t.T

        for d in range(N_DIRS):
            reverse = d == 1
            last = 0 if reverse else CHUNK - 1
            b = b_rows[d * n_chunks:(d + 1) * n_chunks]
            a = a_rows[d * n_chunks:(d + 1) * n_chunks]
            cm = cm_rows[d * n_chunks:(d + 1) * n_chunks]
            b_last = jnp.broadcast_to(b[:, last:last + 1], b.shape)
            cm_last = jnp.broadcast_to(cm[:, last:last + 1], b.shape)
            m_prev = _chunk_carry(b_last, cm_last, reverse)
            m_row = jnp.maximum(m_prev, cm)
            a_inter = jnp.exp(m_prev - m_row)
            exp_neg = jnp.exp(-(b + m_row))
            m_end = jnp.broadcast_to(m_row[:, last:last + 1], b.shape)
            wk = jnp.exp(a - m_end)
            decay = jnp.broadcast_to(a_inter[:, last:last + 1], b.shape)
            for r, val in ((ROW_A, a * LOG2_E), (ROW_M, m_row * LOG2_E), (ROW_AINTER, a_inter),
                           (ROW_EXPNEG, exp_neg), (ROW_WK, wk), (ROW_DECAY, decay)):
                coef_ref[slot, d, r] = val

    def passes(slot):
        def coef_row(d, r, c):
            return coef_ref[slot, d, r, pl.ds(c, 1), :]

        def chunk_operands(c):
            off = pl.multiple_of(c * CHUNK, CHUNK)
            return off, qt_ref[0, 0, :, pl.ds(off, CHUNK)], k_ref[0, 0, pl.ds(off, CHUNK), :], \
                vt_ref[0, 0, :, pl.ds(off, CHUNK)]

        key_pos = lax.broadcasted_iota(jnp.int32, (CHUNK, CHUNK), 0)
        qry_pos = lax.broadcasted_iota(jnp.int32, (CHUNK, CHUNK), 1)
        visible = (key_pos <= qry_pos, key_pos >= qry_pos)

        def weighted_scores(c):
            _, qt_c, k_c, _ = chunk_operands(c)
            scores = jnp.dot(k_c, qt_c, preferred_element_type=F32)
            for d in range(N_DIRS):
                a_col = jnp.broadcast_to(coef_row(d, ROW_A, c), (CHUNK, CHUNK)).T
                weight = jnp.exp2(jnp.where(visible[d], a_col - coef_row(d, ROW_M, c), -jnp.inf))
                st_ref[d, c] = (scores * weight).astype(BF16)

        def state_step(d, c, state):
            _, _, k_c, vt_c = chunk_operands(c)
            sprev_ref[d, c] = state.astype(BF16)
            vw = vt_c * coef_row(d, ROW_WK, c).astype(BF16)
            return coef_row(d, ROW_DECAY, c) * state + jnp.dot(vw, k_c, preferred_element_type=F32)

        def state_body(i, states):
            s_fwd, s_bwd = states
            for j in range(CHUNKS_PER_STEP):
                c = i * CHUNKS_PER_STEP + j
                weighted_scores(c)
                s_fwd = state_step(0, c, s_fwd)
                s_bwd = state_step(1, n_chunks - 1 - c, s_bwd)
            return s_fwd, s_bwd

        zero_state = jnp.zeros((AUG_ROWS, HEAD_DIM), F32)
        lax.fori_loop(0, n_chunks // CHUNKS_PER_STEP, state_body, (zero_state, zero_state))

        def output_chunk(c):
            off, qt_c, _, vt_c = chunk_operands(c)
            hsum = None
            for d in range(N_DIRS):
                qa = qt_c * coef_row(d, ROW_AINTER, c).astype(BF16)
                lhs = jnp.concatenate([vt_c, sprev_ref[d, c]], axis=1)
                rhs = jnp.concatenate([st_ref[d, c], qa], axis=0)
                numden = jnp.dot(lhs, rhs, preferred_element_type=F32)
                den = numden[HEAD_DIM:HEAD_DIM + 1, :]
                ht = numden[0:HEAD_DIM, :] * (1.0 / jnp.maximum(jnp.abs(den), coef_row(d, ROW_EXPNEG, c)))
                hsum = ht if hsum is None else hsum + ht
            out_ref[0, 0, pl.ds(off, CHUNK), :] = hsum.T.astype(BF16)

        def output_body(i, carry):
            for j in range(CHUNKS_PER_STEP):
                output_chunk(i * CHUNKS_PER_STEP + j)
            return carry

        lax.fori_loop(0, n_chunks // CHUNKS_PER_STEP, output_body, 0)

    @pl.when(step == 0)
    def _():
        prologue(0)
        weight_slabs()

    @pl.when(step == n_units)
    def _():
        passes((n_units - 1) % 2)
        weight_slabs()

    for fill in range(2):
        @pl.when((step > 0) & (step < n_units) & (step % 2 == fill))
        def _():
            passes(1 - fill)
            prologue(fill)
            weight_slabs()


def _mlstm(qt, k, vt, gates, gbias, w_pool, pool_scale, w_out, weights_to_cast):
    batch, heads, hd, seq = qt.shape
    n_chunks = seq // CHUNK
    assert n_chunks % CHUNKS_PER_STEP == 0
    n_units = batch * heads
    groups, gdim, _ = w_pool.shape
    d = w_out.shape[1]
    first_pool_block = (w_out.shape[0] - groups * gdim) // gdim
    units_per_group = n_units // groups
    fold_rows = gdim // units_per_group
    assert n_units % groups == 0 and fold_rows % BF16_SUBLANES == 0

    def pass_unit(g):
        u = jnp.maximum(g - 1, 0)
        return u // heads, u % heads, 0, 0

    def prologue_unit(g):
        return jnp.minimum(g, n_units - 1)

    spec_t = pl.BlockSpec((1, 1, hd, seq), pass_unit)
    spec_n = pl.BlockSpec((1, 1, seq, hd), pass_unit)
    spec_aug = pl.BlockSpec((1, 1, AUG_ROWS, seq), pass_unit)
    cast_specs = []
    for w in weights_to_cast:
        assert w.shape[0] % (n_units * BF16_SUBLANES) == 0
        cast_specs.append(pl.BlockSpec((w.shape[0] // n_units, w.shape[1]), lambda g: (prologue_unit(g), 0)))
    outs = pl.pallas_call(
        functools.partial(_mlstm_kernel, n_casts=len(weights_to_cast), n_units=n_units),
        grid=(n_units + 1,),
        in_specs=[spec_t, spec_n, spec_aug,
                  pl.BlockSpec((1, 1, N_GATE_ROWS, n_chunks, CHUNK),
                               lambda g: (prologue_unit(g) // heads, prologue_unit(g) % heads, 0, 0, 0)),
                  pl.BlockSpec((1, N_GATE_ROWS, CHUNK), lambda g: (prologue_unit(g) % heads, 0, 0)),
                  pl.BlockSpec((fold_rows, gdim), lambda g: (prologue_unit(g), 0)),
                  pl.BlockSpec((1, gdim), lambda g: (0, prologue_unit(g) // units_per_group)),
                  pl.BlockSpec((gdim, d), lambda g: (first_pool_block + prologue_unit(g) // units_per_group, 0))]
        + cast_specs,
        out_specs=[spec_n, pl.BlockSpec((fold_rows, d), lambda g: (prologue_unit(g), 0))] + cast_specs,
        out_shape=[jax.ShapeDtypeStruct((batch, heads, seq, hd), BF16),
                   jax.ShapeDtypeStruct((groups * gdim, d), BF16)]
        + [jax.ShapeDtypeStruct(w.shape, BF16) for w in weights_to_cast],
        scratch_shapes=[pltpu.VMEM((2, N_DIRS, N_COEF, n_chunks, CHUNK), F32),
                        pltpu.VMEM((N_DIRS, n_chunks, AUG_ROWS, hd), BF16),
                        pltpu.VMEM((N_DIRS, n_chunks, CHUNK, CHUNK), BF16)],
        compiler_params=pltpu.CompilerParams(dimension_semantics=("arbitrary",),
                                             vmem_limit_bytes=VMEM_LIMIT_BYTES),
    )(qt, k, vt, gates, gbias, w_pool.reshape(groups * gdim, gdim), pool_scale.reshape(1, groups * gdim), w_out,
      *weights_to_cast)
    return outs[0], outs[1], outs[2:]


def _outffn_kernel(x_ref, hm_ref, o_ref, u_ref, uprev_ref, unext_ref, mod_ref,
                   ghead_ref, gffn_ref, gfinal_ref,
                   band_ref, wout_ref, wfold_ref, wff1_ref, wff2_ref, out_ref,
                   ubuf_ref, ycat_ref, *, seq, n_tiles, final_norm):
    tm = x_ref.shape[1]
    dm = N_HEADS * HEAD_DIM
    step = pl.program_id(0)
    tiles_per_seq = seq // tm
    tile = step % tiles_per_seq

    def matmul_stage():
        gate1 = mod_ref[0, 2:3, :]
        yproj = (jnp.dot(ycat_ref[:, 0:dm], wout_ref[0:dm, :], preferred_element_type=F32)
                 + jnp.dot(ycat_ref[:, dm:], wfold_ref[...], preferred_element_type=F32))
        x1 = x_ref[0] + gate1 * yproj

        shift2 = mod_ref[0, 3:4, :]
        scale2 = mod_ref[0, 4:5, :]
        gate2 = mod_ref[0, 5:6, :]
        h2 = ((_rms_scale(x1) * gffn_ref[...]) * (1.0 + scale2) + shift2).astype(BF16)
        acc = jnp.zeros(x1.shape, F32)
        for j in range(wff1_ref.shape[1] // FF_COLS):
            hid = jnp.dot(h2, wff1_ref[:, j * FF_COLS:(j + 1) * FF_COLS], preferred_element_type=F32)
            hid = jnp.square(jnp.maximum(hid, 0.0)).astype(BF16)
            acc = acc + jnp.dot(hid, wff2_ref[j * FF_COLS:(j + 1) * FF_COLS, :], preferred_element_type=F32)
        x2 = x1 + gate2 * acc
        if final_norm:
            x2 = _rms_scale(x2) * gfinal_ref[...]
        out_ref[0] = x2

    def elementwise_stage():
        for hd in range(N_HEADS):
            cols = slice(hd * HEAD_DIM, (hd + 1) * HEAD_DIM)
            hn = _rms_scale(hm_ref[0, hd].astype(F32)) * ghead_ref[:, cols]
            gate = jax.nn.sigmoid(o_ref[0, :, cols].astype(F32))
            ycat_ref[:, cols] = (gate * hn).astype(BF16)

        ubuf_ref[0:POOL_HALO, :] = jnp.where(tile > 0, uprev_ref[0], jnp.zeros_like(uprev_ref[0]))
        ubuf_ref[POOL_HALO:POOL_HALO + tm, :] = u_ref[0]
        ubuf_ref[POOL_HALO + tm:, :] = jnp.where(tile < tiles_per_seq - 1, unext_ref[0],
                                                 jnp.zeros_like(unext_ref[0]))
        for gi in range(len(POOL_WINDOWS)):
            cols = slice(gi * HEAD_DIM, (gi + 1) * HEAD_DIM)
            ext = jnp.concatenate([ubuf_ref[r0:r0 + POOL_BLOCK + 2 * POOL_HALO, cols]
                                   for r0 in range(0, tm, POOL_BLOCK)], axis=1)
            mixed = jnp.dot(band_ref[gi], ext, preferred_element_type=F32).astype(BF16)
            for blk, r0 in enumerate(range(0, tm, POOL_BLOCK)):
                ycat_ref[r0:r0 + POOL_BLOCK, dm + gi * HEAD_DIM:dm + (gi + 1) * HEAD_DIM] = (
                    mixed[:, blk * HEAD_DIM:(blk + 1) * HEAD_DIM])

    def sequence_end_rows():
        def clipped_rows(r0):
            ext = ubuf_ref[r0:r0 + 3 * POOL_HALO, :].astype(F32)
            pos = tile * tm + r0 + lax.broadcasted_iota(jnp.int32, (POOL_HALO, HEAD_DIM), 0)
            for gi, win in enumerate(POOL_WINDOWS):
                half = win // 2
                grp = ext[:, gi * HEAD_DIM:(gi + 1) * HEAD_DIM]
                total = grp[POOL_HALO - half:2 * POOL_HALO - half]
                for j in range(1 - half, half):
                    total = total + grp[POOL_HALO + j:2 * POOL_HALO + j]
                cnt = jnp.minimum(pos + half, seq) - jnp.maximum(pos - half, 0)
                mixed = total / cnt.astype(F32) - grp[POOL_HALO:2 * POOL_HALO]
                ycat_ref[r0:r0 + POOL_HALO, dm + gi * HEAD_DIM:dm + (gi + 1) * HEAD_DIM] = mixed.astype(BF16)

        @pl.when(tile == 0)
        def _():
            clipped_rows(0)

        @pl.when(tile == tiles_per_seq - 1)
        def _():
            clipped_rows(tm - POOL_HALO)

    @pl.when(step == 0)
    def _():
        elementwise_stage()
        sequence_end_rows()

    @pl.when(step > 0)
    def _():
        matmul_stage()
        elementwise_stage()
        sequence_end_rows()


def _pool_bands():
    row = np.arange(POOL_BLOCK)[:, None]
    col = np.arange(POOL_BLOCK + 2 * POOL_HALO)[None, :] - POOL_HALO
    bands = [((col >= row - win // 2) & (col < row + win // 2)) / win - (col == row) for win in POOL_WINDOWS]
    return jnp.asarray(np.stack(bands), dtype=BF16)


def _outffn(x, hm, o, u, mod, g_head, g_ffn, g_final, w_out, w_fold, w_ff1, w_ff2, final_norm):
    batch, seq, d = x.shape
    tm = OUTFFN_ROWS
    dm = N_HEADS * HEAD_DIM
    d_pool = u.shape[2]
    halo_per_tile = tm // POOL_HALO
    n_halo_blocks = seq // POOL_HALO
    tiles_per_seq = seq // tm
    n_tiles = batch * tiles_per_seq

    def mm_tile(g):
        t = jnp.maximum(g - 1, 0)
        return t // tiles_per_seq, t % tiles_per_seq

    def ew_tile(g):
        t = jnp.minimum(g, n_tiles - 1)
        return t // tiles_per_seq, t % tiles_per_seq

    def mm_rows(g):
        b, i = mm_tile(g)
        return b, i, 0

    def ew_rows(g):
        b, i = ew_tile(g)
        return b, i, 0

    def ew_heads(g):
        b, i = ew_tile(g)
        return b, 0, i, 0

    def ew_prev(g):
        b, i = ew_tile(g)
        return b, jnp.maximum(i * halo_per_tile - 1, 0), 0

    def ew_next(g):
        b, i = ew_tile(g)
        return b, jnp.minimum((i + 1) * halo_per_tile, n_halo_blocks - 1), 0

    bands = _pool_bands()
    kernel = functools.partial(_outffn_kernel, seq=seq, n_tiles=n_tiles, final_norm=final_norm)
    return pl.pallas_call(
        kernel,
        grid=(n_tiles + 1,),
        in_specs=[pl.BlockSpec((1, tm, d), mm_rows),
                  pl.BlockSpec((1, N_HEADS, tm, HEAD_DIM), ew_heads),
                  pl.BlockSpec((1, tm, dm), ew_rows),
                  pl.BlockSpec((1, tm, d_pool), ew_rows),
                  pl.BlockSpec((1, POOL_HALO, d_pool), ew_prev),
                  pl.BlockSpec((1, POOL_HALO, d_pool), ew_next),
                  pl.BlockSpec((1,) + mod.shape[1:], lambda g: (mm_tile(g)[0], 0, 0)),
                  _const_spec((1, dm)),
                  _const_spec((1, d)),
                  _const_spec((1, d)),
                  _const_spec(bands.shape),
                  _const_spec(w_out.shape),
                  _const_spec(w_fold.shape),
                  _const_spec(w_ff1.shape),
                  _const_spec(w_ff2.shape)],
        out_specs=pl.BlockSpec((1, tm, d), mm_rows),
        out_shape=jax.ShapeDtypeStruct((batch, seq, d), F32),
        scratch_shapes=[pltpu.VMEM((tm + 2 * POOL_HALO, d_pool), BF16),
                        pltpu.VMEM((tm, dm + d_pool), BF16)],
        compiler_params=pltpu.CompilerParams(dimension_semantics=("arbitrary",),
                                             vmem_limit_bytes=VMEM_LIMIT_BYTES),
    )(x, hm, o, u, u, u, mod, g_head.reshape(1, dm),
      g_ffn.reshape(1, d), g_final.reshape(1, d), bands, w_out, w_fold, w_ff1, w_ff2)


def _pack_w_in(w_in):
    dm = N_HEADS * HEAD_DIM
    n_gates = N_DIRS * 2 * N_HEADS
    q, k, v, o = (w_in[:, i * dm:(i + 1) * dm] for i in range(4))
    gates = w_in[:, 4 * dm:4 * dm + n_gates]
    pool = w_in[:, 4 * dm + n_gates:]
    gates_t = gates.reshape(-1, N_DIRS, 2, N_HEADS).transpose(3, 1, 2, 0).reshape(n_gates, -1)
    cols = jnp.concatenate([k.T, o.T, pool.T], axis=0).astype(BF16)
    rows = jnp.concatenate([q.T, v.T, gates_t], axis=0).astype(BF16)
    return cols, rows


def _pack_gate_bias(b_igate, b_fgate):
    rows = jnp.stack([b_igate, b_fgate], axis=1)
    rows = rows.transpose(2, 0, 1).reshape(N_HEADS, N_GATE_ROWS)
    return jnp.broadcast_to(rows[:, :, None], (N_HEADS, N_GATE_ROWS, CHUNK)).astype(F32)


def kernel(x, c, w_ada, b_ada, g_mix, w_in, b_igate, b_fgate, g_head, w_pool, pool_scale, w_out,
           g_ffn, w_ff1, w_ff2, g_final):
    depth = w_ada.shape[0]
    batch, seq, d = x.shape
    d_pool = pool_scale.shape[1]
    assert g_head.shape[1] == N_HEADS * HEAD_DIM and w_pool.shape[1] == len(POOL_WINDOWS)
    assert seq % INPROJ_ROWS == 0 and seq % OUTFFN_ROWS == 0 and w_ff1.shape[2] % FF_COLS == 0
    for l in range(depth):
        mod = _adaln(c, w_ada[l], b_ada[l]).reshape(batch, 6, d)
        (qt, k, vt, o, u, gates), (w_out_b, w_ff1_b) = _inproj(
            x, mod, g_mix[l], *_pack_w_in(w_in[l]), d_pool, (w_out[l], w_ff1[l]))
        hm, w_fold, (w_ff2_b,) = _mlstm(
            qt, k, vt, gates, _pack_gate_bias(b_igate[l], b_fgate[l]), w_pool[l], pool_scale[l], w_out[l],
            (w_ff2[l],))
        x = _outffn(x, hm, o, u, mod, g_head[l], g_ffn[l], g_final,
                    w_out_b, w_fold, w_ff1_b, w_ff2_b, final_norm=(l == depth - 1))
    return x
```

```python
import functools

import jax
import jax.numpy as jnp
import numpy as np
from jax import lax
from jax.experimental import pallas as pl
from jax.experimental.pallas import tpu as pltpu

F32 = jnp.float32
BF16 = jnp.bfloat16

LANES = 128
BF16_SUBLANES = 16
VMEM_LIMIT_BYTES = 56 * 1024 * 1024

N_HEADS = 4
HEAD_DIM = 128
CHUNK = 128
POOL_WINDOWS = (2, 4, 8, 16)
POOL_BLOCK = 128
POOL_HALO = 16
N_DIRS = 2
N_GATE_ROWS = 2 * N_DIRS
EPS = 1e-6
LOG2_E = 1.4426950408889634

ADALN_ROWS = 128
INPROJ_ROWS = 1024
OUTFFN_ROWS = 512
FF_COLS = 1024
AUG_ROWS = HEAD_DIM + BF16_SUBLANES

ROW_A, ROW_M, ROW_AINTER, ROW_EXPNEG, ROW_WK, ROW_DECAY = range(6)
N_COEF = 6
CHUNKS_PER_STEP = 32


def _const_spec(shape):
    zeros = (0,) * len(shape)
    return pl.BlockSpec(shape, lambda *_: zeros, pipeline_mode=pl.Buffered(1))


def _rms_scale(x):
    return x * lax.rsqrt(jnp.mean(x * x, axis=-1, keepdims=True) + EPS)


def _modulated_norm_bf16(x, gain, shift, scale):
    slope = (gain * (1.0 + scale)).astype(BF16)
    return _rms_scale(x).astype(BF16) * slope + shift.astype(BF16)


def _adaln_kernel(c_ref, w_ref, b_ref, o_ref):
    @pl.when(pl.program_id(0) == 0)
    def _():
        o_ref[...] = jnp.broadcast_to(b_ref[...], o_ref.shape)

    c = c_ref[...]
    act = c * jax.nn.sigmoid(c)
    o_ref[...] += jnp.dot(act.astype(BF16), w_ref[...].astype(BF16), preferred_element_type=F32)


def _adaln(c, w_ada, b_ada):
    batch, d = c.shape
    n = w_ada.shape[1]
    rows = ADALN_ROWS
    assert d % rows == 0
    return pl.pallas_call(
        _adaln_kernel,
        grid=(d // rows,),
        in_specs=[pl.BlockSpec((batch, rows), lambda j: (0, j)),
                  pl.BlockSpec((rows, n), lambda j: (j, 0)),
                  pl.BlockSpec((1, n), lambda j: (0, 0))],
        out_specs=pl.BlockSpec((batch, n), lambda j: (0, 0)),
        out_shape=jax.ShapeDtypeStruct((batch, n), F32),
        compiler_params=pltpu.CompilerParams(dimension_semantics=("arbitrary",),
                                             vmem_limit_bytes=VMEM_LIMIT_BYTES),
    )(c, w_ada, b_ada.reshape(1, n))


def _inproj_kernel(x_ref, mod_ref, g_ref, w_ref, wt_ref, *refs, n_casts):
    cast_in, cast_out = refs[:n_casts], refs[n_casts + 6:]
    qt_ref, k_ref, vt_ref, o_ref, u_ref, gates_ref = refs[n_casts:n_casts + 6]
    for src, dst in zip(cast_in, cast_out):
        dst[...] = src[...].astype(BF16)

    dm = N_HEADS * HEAD_DIM
    x = x_ref[0]
    shift = mod_ref[0, 0:1, :]
    scale = mod_ref[0, 1:2, :]
    hb = _modulated_norm_bf16(x, g_ref[...], shift, scale)

    by_token = lax.dot_general(hb, w_ref[...], (((1,), (1,)), ((), ())), preferred_element_type=F32)
    for hd in range(N_HEADS):
        k_ref[0, hd] = by_token[:, hd * HEAD_DIM:(hd + 1) * HEAD_DIM].astype(BF16)
    o_ref[0] = by_token[:, dm:2 * dm].astype(BF16)
    u_ref[0] = by_token[:, 2 * dm:].astype(BF16)
    by_feature = lax.dot_general(wt_ref[...], hb, (((1,), (1,)), ((), ())), preferred_element_type=F32)
    for hd in range(N_HEADS):
        rows = slice(hd * HEAD_DIM, (hd + 1) * HEAD_DIM)
        qt_ref[0, hd] = (by_feature[rows, :] * (HEAD_DIM ** -0.5)).astype(BF16)
        vt_ref[0, hd, 0:HEAD_DIM, :] = by_feature[dm + hd * HEAD_DIM:dm + (hd + 1) * HEAD_DIM, :].astype(BF16)
        vt_ref[0, hd, HEAD_DIM:AUG_ROWS, :] = jnp.ones((AUG_ROWS - HEAD_DIM, x.shape[0]), BF16)
    gt = by_feature[2 * dm:, :]
    n_chunks = x.shape[0] // CHUNK
    for hd in range(N_HEADS):
        for r in range(N_GATE_ROWS):
            row = hd * N_GATE_ROWS + r
            for j in range(n_chunks):
                gates_ref[0, hd, r, j:j + 1, :] = gt[row:row + 1, j * CHUNK:(j + 1) * CHUNK]


def _inproj(x, mod, g_mix, w_cat, w_gates_t, d_pool, weights_to_cast):
    batch, seq, d = x.shape
    tm = INPROJ_ROWS
    dm = N_HEADS * HEAD_DIM
    grid = (batch, seq // tm)
    tiles_per_seq = seq // tm
    n_steps = batch * tiles_per_seq
    cast_specs = []
    for w in weights_to_cast:
        assert w.shape[0] % (n_steps * BF16_SUBLANES) == 0
        cast_specs.append(pl.BlockSpec((w.shape[0] // n_steps, w.shape[1]),
                                       lambda b, i: (b * tiles_per_seq + i, 0)))
    head_t = jax.ShapeDtypeStruct((batch, N_HEADS, HEAD_DIM, seq), BF16)
    head_n = jax.ShapeDtypeStruct((batch, N_HEADS, seq, HEAD_DIM), BF16)
    head_aug = jax.ShapeDtypeStruct((batch, N_HEADS, AUG_ROWS, seq), BF16)
    out_shape = (head_t, head_n, head_aug,
                 jax.ShapeDtypeStruct((batch, seq, dm), BF16),
                 jax.ShapeDtypeStruct((batch, seq, d_pool), BF16),
                 jax.ShapeDtypeStruct((batch, N_HEADS, N_GATE_ROWS, seq // CHUNK, CHUNK), F32))
    spec_t = pl.BlockSpec((1, N_HEADS, HEAD_DIM, tm), lambda b, i: (b, 0, 0, i))
    spec_n = pl.BlockSpec((1, N_HEADS, tm, HEAD_DIM), lambda b, i: (b, 0, i, 0))
    spec_aug = pl.BlockSpec((1, N_HEADS, AUG_ROWS, tm), lambda b, i: (b, 0, 0, i))
    out_specs = (spec_t, spec_n, spec_aug,
                 pl.BlockSpec((1, tm, dm), lambda b, i: (b, i, 0)),
                 pl.BlockSpec((1, tm, d_pool), lambda b, i: (b, i, 0)),
                 pl.BlockSpec((1, N_HEADS, N_GATE_ROWS, tm // CHUNK, CHUNK), lambda b, i: (b, 0, 0, i, 0)))
    outs = pl.pallas_call(
        functools.partial(_inproj_kernel, n_casts=len(weights_to_cast)),
        grid=grid,
        in_specs=[pl.BlockSpec((1, tm, d), lambda b, i: (b, i, 0)),
                  pl.BlockSpec((1,) + mod.shape[1:], lambda b, i: (b, 0, 0)),
                  _const_spec((1, d)),
                  _const_spec(w_cat.shape),
                  _const_spec(w_gates_t.shape)] + cast_specs,
        out_specs=list(out_specs) + cast_specs,
        out_shape=list(out_shape) + [jax.ShapeDtypeStruct(w.shape, BF16) for w in weights_to_cast],
        compiler_params=pltpu.CompilerParams(dimension_semantics=("arbitrary", "arbitrary"),
                                             vmem_limit_bytes=VMEM_LIMIT_BYTES),
    )(x, mod, g_mix.reshape(1, d), w_cat, w_gates_t, *weights_to_cast)
    return outs[:6], outs[6:]


def _time_scan(x, op, fill, reverse):
    row = lax.broadcasted_iota(jnp.int32, x.shape, 0)
    step = 1
    while step < CHUNK:
        if reverse:
            shifted = jnp.where(row < CHUNK - step, pltpu.roll(x, CHUNK - step, axis=0), fill)
        else:
            shifted = jnp.where(row >= step, pltpu.roll(x, step, axis=0), fill)
        x = op(x, shifted)
        step *= 2
    return x


def _chunk_carry(decay_log, peak, reverse):
    n = decay_log.shape[0]
    row = lax.broadcasted_iota(jnp.int32, decay_log.shape, 0)

    def earlier(x, dist, fill):
        if reverse:
            return jnp.where(row < n - dist, pltpu.roll(x, n - dist, axis=0), fill)
        return jnp.where(row >= dist, pltpu.roll(x, dist, axis=0), fill)

    p, q = decay_log, decay_log + peak
    dist = 1
    while dist < n:
        q = jnp.maximum(earlier(q, dist, -jnp.inf) + p, q)
        p = earlier(p, dist, 0.0) + p
        dist *= 2
    return earlier(jnp.maximum(p, q), 1, 0.0)


def _mlstm_kernel(qt_ref, k_ref, vt_ref, gates_ref, gbias_ref, wpool_ref, pscale_ref, woutp_ref, *refs,
                  n_casts, n_units):
    cast_in, out_ref, fold_ref = refs[:n_casts], refs[n_casts], refs[n_casts + 1]
    cast_out = refs[n_casts + 2:2 * n_casts + 2]
    coef_ref, sprev_ref, st_ref = refs[2 * n_casts + 2:]

    def weight_slabs():
        for src, dst in zip(cast_in, cast_out):
            dst[...] = src[...].astype(BF16)
        lhs = wpool_ref[...] * pscale_ref[...]
        rhs = woutp_ref[...]
        lhs_hi, rhs_hi = lhs.astype(BF16), rhs.astype(BF16)
        lhs_lo = (lhs - lhs_hi.astype(F32)).astype(BF16)
        rhs_lo = (rhs - rhs_hi.astype(F32)).astype(BF16)
        fold_ref[...] = (jnp.dot(lhs_hi, rhs_hi, preferred_element_type=F32)
                         + jnp.dot(lhs_hi, rhs_lo, preferred_element_type=F32)
                         + jnp.dot(lhs_lo, rhs_hi, preferred_element_type=F32)).astype(BF16)

    n_chunks = gates_ref.shape[3]
    assert 2 * N_DIRS * n_chunks == LANES
    step = pl.program_id(0)

    def prologue(slot):
        log_i, log_f = [], []
        for d in range(N_DIRS):
            log_i.append(gates_ref[0, 0, 2 * d] + gbias_ref[0, 2 * d:2 * d + 1, :])
            z = gates_ref[0, 0, 2 * d + 1] + gbias_ref[0, 2 * d + 1:2 * d + 2, :]
            log_f.append(-(jnp.maximum(-z, 0.0) + jnp.log1p(jnp.exp(-jnp.abs(z)))))
        by_time = jnp.concatenate(log_f + log_i, axis=0).T
        lane = lax.broadcasted_iota(jnp.int32, by_time.shape, 1)
        fwd_lane = (lane % (N_DIRS * n_chunks)) < n_chunks
        b_t = jnp.where(fwd_lane, _time_scan(by_time, jnp.add, 0.0, False),
                        _time_scan(by_time, jnp.add, 0.0, True))
        a_t = pltpu.roll(by_time, N_DIRS * n_chunks, axis=1) - b_t
        cm_t = jnp.where(fwd_lane, _time_scan(a_t, jnp.maximum, -jnp.inf, False),
                         _time_scan(a_t, jnp.maximum, -jnp.inf, True))
        b_rows, a_rows, cm_rows = b_t.T, a_t.T, cm_t.T

        for d in range(N_DIRS):
            reverse = d == 1
            last = 0 if reverse else CHUNK - 1
            b = b_rows[d * n_chunks:(d + 1) * n_chunks]
            a = a_rows[d * n_chunks:(d + 1) * n_chunks]
            cm = cm_rows[d * n_chunks:(d + 1) * n_chunks]
            b_last = jnp.broadcast_to(b[:, last:last + 1], b.shape)
            cm_last = jnp.broadcast_to(cm[:, last:last + 1], b.shape)
            m_prev = _chunk_carry(b_last, cm_last, reverse)
            m_row = jnp.maximum(m_prev, cm)
            a_inter = jnp.exp(m_prev - m_row)
            exp_neg = jnp.exp(-(b + m_row))
            m_end = jnp.broadcast_to(m_row[:, last:last + 1], b.shape)
            wk = jnp.exp(a - m_end)
            decay = jnp.broadcast_to(a_inter[:, last:last + 1], b.shape)
            for r, val in ((ROW_A, a * LOG2_E), (ROW_M, m_row * LOG2_E), (ROW_AINTER, a_inter),
                           (ROW_EXPNEG, exp_neg), (ROW_WK, wk), (ROW_DECAY, decay)):
                coef_ref[slot, d, r] = val

    def passes(slot):
        def coef_row(d, r, c):
            return coef_ref[slot, d, r, pl.ds(c, 1), :]

        def chunk_operands(c):
            off = pl.multiple_of(c * CHUNK, CHUNK)
            return off, qt_ref[0, 0, :, pl.ds(off, CHUNK)], k_ref[0, 0, pl.ds(off, CHUNK), :], \
                vt_ref[0, 0, :, pl.ds(off, CHUNK)]

        key_pos = lax.broadcasted_iota(jnp.int32, (CHUNK, CHUNK), 0)
        qry_pos = lax.broadcasted_iota(jnp.int32, (CHUNK, CHUNK), 1)
        visible = (key_pos <= qry_pos, key_pos >= qry_pos)

        def weighted_scores(c):
            _, qt_c, k_c, _ = chunk_operands(c)
            scores = jnp.dot(k_c, qt_c, preferred_element_type=F32)
            for d in range(N_DIRS):
                a_col = jnp.broadcast_to(coef_row(d, ROW_A, c), (CHUNK, CHUNK)).T
                weight = jnp.exp2(jnp.where(visible[d], a_col - coef_row(d, ROW_M, c), -jnp.inf))
                st_ref[d, c] = (scores * weight).astype(BF16)

        def state_step(d, c, state):
            _, _, k_c, vt_c = chunk_operands(c)
            sprev_ref[d, c] = state.astype(BF16)
            vw = vt_c * coef_row(d, ROW_WK, c).astype(BF16)
            return coef_row(d, ROW_DECAY, c) * state + jnp.dot(vw, k_c, preferred_element_type=F32)

        def state_body(i, states):
            s_fwd, s_bwd = states
            for j in range(CHUNKS_PER_STEP):
                c = i * CHUNKS_PER_STEP + j
                weighted_scores(c)
                s_fwd = state_step(0, c, s_fwd)
                s_bwd = state_step(1, n_chunks - 1 - c, s_bwd)
            return s_fwd, s_bwd

        zero_state = jnp.zeros((AUG_ROWS, HEAD_DIM), F32)
        lax.fori_loop(0, n_chunks // CHUNKS_PER_STEP, state_body, (zero_state, zero_state))

        def output_chunk(c):
            off, qt_c, _, vt_c = chunk_operands(c)
            hsum = None
            for d in range(N_DIRS):
                qa = qt_c * coef_row(d, ROW_AINTER, c).astype(BF16)
                lhs = jnp.concatenate([vt_c, sprev_ref[d, c]], axis=1)
                rhs = jnp.concatenate([st_ref[d, c], qa], axis=0)
                numden = jnp.dot(lhs, rhs, preferred_element_type=F32)
                den = numden[HEAD_DIM:HEAD_DIM + 1, :]
                ht = numden[0:HEAD_DIM, :] * (1.0 / jnp.maximum(jnp.abs(den), coef_row(d, ROW_EXPNEG, c)))
                hsum = ht if hsum is None else hsum + ht
            out_ref[0, 0, pl.ds(off, CHUNK), :] = hsum.T.astype(BF16)

        def output_body(i, carry):
            for j in range(CHUNKS_PER_STEP):
                output_chunk(i * CHUNKS_PER_STEP + j)
            return carry

        lax.fori_loop(0, n_chunks // CHUNKS_PER_STEP, output_body, 0)

    @pl.when(step == 0)
    def _():
        prologue(0)
        weight_slabs()

    @pl.when(step == n_units)
    def _():
        passes((n_units - 1) % 2)
        weight_slabs()

    for fill in range(2):
        @pl.when((step > 0) & (step < n_units) & (step % 2 == fill))
        def _():
            passes(1 - fill)
            prologue(fill)
            weight_slabs()


def _mlstm(qt, k, vt, gates, gbias, w_pool, pool_scale, w_out, weights_to_cast):
    batch, heads, hd, seq = qt.shape
    n_chunks = seq // CHUNK
    assert n_chunks % CHUNKS_PER_STEP == 0
    n_units = batch * heads
    groups, gdim, _ = w_pool.shape
    d = w_out.shape[1]
    first_pool_block = (w_out.shape[0] - groups * gdim) // gdim
    units_per_group = n_units // groups
    fold_rows = gdim // units_per_group
    assert n_units % groups == 0 and fold_rows % BF16_SUBLANES == 0

    def pass_unit(g):
        u = jnp.maximum(g - 1, 0)
        return u // heads, u % heads, 0, 0

    def prologue_unit(g):
        return jnp.minimum(g, n_units - 1)

    spec_t = pl.BlockSpec((1, 1, hd, seq), pass_unit)
    spec_n = pl.BlockSpec((1, 1, seq, hd), pass_unit)
    spec_aug = pl.BlockSpec((1, 1, AUG_ROWS, seq), pass_unit)
    cast_specs = []
    for w in weights_to_cast:
        assert w.shape[0] % (n_units * BF16_SUBLANES) == 0
        cast_specs.append(pl.BlockSpec((w.shape[0] // n_units, w.shape[1]), lambda g: (prologue_unit(g), 0)))
    outs = pl.pallas_call(
        functools.partial(_mlstm_kernel, n_casts=len(weights_to_cast), n_units=n_units),
        grid=(n_units + 1,),
        in_specs=[spec_t, spec_n, spec_aug,
                  pl.BlockSpec((1, 1, N_GATE_ROWS, n_chunks, CHUNK),
                               lambda g: (prologue_unit(g) // heads, prologue_unit(g) % heads, 0, 0, 0)),
                  pl.BlockSpec((1, N_GATE_ROWS, CHUNK), lambda g: (prologue_unit(g) % heads, 0, 0)),
                  pl.BlockSpec((fold_rows, gdim), lambda g: (prologue_unit(g), 0)),
                  pl.BlockSpec((1, gdim), lambda g: (0, prologue_unit(g) // units_per_group)),
                  pl.BlockSpec((gdim, d), lambda g: (first_pool_block + prologue_unit(g) // units_per_group, 0))]
        + cast_specs,
        out_specs=[spec_n, pl.BlockSpec((fold_rows, d), lambda g: (prologue_unit(g), 0))] + cast_specs,
        out_shape=[jax.ShapeDtypeStruct((batch, heads, seq, hd), BF16),
                   jax.ShapeDtypeStruct((groups * gdim, d), BF16)]
        + [jax.ShapeDtypeStruct(w.shape, BF16) for w in weights_to_cast],
        scratch_shapes=[pltpu.VMEM((2, N_DIRS, N_COEF, n_chunks, CHUNK), F32),
                        pltpu.VMEM((N_DIRS, n_chunks, AUG_ROWS, hd), BF16),
                        pltpu.VMEM((N_DIRS, n_chunks, CHUNK, CHUNK), BF16)],
        compiler_params=pltpu.CompilerParams(dimension_semantics=("arbitrary",),
                                             vmem_limit_bytes=VMEM_LIMIT_BYTES),
    )(qt, k, vt, gates, gbias, w_pool.reshape(groups * gdim, gdim), pool_scale.reshape(1, groups * gdim), w_out,
      *weights_to_cast)
    return outs[0], outs[1], outs[2:]


def _outffn_kernel(x_ref, hm_ref, o_ref, u_ref, uprev_ref, unext_ref, mod_ref,
                   ghead_ref, gffn_ref, gfinal_ref,
                   band_ref, wout_ref, wfold_ref, wff1_ref, wff2_ref, out_ref,
                   ubuf_ref, ycat_ref, *, seq, n_tiles, final_norm):
    tm = x_ref.shape[1]
    dm = N_HEADS * HEAD_DIM
    step = pl.program_id(0)
    tiles_per_seq = seq // tm
    tile = step % tiles_per_seq

    def matmul_stage():
        gate1 = mod_ref[0, 2:3, :]
        yproj = (jnp.dot(ycat_ref[:, 0:dm], wout_ref[0:dm, :], preferred_element_type=F32)
                 + jnp.dot(ycat_ref[:, dm:], wfold_ref[...], preferred_element_type=F32))
        x1 = x_ref[0] + gate1 * yproj

        shift2 = mod_ref[0, 3:4, :]
        scale2 = mod_ref[0, 4:5, :]
        gate2 = mod_ref[0, 5:6, :]
        h2 = ((_rms_scale(x1) * gffn_ref[...]) * (1.0 + scale2) + shift2).astype(BF16)
        acc = jnp.zeros(x1.shape, F32)
        for j in range(wff1_ref.shape[1] // FF_COLS):
            hid = jnp.dot(h2, wff1_ref[:, j * FF_COLS:(j + 1) * FF_COLS], preferred_element_type=F32)
            hid = jnp.square(jnp.maximum(hid, 0.0)).astype(BF16)
            acc = acc + jnp.dot(hid, wff2_ref[j * FF_COLS:(j + 1) * FF_COLS, :], preferred_element_type=F32)
        x2 = x1 + gate2 * acc
        if final_norm:
            x2 = _rms_scale(x2) * gfinal_ref[...]
        out_ref[0] = x2

    def elementwise_stage():
        for hd in range(N_HEADS):
            cols = slice(hd * HEAD_DIM, (hd + 1) * HEAD_DIM)
            hn = _rms_scale(hm_ref[0, hd].astype(F32)) * ghead_ref[:, cols]
            gate = jax.nn.sigmoid(o_ref[0, :, cols].astype(F32))
            ycat_ref[:, cols] = (gate * hn).astype(BF16)

        ubuf_ref[0:POOL_HALO, :] = jnp.where(tile > 0, uprev_ref[0], jnp.zeros_like(uprev_ref[0]))
        ubuf_ref[POOL_HALO:POOL_HALO + tm, :] = u_ref[0]
        ubuf_ref[POOL_HALO + tm:, :] = jnp.where(tile < tiles_per_seq - 1, unext_ref[0],
                                                 jnp.zeros_like(unext_ref[0]))
        for gi in range(len(POOL_WINDOWS)):
            cols = slice(gi * HEAD_DIM, (gi + 1) * HEAD_DIM)
            ext = jnp.concatenate([ubuf_ref[r0:r0 + POOL_BLOCK + 2 * POOL_HALO, cols]
                                   for r0 in range(0, tm, POOL_BLOCK)], axis=1)
            mixed = jnp.dot(band_ref[gi], ext, preferred_element_type=F32).astype(BF16)
            for blk, r0 in enumerate(range(0, tm, POOL_BLOCK)):
                ycat_ref[r0:r0 + POOL_BLOCK, dm + gi * HEAD_DIM:dm + (gi + 1) * HEAD_DIM] = (
                    mixed[:, blk * HEAD_DIM:(blk + 1) * HEAD_DIM])

    def sequence_end_rows():
        def clipped_rows(r0):
            ext = ubuf_ref[r0:r0 + 3 * POOL_HALO, :].astype(F32)
            pos = tile * tm + r0 + lax.broadcasted_iota(jnp.int32, (POOL_HALO, HEAD_DIM), 0)
            for gi, win in enumerate(POOL_WINDOWS):
                half = win // 2
                grp = ext[:, gi * HEAD_DIM:(gi + 1) * HEAD_DIM]
                total = grp[POOL_HALO - half:2 * POOL_HALO - half]
                for j in range(1 - half, half):
                    total = total + grp[POOL_HALO + j:2 * POOL_HALO + j]
                cnt = jnp.minimum(pos + half, seq) - jnp.maximum(pos - half, 0)
                mixed = total / cnt.astype(F32) - grp[POOL_HALO:2 * POOL_HALO]
                ycat_ref[r0:r0 + POOL_HALO, dm + gi * HEAD_DIM:dm + (gi + 1) * HEAD_DIM] = mixed.astype(BF16)

        @pl.when(tile == 0)
        def _():
            clipped_rows(0)

        @pl.when(tile == tiles_per_seq - 1)
        def _():
            clipped_rows(tm - POOL_HALO)

    @pl.when(step == 0)
    def _():
        elementwise_stage()
        sequence_end_rows()

    @pl.when(step > 0)
    def _():
        matmul_stage()
        elementwise_stage()
        sequence_end_rows()


def _pool_bands():
    row = np.arange(POOL_BLOCK)[:, None]
    col = np.arange(POOL_BLOCK + 2 * POOL_HALO)[None, :] - POOL_HALO
    bands = [((col >= row - win // 2) & (col < row + win // 2)) / win - (col == row) for win in POOL_WINDOWS]
    return jnp.asarray(np.stack(bands), dtype=BF16)


def _outffn(x, hm, o, u, mod, g_head, g_ffn, g_final, w_out, w_fold, w_ff1, w_ff2, final_norm):
    batch, seq, d = x.shape
    tm = OUTFFN_ROWS
    dm = N_HEADS * HEAD_DIM
    d_pool = u.shape[2]
    halo_per_tile = tm // POOL_HALO
    n_halo_blocks = seq // POOL_HALO
    tiles_per_seq = seq // tm
    n_tiles = batch * tiles_per_seq

    def mm_tile(g):
        t = jnp.maximum(g - 1, 0)
        return t // tiles_per_seq, t % tiles_per_seq

    def ew_tile(g):
        t = jnp.minimum(g, n_tiles - 1)
        return t // tiles_per_seq, t % tiles_per_seq

    def mm_rows(g):
        b, i = mm_tile(g)
        return b, i, 0

    def ew_rows(g):
        b, i = ew_tile(g)
        return b, i, 0

    def ew_heads(g):
        b, i = ew_tile(g)
        return b, 0, i, 0

    def ew_prev(g):
        b, i = ew_tile(g)
        return b, jnp.maximum(i * halo_per_tile - 1, 0), 0

    def ew_next(g):
        b, i = ew_tile(g)
        return b, jnp.minimum((i + 1) * halo_per_tile, n_halo_blocks - 1), 0

    bands = _pool_bands()
    kernel = functools.partial(_outffn_kernel, seq=seq, n_tiles=n_tiles, final_norm=final_norm)
    return pl.pallas_call(
        kernel,
        grid=(n_tiles + 1,),
        in_specs=[pl.BlockSpec((1, tm, d), mm_rows),
                  pl.BlockSpec((1, N_HEADS, tm, HEAD_DIM), ew_heads),
                  pl.BlockSpec((1, tm, dm), ew_rows),
                  pl.BlockSpec((1, tm, d_pool), ew_rows),
                  pl.BlockSpec((1, POOL_HALO, d_pool), ew_prev),
                  pl.BlockSpec((1, POOL_HALO, d_pool), ew_next),
                  pl.BlockSpec((1,) + mod.shape[1:], lambda g: (mm_tile(g)[0], 0, 0)),
                  _const_spec((1, dm)),
                  _const_spec((1, d)),
                  _const_spec((1, d)),
                  _const_spec(bands.shape),
                  _const_spec(w_out.shape),
                  _const_spec(w_fold.shape),
                  _const_spec(w_ff1.shape),
                  _const_spec(w_ff2.shape)],
        out_specs=pl.BlockSpec((1, tm, d), mm_rows),
        out_shape=jax.ShapeDtypeStruct((batch, seq, d), F32),
        scratch_shapes=[pltpu.VMEM((tm + 2 * POOL_HALO, d_pool), BF16),
                        pltpu.VMEM((tm, dm + d_pool), BF16)],
        compiler_params=pltpu.CompilerParams(dimension_semantics=("arbitrary",),
                                             vmem_limit_bytes=VMEM_LIMIT_BYTES),
    )(x, hm, o, u, u, u, mod, g_head.reshape(1, dm),
      g_ffn.reshape(1, d), g_final.reshape(1, d), bands, w_out, w_fold, w_ff1, w_ff2)


def _pack_w_in(w_in):
    dm = N_HEADS * HEAD_DIM
    n_gates = N_DIRS * 2 * N_HEADS
    q, k, v, o = (w_in[:, i * dm:(i + 1) * dm] for i in range(4))
    gates = w_in[:, 4 * dm:4 * dm + n_gates]
    pool = w_in[:, 4 * dm + n_gates:]
    gates_t = gates.reshape(-1, N_DIRS, 2, N_HEADS).transpose(3, 1, 2, 0).reshape(n_gates, -1)
    by_token_t = jnp.concatenate([k.T, o.T, pool.T], axis=0).astype(BF16)
    by_feature_t = jnp.concatenate([q.T, v.T, gates_t], axis=0).astype(BF16)
    return by_token_t, by_feature_t


def _pack_gate_bias(b_igate, b_fgate):
    rows = jnp.stack([b_igate, b_fgate], axis=1)
    rows = rows.transpose(2, 0, 1).reshape(N_HEADS, N_GATE_ROWS)
    return jnp.broadcast_to(rows[:, :, None], (N_HEADS, N_GATE_ROWS, CHUNK)).astype(F32)


def kernel(x, c, w_ada, b_ada, g_mix, w_in, b_igate, b_fgate, g_head, w_pool, pool_scale, w_out,
           g_ffn, w_ff1, w_ff2, g_final):
    depth = w_ada.shape[0]
    batch, seq, d = x.shape
    d_pool = pool_scale.shape[1]
    assert g_head.shape[1] == N_HEADS * HEAD_DIM and w_pool.shape[1] == len(POOL_WINDOWS)
    assert seq % INPROJ_ROWS == 0 and seq % OUTFFN_ROWS == 0 and w_ff1.shape[2] % FF_COLS == 0
    for l in range(depth):
        mod = _adaln(c, w_ada[l], b_ada[l]).reshape(batch, 6, d)
        (qt, k, vt, o, u, gates), (w_out_b, w_ff1_b) = _inproj(
            x, mod, g_mix[l], *_pack_w_in(w_in[l]), d_pool, (w_out[l], w_ff1[l]))
        hm, w_fold, (w_ff2_b,) = _mlstm(
            qt, k, vt, gates, _pack_gate_bias(b_igate[l], b_fgate[l]), w_pool[l], pool_scale[l], w_out[l],
            (w_ff2[l],))
        x = _outffn(x, hm, o, u, mod, g_head[l], g_ffn[l], g_final,
                    w_out_b, w_fold, w_ff1_b, w_ff2_b, final_norm=(l == depth - 1))
    return x
```
